```python
import jax, jax.numpy as jnp
from jax import lax
import numpy as np

D_MODEL = 1024
BATCH = 8
SEQ = 4096
DEPTH = 4

GRID_W = 64
CTX_LEN = 256

HEAD_DIM = 64
N_Q_HEADS = 8
N_KV_HEADS = 2
GQA_GROUP = N_Q_HEADS // N_KV_HEADS
WINDOW = 128
ATT_BLOCK = 128
ROPE_THETA = 10000.0
Q_W = N_Q_HEADS * HEAD_DIM
KV_W = N_KV_HEADS * HEAD_DIM

CHUNK = 128
SGU_GROUPS = 4
SGU_WIDTH = 512
SGU_GROUP_W = SGU_WIDTH // SGU_GROUPS

FNET_GROUPS = 4
FNET_WIDTH = 512
FNET_GROUP_W = FNET_WIDTH // FNET_GROUPS

BRANCH_W = 512
N_BRANCH = 3
IN_W = Q_W + 2 * KV_W + 2 * SGU_WIDTH + FNET_WIDTH
IN_SPLITS = (Q_W, Q_W + KV_W, Q_W + 2 * KV_W, Q_W + 2 * KV_W + SGU_WIDTH, Q_W + 2 * KV_W + 2 * SGU_WIDTH)

D_FF = 2752
N_MOD = 9
ALPHA = (2 * DEPTH) ** 0.25
BETA = (8 * DEPTH) ** -0.25
LN_EPS = 1e-5
NEG_INF = -1e30

kernel_name = 'hybrid_diffusion_gated_mixers'


def layer_norm(x, g, b):
    xf = x.astype(jnp.float32)
    mu = jnp.mean(xf, axis=-1, keepdims=True)
    var = jnp.mean(jnp.square(xf - mu), axis=-1, keepdims=True)
    return ((xf - mu) * lax.rsqrt(var + LN_EPS)).astype(x.dtype) * g + b


def modulate(x, shift, scale):
    return x * (1 + scale) + shift


def swiglu(h, w_up, w_down):
    gate, up = jnp.split(h @ w_up, 2, axis=-1)
    return (jax.nn.silu(gate) * up) @ w_down


def axial_rope_tables(n_tokens, dtype):
    rows = n_tokens // GRID_W
    row = jnp.broadcast_to(jnp.arange(rows)[:, None], (rows, GRID_W)).reshape(-1).astype(jnp.float32)
    col = jnp.broadcast_to(jnp.arange(GRID_W)[None, :], (rows, GRID_W)).reshape(-1).astype(jnp.float32)
    axis_dim = HEAD_DIM // 2
    inv_freq = ROPE_THETA ** (-jnp.arange(0, axis_dim, 2, dtype=jnp.float32) / axis_dim)
    ang_r = row[:, None] * inv_freq[None, :]
    ang_c = col[:, None] * inv_freq[None, :]
    return (jnp.cos(ang_r).astype(dtype), jnp.sin(ang_r).astype(dtype),
            jnp.cos(ang_c).astype(dtype), jnp.sin(ang_c).astype(dtype))


def _rotate(x, cos, sin):
    x1, x2 = jnp.split(x, 2, axis=-1)
    cos, sin = cos[:, None, :], sin[:, None, :]
    return jnp.concatenate([x1 * cos - x2 * sin, x1 * sin + x2 * cos], axis=-1)


def apply_axial_rope(x, tables):
    cos_r, sin_r, cos_c, sin_c = tables
    xr, xc = jnp.split(x, 2, axis=-1)
    return jnp.concatenate([_rotate(xr, cos_r, sin_r), _rotate(xc, cos_c, sin_c)], axis=-1)


def windowed_gqa_with_context(q_lat, k_lat, v_lat, k_ctx, v_ctx, sink):
    b, s = q_lat.shape[:2]
    n_ctx = k_ctx.shape[1]
    nb = s // ATT_BLOCK
    band_len = 3 * ATT_BLOCK
    scale = HEAD_DIM ** -0.5
    qb = q_lat.reshape(b, nb, ATT_BLOCK, N_KV_HEADS, GQA_GROUP, HEAD_DIM)

    def band(t):
        tp = jnp.pad(t, ((0, 0), (ATT_BLOCK, ATT_BLOCK), (0, 0), (0, 0)))
        tp = tp.reshape(b, nb + 2, ATT_BLOCK, N_KV_HEADS, HEAD_DIM)
        return jnp.concatenate([tp[:, :-2], tp[:, 1:-1], tp[:, 2:]], axis=2)

    kb, vb = band(k_lat), band(v_lat)
    q_pos = jnp.arange(s).reshape(nb, ATT_BLOCK)
    k_pos = (jnp.arange(nb)[:, None] - 1) * ATT_BLOCK + jnp.arange(band_len)[None, :]
    valid = ((jnp.abs(q_pos[:, :, None] - k_pos[:, None, :]) <= WINDOW)
             & (k_pos[:, None, :] >= 0) & (k_pos[:, None, :] < s))
    s_band = jnp.einsum('bnqkgd,bnjkd->bnkgqj', qb, kb).astype(jnp.float32) * scale
    s_band = jnp.where(valid[None, :, None, None], s_band, NEG_INF)
    s_ctx = jnp.einsum('bnqkgd,bjkd->bnkgqj', qb, k_ctx).astype(jnp.float32) * scale
    s_sink = jnp.broadcast_to(sink.astype(jnp.float32).reshape(1, 1, N_KV_HEADS, GQA_GROUP, 1, 1),
                              s_band.shape[:-1] + (1,))
    p = jax.nn.softmax(jnp.concatenate([s_band, s_ctx, s_sink], axis=-1), axis=-1).astype(v_lat.dtype)
    o = (jnp.einsum('bnkgqj,bnjkd->bnqkgd', p[..., :band_len], vb)
         + jnp.einsum('bnkgqj,bjkd->bnqkgd', p[..., band_len:band_len + n_ctx], v_ctx))
    return o.reshape(b, s, Q_W)


def context_gqa_with_sink(q_ctx, k_ctx, v_ctx, sink):
    b, n_ctx = q_ctx.shape[:2]
    qg = q_ctx.reshape(b, n_ctx, N_KV_HEADS, GQA_GROUP, HEAD_DIM)
    sc = jnp.einsum('bqkgd,bjkd->bkgqj', qg, k_ctx).astype(jnp.float32) * HEAD_DIM ** -0.5
    s_sink = jnp.broadcast_to(sink.astype(jnp.float32).reshape(1, N_KV_HEADS, GQA_GROUP, 1, 1),
                              sc.shape[:-1] + (1,))
    p = jax.nn.softmax(jnp.concatenate([sc, s_sink], axis=-1), axis=-1)[..., :n_ctx].astype(v_ctx.dtype)
    return jnp.einsum('bkgqj,bjkd->bqkgd', p, v_ctx).reshape(b, n_ctx, Q_W)


def chunked_sgu(u, v, w_s, b_s, g, bn):
    b, n = u.shape[:2]
    u = jax.nn.gelu(u)
    v = layer_norm(jax.nn.gelu(v), g, bn)
    vb = v.reshape(b, n // CHUNK, CHUNK, SGU_GROUPS, SGU_GROUP_W)
    mixed = jnp.einsum('gpq,bnqgc->bnpgc', w_s, vb) + b_s.T[None, None, :, :, None]
    return u * mixed.reshape(b, n, SGU_WIDTH)


def fourier_mix(f):
    b, n = f.shape[:2]
    fg = f.reshape(b, n, FNET_GROUPS, FNET_GROUP_W).astype(jnp.float32)
    y = jnp.fft.fftn(fg, axes=(1, 3), norm='ortho').real
    return y.astype(f.dtype).reshape(b, n, FNET_WIDTH)


def gated_merge(h, branches, w_gate, w_branch, w_out):
    merged = jax.nn.sigmoid(h @ w_gate[0]) * (branches[0] @ w_branch[0])
    for r in range(1, N_BRANCH):
        merged = merged + jax.nn.sigmoid(h @ w_gate[r]) * (branches[r] @ w_branch[r])
    return merged @ w_out


def mixer_layer(h_lat, h_ctx, w_in, attn_sink, sgu_w, sgu_b, sgu_ln_g, sgu_ln_b, w_gate, w_branch, w_out, need_ctx):
    b, s = h_lat.shape[:2]
    n_ctx = h_ctx.shape[1]
    q_l, k_l, v_l, u_l, z_l, f_l = jnp.split(h_lat @ w_in, IN_SPLITS, axis=-1)
    tables = axial_rope_tables(s, h_lat.dtype)
    q_l = apply_axial_rope(q_l.reshape(b, s, N_Q_HEADS, HEAD_DIM), tables)
    k_l = apply_axial_rope(k_l.reshape(b, s, N_KV_HEADS, HEAD_DIM), tables)
    v_l = v_l.reshape(b, s, N_KV_HEADS, HEAD_DIM)
    if need_ctx:
        q_c, k_c, v_c, u_c, z_c, f_c = jnp.split(h_ctx @ w_in, IN_SPLITS, axis=-1)
    else:
        k_c, v_c = jnp.split(h_ctx @ w_in[:, Q_W:Q_W + 2 * KV_W], 2, axis=-1)
    k_c = k_c.reshape(b, n_ctx, N_KV_HEADS, HEAD_DIM)
    v_c = v_c.reshape(b, n_ctx, N_KV_HEADS, HEAD_DIM)
    branches_lat = (windowed_gqa_with_context(q_l, k_l, v_l, k_c, v_c, attn_sink),
                    chunked_sgu(u_l, z_l, sgu_w, sgu_b, sgu_ln_g, sgu_ln_b),
                    fourier_mix(f_l))
    y_lat = gated_merge(h_lat, branches_lat, w_gate, w_branch, w_out)
    y_ctx = None
    if need_ctx:
        branches_ctx = (context_gqa_with_sink(q_c, k_c, v_c, attn_sink),
                        chunked_sgu(u_c, z_c, sgu_w, sgu_b, sgu_ln_g, sgu_ln_b),
                        fourier_mix(f_c))
        y_ctx = gated_merge(h_ctx, branches_ctx, w_gate, w_branch, w_out)
    return y_lat, y_ctx


def ffn_sublayer(x, mods, j, w_up, w_down, g, b):
    sub = swiglu(modulate(x, mods[3 * j], mods[3 * j + 1]), w_up, w_down)
    return layer_norm(ALPHA * x + 0.5 * mods[3 * j + 2] * sub, g, b)


def setup_inputs(seed: int = 0) -> dict:
    key = jax.random.key(seed)
    ks = jax.random.split(key, 19)
    nrm = lambda k, shape, s: jax.random.normal(k, shape, jnp.float32) * s
    return {
        'x': nrm(ks[0], (BATCH, SEQ, D_MODEL), 1.0),
        'c': nrm(ks[1], (BATCH, D_MODEL), 1.0),
        'ctx': nrm(ks[2], (BATCH, CTX_LEN, D_MODEL), 1.0),
        'c_ctx': nrm(ks[3], (D_MODEL,), 1.0),
        'w_mod': nrm(ks[4], (DEPTH, D_MODEL, N_MOD * D_MODEL), D_MODEL ** -0.5),
        'b_mod': nrm(ks[5], (DEPTH, N_MOD * D_MODEL), 0.02),
        'w_ffn_up': nrm(ks[6], (DEPTH, 2, D_MODEL, 2 * D_FF), D_MODEL ** -0.5),
        'w_ffn_down': nrm(ks[7], (DEPTH, 2, D_FF, D_MODEL), BETA * D_FF ** -0.5),
        'ln_g': 1.0 + nrm(ks[8], (DEPTH, 3, D_MODEL), 0.02),
        'ln_b': nrm(ks[9], (DEPTH, 3, D_MODEL), 0.02),
        'w_in': nrm(ks[10], (DEPTH, D_MODEL, IN_W), D_MODEL ** -0.5),
        'attn_sink': nrm(ks[11], (DEPTH, N_Q_HEADS), 0.5),
        'sgu_w': nrm(ks[12], (DEPTH, SGU_GROUPS, CHUNK, CHUNK), CHUNK ** -0.5),
        'sgu_b': 1.0 + nrm(ks[13], (DEPTH, SGU_GROUPS, CHUNK), 0.02),
        'sgu_ln_g': 1.0 + nrm(ks[14], (DEPTH, SGU_WIDTH), 0.02),
        'sgu_ln_b': nrm(ks[15], (DEPTH, SGU_WIDTH), 0.02),
        'w_gate': nrm(ks[16], (DEPTH, N_BRANCH, D_MODEL, D_MODEL), D_MODEL ** -0.5),
        'w_branch': nrm(ks[17], (DEPTH, N_BRANCH, BRANCH_W, D_MODEL), BRANCH_W ** -0.5),
        'w_out': nrm(ks[18], (DEPTH, D_MODEL, D_MODEL), BETA * D_MODEL ** -0.5),
    }


def reference(x, c, ctx, c_ctx, w_mod, b_mod, w_ffn_up, w_ffn_down, ln_g, ln_b, w_in, attn_sink,
              sgu_w, sgu_b, sgu_ln_g, sgu_ln_b, w_gate, w_branch, w_out):
    x_lat, x_ctx = x, ctx
    for i in range(DEPTH):
        last = i == DEPTH - 1
        m_lat = jnp.split((jax.nn.silu(c) @ w_mod[i] + b_mod[i])[:, None, :], N_MOD, axis=-1)
        m_ctx = jnp.split(jax.nn.silu(c_ctx) @ w_mod[i] + b_mod[i], N_MOD, axis=-1)
        x_lat = ffn_sublayer(x_lat, m_lat, 0, w_ffn_up[i, 0], w_ffn_down[i, 0], ln_g[i, 0], ln_b[i, 0])
        x_ctx = ffn_sublayer(x_ctx, m_ctx, 0, w_ffn_up[i, 0], w_ffn_down[i, 0], ln_g[i, 0], ln_b[i, 0])
        y_lat, y_ctx = mixer_layer(modulate(x_lat, m_lat[3], m_lat[4]), modulate(x_ctx, m_ctx[3], m_ctx[4]),
                                   w_in[i], attn_sink[i], sgu_w[i], sgu_b[i], sgu_ln_g[i], sgu_ln_b[i],
                                   w_gate[i], w_branch[i], w_out[i], not last)
        x_lat = layer_norm(ALPHA * x_lat + m_lat[5] * y_lat, ln_g[i, 1], ln_b[i, 1])
        x_lat = ffn_sublayer(x_lat, m_lat, 2, w_ffn_up[i, 1], w_ffn_down[i, 1], ln_g[i, 2], ln_b[i, 2])
        if not last:
            x_ctx = layer_norm(ALPHA * x_ctx + m_ctx[5] * y_ctx, ln_g[i, 1], ln_b[i, 1])
            x_ctx = ffn_sublayer(x_ctx, m_ctx, 2, w_ffn_up[i, 1], w_ffn_down[i, 1], ln_g[i, 2], ln_b[i, 2])
    return x_lat
```

```python
import functools
import math

import jax
import jax.numpy as jnp
from jax import lax
from jax.experimental import pallas as pl
from jax.experimental.pallas import tpu as pltpu

F32 = jnp.float32
BF16 = jnp.bfloat16

D_MODEL = 1024
GRID_W = 64
HEAD_DIM = 64
N_Q_HEADS = 8
N_KV_HEADS = 2
GQA_GROUP = N_Q_HEADS // N_KV_HEADS
ATT_BLOCK = 128
ROPE_THETA = 10000.0
Q_W = N_Q_HEADS * HEAD_DIM
KV_W = N_KV_HEADS * HEAD_DIM
CHUNK = 128
SGU_GROUPS = 4
SGU_WIDTH = 512
FNET_GROUPS = 4
FNET_WIDTH = 512
FNET_GROUP_W = FNET_WIDTH // FNET_GROUPS
N_BRANCH = 3
D_FF = 2752
N_MOD = 9
MODEL_DEPTH = 4
ALPHA = (2 * MODEL_DEPTH) ** 0.25
LN_EPS = 1e-5
NEG_INF = -1e30

LANES = 128
MXU_W = 256
D_FF_PAD = -(-D_FF // MXU_W) * MXU_W
KV_DUP_W = 2 * KV_W
IN_Q, IN_K, IN_V, IN_U, IN_Z, IN_F, IN_END = 0, 512, 768, 1024, 1536, 2048, 2560
VMEM_LIMIT = 56 * 1024 * 1024


def _params(n_axes):
    return pltpu.CompilerParams(dimension_semantics=("parallel",) * n_axes,
                                vmem_limit_bytes=VMEM_LIMIT)


def _const_spec(shape):
    nd = len(shape)
    return pl.BlockSpec(shape, lambda *_: (0,) * nd)


def _row_tile(s, pref):
    t = min(s, pref)
    while s % t:
        t -= LANES
    return t


def _dot(a, b):
    return jnp.dot(a, b, preferred_element_type=F32)


def _layer_norm(y, g, b):
    mu = jnp.mean(y, axis=-1, keepdims=True)
    d = y - mu
    var = jnp.mean(d * d, axis=-1, keepdims=True)
    return d * lax.rsqrt(var + LN_EPS) * g + b


def _mods_body(cc_ref, w_ref, b_ref, o_ref):
    s = cc_ref[...]
    s = (s * jax.nn.sigmoid(s)).astype(BF16)
    o_ref[...] = _dot(s, w_ref[...].astype(BF16)) + b_ref[...]


def _mods_call(cc, w_mod, b_mod):
    depth, _, n = w_mod.shape
    rows = cc.shape[0]
    tn = 1024
    return pl.pallas_call(
        _mods_body,
        grid=(depth, n // tn),
        in_specs=[
            pl.BlockSpec((rows, D_MODEL), lambda l, j: (0, 0)),
            pl.BlockSpec((None, D_MODEL, tn), lambda l, j: (l, 0, j)),
            pl.BlockSpec((None, 1, tn), lambda l, j: (l, 0, j)),
        ],
        out_specs=pl.BlockSpec((None, rows, tn), lambda l, j: (l, 0, j)),
        out_shape=jax.ShapeDtypeStruct((depth, rows, n), F32),
        compiler_params=_params(2),
        name="mods",
    )(cc, w_mod, b_mod.reshape(depth, 1, n))


def _ffn_body(x_ref, m_ref, wg_ref, wu_ref, wd_ref, g_ref, b_ref, o_ref, *, j):
    x = x_ref[...]
    shift = m_ref[3 * j:3 * j + 1, :]
    scale = m_ref[3 * j + 1:3 * j + 2, :]
    gate = m_ref[3 * j + 2:3 * j + 3, :]
    xm = (x * (1 + scale) + shift).astype(BF16)
    acc = None
    for c in range(D_FF_PAD // MXU_W):
        sl = slice(c * MXU_W, (c + 1) * MXU_W)
        gg = _dot(xm, wg_ref[:, sl])
        uu = _dot(xm, wu_ref[:, sl])
        h = (gg * jax.nn.sigmoid(gg) * uu).astype(BF16)
        d = _dot(h, wd_ref[sl, :])
        acc = d if acc is None else acc + d
    y = ALPHA * x + (0.5 * gate) * acc
    o_ref[...] = _layer_norm(y, g_ref[...], b_ref[...])


def _ffn_call(x, mods, wg, wu, wd, g, b, *, j, tm):
    bsz, s, _ = x.shape
    tm = _row_tile(s, tm)
    mrows = mods.shape[0]
    return pl.pallas_call(
        functools.partial(_ffn_body, j=j),
        grid=(bsz, s // tm),
        in_specs=[
            pl.BlockSpec((None, tm, D_MODEL), lambda bi, t: (bi, t, 0)),
            pl.BlockSpec((None, N_MOD, D_MODEL), lambda bi, t: (bi % mrows, 0, 0)),
            _const_spec(wg.shape), _const_spec(wu.shape), _const_spec(wd.shape),
            _const_spec(g.shape), _const_spec(b.shape),
        ],
        out_specs=pl.BlockSpec((None, tm, D_MODEL), lambda bi, t: (bi, t, 0)),
        out_shape=jax.ShapeDtypeStruct(x.shape, F32),
        compiler_params=_params(2),
        name="ffn",
    )(x, mods, wg, wu, wd, g, b)


def _gelu(x):
    return jax.nn.gelu(x)


def _inproj_body(*refs, rope):
    if rope:
        (x_ref, m_ref, w_ref, cos_ref, sin_ref, sw_ref, sb_ref, sg_ref, sbn_ref, dft_ref,
         q_ref, k_ref, v_ref, s_ref, ab_ref) = refs
    else:
        (x_ref, m_ref, w_ref, sw_ref, sb_ref, sg_ref, sbn_ref, dft_ref,
         q_ref, k_ref, v_ref, s_ref, ab_ref) = refs
    x = x_ref[...]
    tm = x.shape[0]
    xm = (x * (1 + m_ref[4:5, :]) + m_ref[3:4, :]).astype(BF16)

    if rope:
        cos = cos_ref[...]
        sin = sin_ref[...]
        lane = lax.broadcasted_iota(jnp.int32, (tm, LANES), 1)
        first = (lane % 32) < 16

        def rot(t):
            partner = jnp.where(first, pltpu.roll(t, LANES - 16, axis=1), pltpu.roll(t, 16, axis=1))
            return t * cos + partner * sin
    else:
        def rot(t):
            return t

    q = _dot(xm, w_ref[:, IN_Q:IN_K])
    for c in range(Q_W // LANES):
        sl = slice(c * LANES, (c + 1) * LANES)
        q_ref[:, sl] = (rot(q[:, sl]) * (HEAD_DIM ** -0.5)).astype(BF16)
    k = _dot(xm, w_ref[:, IN_K:IN_V])
    for c in range(KV_DUP_W // LANES):
        sl = slice(c * LANES, (c + 1) * LANES)
        k_ref[:, sl] = rot(k[:, sl]).astype(BF16)
    v_ref[...] = _dot(xm, w_ref[:, IN_V:IN_U]).astype(BF16)

    u = _gelu(_dot(xm, w_ref[:, IN_U:IN_Z]))
    z = _gelu(_dot(xm, w_ref[:, IN_Z:IN_F]))
    zn = _layer_norm(z, sg_ref[...], sbn_ref[...]).astype(BF16)
    for ci in range(tm // CHUNK):
        rs = slice(ci * CHUNK, (ci + 1) * CHUNK)
        for gi in range(SGU_GROUPS):
            cs = slice(gi * LANES, (gi + 1) * LANES)
            mixed = _dot(sw_ref[gi], zn[rs, cs]) + sb_ref[gi]
            s_ref[rs, cs] = (u[rs, cs] * mixed).astype(BF16)

    f = _dot(xm, w_ref[:, IN_F:IN_END]).astype(BF16)
    ab_ref[...] = _dot(f, dft_ref[...]).astype(BF16)


def _inproj_call(x, mods, w_in, tables, sw, sb, sg, sbn, dft_c, *, tm):
    bsz, s, _ = x.shape
    tm = _row_tile(s, tm)
    mrows = mods.shape[0]
    rope = tables is not None
    row = lambda w: pl.BlockSpec((None, tm, w), lambda bi, t: (bi, t, 0))
    in_specs = [
        row(D_MODEL),
        pl.BlockSpec((None, N_MOD, D_MODEL), lambda bi, t: (bi % mrows, 0, 0)),
        _const_spec(w_in.shape),
    ]
    args = [x, mods, w_in]
    if rope:
        in_specs += [pl.BlockSpec((tm, LANES), lambda bi, t: (t, 0))] * 2
        args += list(tables)
    in_specs += [_const_spec(a.shape) for a in (sw, sb, sg, sbn, dft_c)]
    args += [sw, sb, sg, sbn, dft_c]
    widths = (Q_W, KV_DUP_W, KV_DUP_W, SGU_WIDTH, 2 * FNET_WIDTH)
    return pl.pallas_call(
        functools.partial(_inproj_body, rope=rope),
        grid=(bsz, s // tm),
        in_specs=in_specs,
        out_specs=[row(w) for w in widths],
        out_shape=[jax.ShapeDtypeStruct((bsz, s, w), BF16) for w in widths],
        compiler_params=_params(2),
        name="inproj",
    )(*args)


def _attn_body(*refs, band, nb):
    if band:
        sink_ref, q_ref, kp_ref, kc_ref, kn_ref, vp_ref, vc_ref, vn_ref, kx_ref, vx_ref, o_ref = refs
        k_refs = (kp_ref, kc_ref, kn_ref, kx_ref)
        v_refs = (vp_ref, vc_ref, vn_ref, vx_ref)
    else:
        sink_ref, q_ref, kx_ref, vx_ref, o_ref = refs
        k_refs = (kx_ref,)
        v_refs = (vx_ref,)
    i = pl.program_id(1)
    blk = ATT_BLOCK
    row = lax.broadcasted_iota(jnp.int32, (blk, blk), 0)
    col = lax.broadcasted_iota(jnp.int32, (blk, blk), 1)
    lo = col < HEAD_DIM
    if band:
        prev_ok = jnp.logical_and(col >= row, i > 0)
        next_ok = jnp.logical_and(col <= row, i < nb - 1)
    zero = jnp.zeros((blk, blk), BF16)

    for kh in range(N_KV_HEADS):
        cs = slice(kh * LANES, (kh + 1) * LANES)
        kb = jnp.concatenate([r[:, cs] for r in k_refs], axis=0)
        vb = jnp.concatenate([r[:, cs] for r in v_refs], axis=0)
        qs = []
        for t in range(GQA_GROUP // 2):
            pair = q_ref[:, (2 * kh + t) * LANES:(2 * kh + t + 1) * LANES]
            qs.append(jnp.where(lo, pair, zero))
            qs.append(jnp.where(lo, zero, pair))
        qstack = jnp.concatenate(qs, axis=0)
        sc = lax.dot_general(qstack, kb, (((1,), (1,)), ((), ())), preferred_element_type=F32)
        n_keys = sc.shape[1]
        ps, dens = [], []
        for g in range(GQA_GROUP):
            s = sc[g * blk:(g + 1) * blk]
            pieces = [s[:, c * blk:(c + 1) * blk] for c in range(n_keys // blk)]
            if band:
                pieces[0] = jnp.where(prev_ok, pieces[0], NEG_INF)
                pieces[2] = jnp.where(next_ok, pieces[2], NEG_INF)
            sink = sink_ref[kh * GQA_GROUP + g]
            mx = pieces[0]
            for p in pieces[1:]:
                mx = jnp.maximum(mx, p)
            m = jnp.maximum(jnp.max(mx, axis=1, keepdims=True), sink)
            es = [jnp.exp(p - m) for p in pieces]
            tot = es[0]
            for e in es[1:]:
                tot = tot + e
            dens.append(jnp.sum(tot, axis=1, keepdims=True) + jnp.exp(sink - m))
            ps.append(jnp.concatenate(es, axis=1).astype(BF16))
        pall = jnp.concatenate(ps, axis=0)
        o = _dot(pall, vb)
        outs = [o[g * blk:(g + 1) * blk] / dens[g] for g in range(GQA_GROUP)]
        for t in range(GQA_GROUP // 2):
            pair = jnp.where(lo, outs[2 * t], outs[2 * t + 1])
            o_ref[:, (2 * kh + t) * LANES:(2 * kh + t + 1) * LANES] = pair.astype(BF16)


def _attn_call(sink, q, k, v, kx, vx, *, band):
    bsz, s, _ = q.shape
    n_ctx = kx.shape[1]
    nb = s // ATT_BLOCK
    qspec = pl.BlockSpec((None, ATT_BLOCK, Q_W), lambda bi, i: (bi, i, 0))
    xspec = pl.BlockSpec((None, n_ctx, KV_DUP_W), lambda bi, i: (bi, 0, 0))
    sspec = pl.BlockSpec(memory_space=pltpu.SMEM)
    if band:
        def kvspec(off):
            return pl.BlockSpec((None, ATT_BLOCK, KV_DUP_W),
                                lambda bi, i: (bi, jnp.clip(i + off, 0, nb - 1), 0))
        in_specs = [sspec, qspec] + [kvspec(o) for o in (-1, 0, 1)] * 2 + [xspec, xspec]
        args = (sink, q, k, k, k, v, v, v, kx, vx)
    else:
        in_specs = [sspec, qspec, xspec, xspec]
        args = (sink, q, kx, vx)
    return pl.pallas_call(
        functools.partial(_attn_body, band=band, nb=nb),
        grid=(bsz, nb),
        in_specs=in_specs,
        out_specs=qspec,
        out_shape=jax.ShapeDtypeStruct((bsz, s, Q_W), BF16),
        compiler_params=_params(2),
        name="attn_band" if band else "attn_ctx",
    )(*args)


def _fourier_body(cn_ref, nsn_ref, ab_ref, o_ref, *, scale):
    y = _dot(cn_ref[...], ab_ref[:, :FNET_WIDTH]) + _dot(nsn_ref[...], ab_ref[:, FNET_WIDTH:])
    o_ref[...] = (y * scale).astype(BF16)


def _fourier_call(cn, nsn, ab, *, tk):
    bsz, s, _ = ab.shape
    tk = _row_tile(s, tk)
    scale = 1.0 / math.sqrt(s * FNET_GROUP_W)
    return pl.pallas_call(
        functools.partial(_fourier_body, scale=scale),
        grid=(bsz, s // tk),
        in_specs=[
            pl.BlockSpec((tk, s), lambda bi, t: (t, 0)),
            pl.BlockSpec((tk, s), lambda bi, t: (t, 0)),
            pl.BlockSpec((None, s, 2 * FNET_WIDTH), lambda bi, t: (bi, 0, 0)),
        ],
        out_specs=pl.BlockSpec((None, tk, FNET_WIDTH), lambda bi, t: (bi, t, 0)),
        out_shape=jax.ShapeDtypeStruct((bsz, s, FNET_WIDTH), BF16),
        compiler_params=_params(2),
        name="fourier",
    )(cn, nsn, ab)


def _merge_body(x_ref, m_ref, a_ref, s_ref, c_ref, wg_ref, wb_ref, wo_ref, g_ref, b_ref, o_ref, mg_ref):
    x = x_ref[...]
    h = (x * (1 + m_ref[4:5, :]) + m_ref[3:4, :]).astype(BF16)
    branches = (a_ref[...], s_ref[...], c_ref[...])
    for n in range(D_MODEL // MXU_W):
        sl = slice(n * MXU_W, (n + 1) * MXU_W)
        acc = None
        for r in range(N_BRANCH):
            t = jax.nn.sigmoid(_dot(h, wg_ref[r, :, sl])) * _dot(branches[r], wb_ref[r, :, sl])
            acc = t if acc is None else acc + t
        mg_ref[:, sl] = acc.astype(BF16)
    y = _dot(mg_ref[...], wo_ref[...])
    o_ref[...] = _layer_norm(ALPHA * x + m_ref[5:6, :] * y, g_ref[...], b_ref[...])


def _merge_call(x, mods, a, sgu, c, wg, wb, wo, g, b, *, tm):
    bsz, s, _ = x.shape
    tm = _row_tile(s, tm)
    mrows = mods.shape[0]
    row = lambda w: pl.BlockSpec((None, tm, w), lambda bi, t: (bi, t, 0))
    return pl.pallas_call(
        _merge_body,
        grid=(bsz, s // tm),
        in_specs=[
            row(D_MODEL),
            pl.BlockSpec((None, N_MOD, D_MODEL), lambda bi, t: (bi % mrows, 0, 0)),
            row(Q_W), row(SGU_WIDTH), row(FNET_WIDTH),
            _const_spec(wg.shape), _const_spec(wb.shape), _const_spec(wo.shape),
            _const_spec(g.shape), _const_spec(b.shape),
        ],
        out_specs=row(D_MODEL),
        out_shape=jax.ShapeDtypeStruct(x.shape, F32),
        scratch_shapes=[pltpu.VMEM((tm, D_MODEL), BF16)],
        compiler_params=_params(2),
        name="merge",
    )(x, mods, a, sgu, c, wg, wb, wo, g, b)


def _rope_tables(s):
    pos = jnp.arange(s)
    row = (pos // GRID_W).astype(F32)
    col = (pos % GRID_W).astype(F32)
    axis_dim = HEAD_DIM // 2
    inv_freq = ROPE_THETA ** (-jnp.arange(0, axis_dim, 2, dtype=F32) / axis_dim)
    ang_r = row[:, None] * inv_freq[None, :]
    ang_c = col[:, None] * inv_freq[None, :]
    cos = jnp.concatenate([jnp.cos(ang_r)] * 2 + [jnp.cos(ang_c)] * 2, axis=1)
    sin = jnp.concatenate([-jnp.sin(ang_r), jnp.sin(ang_r), -jnp.sin(ang_c), jnp.sin(ang_c)], axis=1)
    return jnp.tile(cos, (1, 2)), jnp.tile(sin, (1, 2))


def _dft_angles(n):
    k = jnp.arange(n, dtype=jnp.int32)
    r = (k[:, None] * k[None, :]) % n
    return r.astype(F32) * (2.0 * math.pi / n)


def _position_dft(n):
    ang = _dft_angles(n)
    return jnp.cos(ang).astype(BF16), (-jnp.sin(ang)).astype(BF16)


def _channel_dft():
    ang = _dft_angles(FNET_GROUP_W)
    eye = jnp.eye(FNET_GROUPS, dtype=F32)
    return jnp.concatenate([jnp.kron(eye, jnp.cos(ang)), jnp.kron(eye, jnp.sin(ang))], axis=1).astype(BF16)


def _extended_w_in(w_in):
    def dup(w):
        h0, h1 = w[..., :HEAD_DIM], w[..., HEAD_DIM:]
        return jnp.concatenate([h0, h0, h1, h1], axis=-1)
    q = w_in[..., :Q_W]
    k = w_in[..., Q_W:Q_W + KV_W]
    v = w_in[..., Q_W + KV_W:Q_W + 2 * KV_W]
    rest = w_in[..., Q_W + 2 * KV_W:]
    return jnp.concatenate([q, dup(k), dup(v), rest], axis=-1).astype(BF16)


def kernel(x, c, ctx, c_ctx, w_mod, b_mod, w_ffn_up, w_ffn_down, ln_g, ln_b, w_in, attn_sink,
           sgu_w, sgu_b, sgu_ln_g, sgu_ln_b, w_gate, w_branch, w_out):
    depth = w_mod.shape[0]
    bsz, s, _ = x.shape
    n_ctx = ctx.shape[1]

    pad_ff = D_FF_PAD - D_FF
    w_g = jnp.pad(w_ffn_up[..., :D_FF], ((0, 0),) * 3 + ((0, pad_ff),)).astype(BF16)
    w_u = jnp.pad(w_ffn_up[..., D_FF:], ((0, 0),) * 3 + ((0, pad_ff),)).astype(BF16)
    w_d = jnp.pad(w_ffn_down, ((0, 0), (0, 0), (0, pad_ff), (0, 0))).astype(BF16)
    w_in_x = _extended_w_in(w_in)
    w_gate_b = w_gate.astype(BF16)
    w_branch_b = w_branch.astype(BF16)
    w_out_b = w_out.astype(BF16)
    sgu_w_b = sgu_w.astype(BF16)
    sgu_b_col = sgu_b[..., None]
    ln_g4 = ln_g[:, :, None, :]
    ln_b4 = ln_b[:, :, None, :]
    sgu_g3 = sgu_ln_g[:, None, :]
    sgu_b3 = sgu_ln_b[:, None, :]

    tables = _rope_tables(s)
    dft_c = _channel_dft()
    cn_lat, nsn_lat = _position_dft(s)
    cn_ctx, nsn_ctx = _position_dft(n_ctx)

    rows = -(-(bsz + 1) // 8) * 8
    cc = jnp.zeros((rows, D_MODEL), F32).at[:bsz].set(c).at[bsz].set(c_ctx)
    mods = _mods_call(cc, w_mod, b_mod).reshape(depth, rows, N_MOD, D_MODEL)

    x_lat, x_ctx = x, ctx
    for i in range(depth):
        last = i == depth - 1
        m_lat = mods[i, :bsz]
        m_ctx = mods[i, bsz:bsz + 1]
        ffn = lambda xx, mm, j, half, tm: _ffn_call(
            xx, mm, w_g[i, half], w_u[i, half], w_d[i, half], ln_g4[i, j], ln_b4[i, j], j=j, tm=tm)
        inproj = lambda xx, mm, tb, tm: _inproj_call(
            xx, mm, w_in_x[i], tb, sgu_w_b[i], sgu_b_col[i], sgu_g3[i], sgu_b3[i], dft_c, tm=tm)
        merge = lambda xx, mm, a, sg, cf, tm: _merge_call(
            xx, mm, a, sg, cf, w_gate_b[i], w_branch_b[i], w_out_b[i], ln_g4[i, 1], ln_b4[i, 1], tm=tm)

        x_lat = ffn(x_lat, m_lat, 0, 0, 512)
        x_ctx = ffn(x_ctx, m_ctx, 0, 0, 256)

        q_c, k_c, v_c, s_c, ab_c = inproj(x_ctx, m_ctx, None, 256)
        q_l, k_l, v_l, s_l, ab_l = inproj(x_lat, m_lat, tables, 512)

        a_l = _attn_call(attn_sink[i], q_l, k_l, v_l, k_c, v_c, band=True)
        f_l = _fourier_call(cn_lat, nsn_lat, ab_l, tk=512)
        x_lat = merge(x_lat, m_lat, a_l, s_l, f_l, 512)
        x_lat = ffn(x_lat, m_lat, 2, 1, 512)
        if not last:
            a_c = _attn_call(attn_sink[i], q_c, None, None, k_c, v_c, band=False)
            f_c = _fourier_call(cn_ctx, nsn_ctx, ab_c, tk=256)
            x_ctx = merge(x_ctx, m_ctx, a_c, s_c, f_c, 256)
            x_ctx = ffn(x_ctx, m_ctx, 2, 1, 256)
    return x_lat
```

```python
import functools
import math

import jax
import jax.numpy as jnp
from jax import lax
from jax.experimental import pallas as pl
from jax.experimental.pallas import tpu as pltpu

F32 = jnp.float32
BF16 = jnp.bfloat16

D_MODEL = 1024
GRID_W = 64
HEAD_DIM = 64
N_Q_HEADS = 8
N_KV_HEADS = 2
GQA_GROUP = N_Q_HEADS // N_KV_HEADS
ATT_BLOCK = 128
ROPE_THETA = 10000.0
Q_W = N_Q_HEADS * HEAD_DIM
KV_W = N_KV_HEADS * HEAD_DIM
CHUNK = 128
SGU_GROUPS = 4
SGU_WIDTH = 512
FNET_GROUPS = 4
FNET_WIDTH = 512
FNET_GROUP_W = FNET_WIDTH // FNET_GROUPS
N_BRANCH = 3
D_FF = 2752
N_MOD = 9
MODEL_DEPTH = 4
ALPHA = (2 * MODEL_DEPTH) ** 0.25
LN_EPS = 1e-5
NEG_INF = -1e30

LANES = 128
MXU_W = 256
D_FF_PAD = -(-D_FF // MXU_W) * MXU_W
KV_DUP_W = 2 * KV_W
IN_Q, IN_K, IN_V, IN_U, IN_Z, IN_F, IN_END = 0, 512, 768, 1024, 1536, 2048, 2560
VMEM_LIMIT = 56 * 1024 * 1024
N_RES = 32
K2_BLK = 16


def _params(n_axes):
    return pltpu.CompilerParams(dimension_semantics=("parallel",) * n_axes,
                                vmem_limit_bytes=VMEM_LIMIT)


def _const_spec(shape):
    nd = len(shape)
    return pl.BlockSpec(shape, lambda *_: (0,) * nd)


def _row_tile(s, pref):
    t = min(s, pref)
    while s % t:
        t -= LANES
    return t


def _dot(a, b):
    return jnp.dot(a, b, preferred_element_type=F32)


def _layer_norm(y, g, b):
    mu = jnp.mean(y, axis=-1, keepdims=True)
    d = y - mu
    var = jnp.mean(d * d, axis=-1, keepdims=True)
    return d * lax.rsqrt(var + LN_EPS) * g + b


def _mods_body(cc_ref, w_ref, b_ref, o_ref):
    s = cc_ref[...]
    s = (s * jax.nn.sigmoid(s)).astype(BF16)
    o_ref[...] = _dot(s, w_ref[...].astype(BF16)) + b_ref[...]


def _mods_call(cc, w_mod, b_mod):
    depth, _, n = w_mod.shape
    rows = cc.shape[0]
    tn = 1024
    return pl.pallas_call(
        _mods_body,
        grid=(depth, n // tn),
        in_specs=[
            pl.BlockSpec((rows, D_MODEL), lambda l, j: (0, 0)),
            pl.BlockSpec((None, D_MODEL, tn), lambda l, j: (l, 0, j)),
            pl.BlockSpec((None, 1, tn), lambda l, j: (l, 0, j)),
        ],
        out_specs=pl.BlockSpec((None, rows, tn), lambda l, j: (l, 0, j)),
        out_shape=jax.ShapeDtypeStruct((depth, rows, n), F32),
        compiler_params=_params(2),
        name="mods",
    )(cc, w_mod, b_mod.reshape(depth, 1, n))


def _ffn_body(x_ref, m_ref, wg_ref, wu_ref, wd_ref, g_ref, b_ref, o_ref, *, j):
    x = x_ref[...]
    shift = m_ref[3 * j:3 * j + 1, :]
    scale = m_ref[3 * j + 1:3 * j + 2, :]
    gate = m_ref[3 * j + 2:3 * j + 3, :]
    xm = (x * (1 + scale) + shift).astype(BF16)
    acc = None
    for c in range(D_FF_PAD // MXU_W):
        sl = slice(c * MXU_W, (c + 1) * MXU_W)
        gg = _dot(xm, wg_ref[:, sl])
        uu = _dot(xm, wu_ref[:, sl])
        h = (gg * jax.nn.sigmoid(gg) * uu).astype(BF16)
        d = _dot(h, wd_ref[sl, :])
        acc = d if acc is None else acc + d
    y = ALPHA * x + (0.5 * gate) * acc
    o_ref[...] = _layer_norm(y, g_ref[...], b_ref[...])


def _ffn_call(x, mods, wg, wu, wd, g, b, *, j, tm):
    bsz, s, _ = x.shape
    tm = _row_tile(s, tm)
    mrows = mods.shape[0]
    return pl.pallas_call(
        functools.partial(_ffn_body, j=j),
        grid=(bsz, s // tm),
        in_specs=[
            pl.BlockSpec((None, tm, D_MODEL), lambda bi, t: (bi, t, 0)),
            pl.BlockSpec((None, N_MOD, D_MODEL), lambda bi, t: (bi % mrows, 0, 0)),
            _const_spec(wg.shape), _const_spec(wu.shape), _const_spec(wd.shape),
            _const_spec(g.shape), _const_spec(b.shape),
        ],
        out_specs=pl.BlockSpec((None, tm, D_MODEL), lambda bi, t: (bi, t, 0)),
        out_shape=jax.ShapeDtypeStruct(x.shape, F32),
        compiler_params=_params(2),
        name="ffn",
    )(x, mods, wg, wu, wd, g, b)


def _gelu(x):
    return jax.nn.gelu(x)


def _inproj_body(*refs, rope):
    if rope:
        (x_ref, m_ref, w_ref, cos_ref, sin_ref, sw_ref, sb_ref, sg_ref, sbn_ref, dft_ref,
         q_ref, k_ref, v_ref, s_ref, ab_ref) = refs
    else:
        (x_ref, m_ref, w_ref, sw_ref, sb_ref, sg_ref, sbn_ref, dft_ref,
         q_ref, k_ref, v_ref, s_ref, ab_ref) = refs
    x = x_ref[...]
    tm = x.shape[0]
    xm = (x * (1 + m_ref[4:5, :]) + m_ref[3:4, :]).astype(BF16)

    if rope:
        cos = cos_ref[...]
        sin = sin_ref[...]
        lane = lax.broadcasted_iota(jnp.int32, (tm, LANES), 1)
        first = (lane % 32) < 16

        def rot(t):
            partner = jnp.where(first, pltpu.roll(t, LANES - 16, axis=1), pltpu.roll(t, 16, axis=1))
            return t * cos + partner * sin
    else:
        def rot(t):
            return t

    q = _dot(xm, w_ref[:, IN_Q:IN_K])
    for c in range(Q_W // LANES):
        sl = slice(c * LANES, (c + 1) * LANES)
        q_ref[:, sl] = (rot(q[:, sl]) * (HEAD_DIM ** -0.5)).astype(BF16)
    k = _dot(xm, w_ref[:, IN_K:IN_V])
    for c in range(KV_DUP_W // LANES):
        sl = slice(c * LANES, (c + 1) * LANES)
        k_ref[:, sl] = rot(k[:, sl]).astype(BF16)
    v_ref[...] = _dot(xm, w_ref[:, IN_V:IN_U]).astype(BF16)

    u = _gelu(_dot(xm, w_ref[:, IN_U:IN_Z]))
    z = _gelu(_dot(xm, w_ref[:, IN_Z:IN_F]))
    zn = _layer_norm(z, sg_ref[...], sbn_ref[...]).astype(BF16)
    for ci in range(tm // CHUNK):
        rs = slice(ci * CHUNK, (ci + 1) * CHUNK)
        for gi in range(SGU_GROUPS):
            cs = slice(gi * LANES, (gi + 1) * LANES)
            mixed = _dot(sw_ref[gi], zn[rs, cs]) + sb_ref[gi]
            s_ref[rs, cs] = (u[rs, cs] * mixed).astype(BF16)

    f = _dot(xm, w_ref[:, IN_F:IN_END]).astype(BF16)
    ab_ref[...] = _dot(f, dft_ref[...]).astype(BF16)


def _inproj_call(x, mods, w_in, tables, sw, sb, sg, sbn, dft_c, *, tm):
    bsz, s, _ = x.shape
    tm = _row_tile(s, tm)
    mrows = mods.shape[0]
    rope = tables is not None
    row = lambda w: pl.BlockSpec((None, tm, w), lambda bi, t: (bi, t, 0))
    in_specs = [
        row(D_MODEL),
        pl.BlockSpec((None, N_MOD, D_MODEL), lambda bi, t: (bi % mrows, 0, 0)),
        _const_spec(w_in.shape),
    ]
    args = [x, mods, w_in]
    if rope:
        in_specs += [pl.BlockSpec((tm, LANES), lambda bi, t: (t, 0))] * 2
        args += list(tables)
    in_specs += [_const_spec(a.shape) for a in (sw, sb, sg, sbn, dft_c)]
    args += [sw, sb, sg, sbn, dft_c]
    widths = (Q_W, KV_DUP_W, KV_DUP_W, SGU_WIDTH, 2 * FNET_WIDTH)
    return pl.pallas_call(
        functools.partial(_inproj_body, rope=rope),
        grid=(bsz, s // tm),
        in_specs=in_specs,
        out_specs=[row(w) for w in widths],
        out_shape=[jax.ShapeDtypeStruct((bsz, s, w), BF16) for w in widths],
        compiler_params=_params(2),
        name="inproj",
    )(*args)


def _attn_body(*refs, band, nb):
    if band:
        sink_ref, q_ref, kp_ref, kc_ref, kn_ref, vp_ref, vc_ref, vn_ref, kx_ref, vx_ref, o_ref = refs
        k_refs = (kp_ref, kc_ref, kn_ref, kx_ref)
        v_refs = (vp_ref, vc_ref, vn_ref, vx_ref)
    else:
        sink_ref, q_ref, kx_ref, vx_ref, o_ref = refs
        k_refs = (kx_ref,)
        v_refs = (vx_ref,)
    i = pl.program_id(1)
    blk = ATT_BLOCK
    row = lax.broadcasted_iota(jnp.int32, (blk, blk), 0)
    col = lax.broadcasted_iota(jnp.int32, (blk, blk), 1)
    lo = col < HEAD_DIM
    if band:
        prev_ok = jnp.logical_and(col >= row, i > 0)
        next_ok = jnp.logical_and(col <= row, i < nb - 1)
    zero = jnp.zeros((blk, blk), BF16)

    for kh in range(N_KV_HEADS):
        cs = slice(kh * LANES, (kh + 1) * LANES)
        kb = jnp.concatenate([r[:, cs] for r in k_refs], axis=0)
        vb = jnp.concatenate([r[:, cs] for r in v_refs], axis=0)
        qs = []
        for t in range(GQA_GROUP // 2):
            pair = q_ref[:, (2 * kh + t) * LANES:(2 * kh + t + 1) * LANES]
            qs.append(jnp.where(lo, pair, zero))
            qs.append(jnp.where(lo, zero, pair))
        qstack = jnp.concatenate(qs, axis=0)
        sc = lax.dot_general(qstack, kb, (((1,), (1,)), ((), ())), preferred_element_type=F32)
        n_keys = sc.shape[1]
        ps, dens = [], []
        for g in range(GQA_GROUP):
            s = sc[g * blk:(g + 1) * blk]
            pieces = [s[:, c * blk:(c + 1) * blk] for c in range(n_keys // blk)]
            if band:
                pieces[0] = jnp.where(prev_ok, pieces[0], NEG_INF)
                pieces[2] = jnp.where(next_ok, pieces[2], NEG_INF)
            sink = sink_ref[kh * GQA_GROUP + g]
            mx = pieces[0]
            for p in pieces[1:]:
                mx = jnp.maximum(mx, p)
            m = jnp.maximum(jnp.max(mx, axis=1, keepdims=True), sink)
            es = [jnp.exp(p - m) for p in pieces]
            tot = es[0]
            for e in es[1:]:
                tot = tot + e
            dens.append(jnp.sum(tot, axis=1, keepdims=True) + jnp.exp(sink - m))
            ps.append(jnp.concatenate(es, axis=1).astype(BF16))
        pall = jnp.concatenate(ps, axis=0)
        o = _dot(pall, vb)
        outs = [o[g * blk:(g + 1) * blk] / dens[g] for g in range(GQA_GROUP)]
        for t in range(GQA_GROUP // 2):
            pair = jnp.where(lo, outs[2 * t], outs[2 * t + 1])
            o_ref[:, (2 * kh + t) * LANES:(2 * kh + t + 1) * LANES] = pair.astype(BF16)


def _attn_call(sink, q, k, v, kx, vx, *, band):
    bsz, s, _ = q.shape
    n_ctx = kx.shape[1]
    nb = s // ATT_BLOCK
    qspec = pl.BlockSpec((None, ATT_BLOCK, Q_W), lambda bi, i: (bi, i, 0))
    xspec = pl.BlockSpec((None, n_ctx, KV_DUP_W), lambda bi, i: (bi, 0, 0))
    sspec = pl.BlockSpec(memory_space=pltpu.SMEM)
    if band:
        def kvspec(off):
            return pl.BlockSpec((None, ATT_BLOCK, KV_DUP_W),
                                lambda bi, i: (bi, jnp.clip(i + off, 0, nb - 1), 0))
        in_specs = [sspec, qspec] + [kvspec(o) for o in (-1, 0, 1)] * 2 + [xspec, xspec]
        args = (sink, q, k, k, k, v, v, v, kx, vx)
    else:
        in_specs = [sspec, qspec, xspec, xspec]
        args = (sink, q, kx, vx)
    return pl.pallas_call(
        functools.partial(_attn_body, band=band, nb=nb),
        grid=(bsz, nb),
        in_specs=in_specs,
        out_specs=qspec,
        out_shape=jax.ShapeDtypeStruct((bsz, s, Q_W), BF16),
        compiler_params=_params(2),
        name="attn_band" if band else "attn_ctx",
    )(*args)


def _fourier_body(cn_ref, nsn_ref, ab_ref, o_ref, *, scale):
    y = _dot(cn_ref[...], ab_ref[:, :FNET_WIDTH]) + _dot(nsn_ref[...], ab_ref[:, FNET_WIDTH:])
    o_ref[...] = (y * scale).astype(BF16)


def _fourier_call(cn, nsn, ab, *, tk):
    bsz, s, _ = ab.shape
    tk = _row_tile(s, tk)
    scale = 1.0 / math.sqrt(s * FNET_GROUP_W)
    return pl.pallas_call(
        functools.partial(_fourier_body, scale=scale),
        grid=(bsz, s // tk),
        in_specs=[
            pl.BlockSpec((tk, s), lambda bi, t: (t, 0)),
            pl.BlockSpec((tk, s), lambda bi, t: (t, 0)),
            pl.BlockSpec((None, s, 2 * FNET_WIDTH), lambda bi, t: (bi, 0, 0)),
        ],
        out_specs=pl.BlockSpec((None, tk, FNET_WIDTH), lambda bi, t: (bi, t, 0)),
        out_shape=jax.ShapeDtypeStruct((bsz, s, FNET_WIDTH), BF16),
        compiler_params=_params(2),
        name="fourier",
    )(cn, nsn, ab)


def _fourier_fact_body(ab_ref, m1_ref, g_ref, o_ref, t_ref, *, n_low, scale):
    w = FNET_WIDTH
    for j in range(N_RES):
        blk = ab_ref[:, j * 2 * w:(j + 1) * 2 * w]
        d = jnp.concatenate([blk[:, :w], blk[:, w:]], axis=0)
        r = _dot(m1_ref[j], d)
        rows = slice(j * n_low, (j + 1) * n_low)
        t_ref[rows, :w] = r[:n_low].astype(BF16)
        t_ref[rows, w:] = r[n_low:].astype(BF16)
    for kb in range(n_low // K2_BLK):
        x = jnp.concatenate(
            [t_ref[j * n_low + kb * K2_BLK:j * n_low + (kb + 1) * K2_BLK, :] for j in range(N_RES)], axis=0)
        y = _dot(g_ref[0], x[:, :w]) + _dot(g_ref[1], x[:, w:])
        for k1 in range(N_RES):
            o_ref[k1 * n_low + kb * K2_BLK:k1 * n_low + (kb + 1) * K2_BLK, :] = (
                y[k1 * K2_BLK:(k1 + 1) * K2_BLK] * scale).astype(BF16)


def _fourier_fact_call(m1, g, ab):
    bsz, s, _ = ab.shape
    n_low = s // N_RES
    scale = 1.0 / math.sqrt(s * FNET_GROUP_W)
    ab_view = ab.reshape(bsz, n_low, N_RES * 2 * FNET_WIDTH)
    return pl.pallas_call(
        functools.partial(_fourier_fact_body, n_low=n_low, scale=scale),
        grid=(bsz,),
        in_specs=[
            pl.BlockSpec((None, n_low, N_RES * 2 * FNET_WIDTH), lambda bi: (bi, 0, 0)),
            _const_spec(m1.shape), _const_spec(g.shape),
        ],
        out_specs=pl.BlockSpec((None, s, FNET_WIDTH), lambda bi: (bi, 0, 0)),
        out_shape=jax.ShapeDtypeStruct((bsz, s, FNET_WIDTH), BF16),
        scratch_shapes=[pltpu.VMEM((s, 2 * FNET_WIDTH), BF16)],
        compiler_params=_params(1),
        name="fourier_fact",
    )(ab_view, m1, g)


def _merge_body(x_ref, m_ref, a_ref, s_ref, c_ref, wg_ref, wb_ref, wo_ref, g_ref, b_ref, o_ref, mg_ref):
    x = x_ref[...]
    h = (x * (1 + m_ref[4:5, :]) + m_ref[3:4, :]).astype(BF16)
    branches = (a_ref[...], s_ref[...], c_ref[...])
    for n in range(D_MODEL // MXU_W):
        sl = slice(n * MXU_W, (n + 1) * MXU_W)
        acc = None
        for r in range(N_BRANCH):
            t = jax.nn.sigmoid(_dot(h, wg_ref[r, :, sl])) * _dot(branches[r], wb_ref[r, :, sl])
            acc = t if acc is None else acc + t
        mg_ref[:, sl] = acc.astype(BF16)
    y = _dot(mg_ref[...], wo_ref[...])
    o_ref[...] = _layer_norm(ALPHA * x + m_ref[5:6, :] * y, g_ref[...], b_ref[...])


def _merge_call(x, mods, a, sgu, c, wg, wb, wo, g, b, *, tm):
    bsz, s, _ = x.shape
    tm = _row_tile(s, tm)
    mrows = mods.shape[0]
    row = lambda w: pl.BlockSpec((None, tm, w), lambda bi, t: (bi, t, 0))
    return pl.pallas_call(
        _merge_body,
        grid=(bsz, s // tm),
        in_specs=[
            row(D_MODEL),
            pl.BlockSpec((None, N_MOD, D_MODEL), lambda bi, t: (bi % mrows, 0, 0)),
            row(Q_W), row(SGU_WIDTH), row(FNET_WIDTH),
            _const_spec(wg.shape), _const_spec(wb.shape), _const_spec(wo.shape),
            _const_spec(g.shape), _const_spec(b.shape),
        ],
        out_specs=row(D_MODEL),
        out_shape=jax.ShapeDtypeStruct(x.shape, F32),
        scratch_shapes=[pltpu.VMEM((tm, D_MODEL), BF16)],
        compiler_params=_params(2),
        name="merge",
    )(x, mods, a, sgu, c, wg, wb, wo, g, b)


def _rope_tables(s):
    pos = jnp.arange(s)
    row = (pos // GRID_W).astype(F32)
    col = (pos % GRID_W).astype(F32)
    axis_dim = HEAD_DIM // 2
    inv_freq = ROPE_THETA ** (-jnp.arange(0, axis_dim, 2, dtype=F32) / axis_dim)
    ang_r = row[:, None] * inv_freq[None, :]
    ang_c = col[:, None] * inv_freq[None, :]
    cos = jnp.concatenate([jnp.cos(ang_r)] * 2 + [jnp.cos(ang_c)] * 2, axis=1)
    sin = jnp.concatenate([-jnp.sin(ang_r), jnp.sin(ang_r), -jnp.sin(ang_c), jnp.sin(ang_c)], axis=1)
    return jnp.tile(cos, (1, 2)), jnp.tile(sin, (1, 2))


def _dft_angles(n):
    k = jnp.arange(n, dtype=jnp.int32)
    r = (k[:, None] * k[None, :]) % n
    return r.astype(F32) * (2.0 * math.pi / n)


def _position_dft(n):
    ang = _dft_angles(n)
    return jnp.cos(ang).astype(BF16), (-jnp.sin(ang)).astype(BF16)


def _factored_dft(s):
    n_low = s // N_RES
    idx = jnp.arange(n_low, dtype=jnp.int32)
    n = jnp.arange(N_RES, dtype=jnp.int32)[:, None, None] + N_RES * idx[None, None, :]
    ang = ((idx[None, :, None] * n) % s).astype(F32) * (2.0 * math.pi / s)
    c, sn = jnp.cos(ang), jnp.sin(ang)
    m1 = jnp.concatenate([jnp.concatenate([c, -sn], axis=2), jnp.concatenate([sn, c], axis=2)], axis=1)
    ang_r = _dft_angles(N_RES)
    eye = jnp.eye(K2_BLK, dtype=F32)
    g = jnp.stack([jnp.kron(jnp.cos(ang_r), eye), -jnp.kron(jnp.sin(ang_r), eye)])
    return m1.astype(BF16), g.astype(BF16)


def _channel_dft():
    ang = _dft_angles(FNET_GROUP_W)
    eye = jnp.eye(FNET_GROUPS, dtype=F32)
    return jnp.concatenate([jnp.kron(eye, jnp.cos(ang)), jnp.kron(eye, jnp.sin(ang))], axis=1).astype(BF16)


def _extended_w_in(w_in):
    def dup(w):
        h0, h1 = w[..., :HEAD_DIM], w[..., HEAD_DIM:]
        return jnp.concatenate([h0, h0, h1, h1], axis=-1)
    q = w_in[..., :Q_W]
    k = w_in[..., Q_W:Q_W + KV_W]
    v = w_in[..., Q_W + KV_W:Q_W + 2 * KV_W]
    rest = w_in[..., Q_W + 2 * KV_W:]
    return jnp.concatenate([q, dup(k), dup(v), rest], axis=-1).astype(BF16)


def kernel(x, c, ctx, c_ctx, w_mod, b_mod, w_ffn_up, w_ffn_down, ln_g, ln_b, w_in, attn_sink,
           sgu_w, sgu_b, sgu_ln_g, sgu_ln_b, w_gate, w_branch, w_out):
    depth = w_mod.shape[0]
    bsz, s, _ = x.shape
    n_ctx = ctx.shape[1]

    pad_ff = D_FF_PAD - D_FF
    w_g = jnp.pad(w_ffn_up[..., :D_FF], ((0, 0),) * 3 + ((0, pad_ff),)).astype(BF16)
    w_u = jnp.pad(w_ffn_up[..., D_FF:], ((0, 0),) * 3 + ((0, pad_ff),)).astype(BF16)
    w_d = jnp.pad(w_ffn_down, ((0, 0), (0, 0), (0, pad_ff), (0, 0))).astype(BF16)
    w_in_x = _extended_w_in(w_in)
    w_gate_b = w_gate.astype(BF16)
    w_branch_b = w_branch.astype(BF16)
    w_out_b = w_out.astype(BF16)
    sgu_w_b = sgu_w.astype(BF16)
    sgu_b_col = sgu_b[..., None]
    ln_g4 = ln_g[:, :, None, :]
    ln_b4 = ln_b[:, :, None, :]
    sgu_g3 = sgu_ln_g[:, None, :]
    sgu_b3 = sgu_ln_b[:, None, :]

    tables = _rope_tables(s)
    dft_c = _channel_dft()
    m1_lat, g_lat = _factored_dft(s)
    cn_ctx, nsn_ctx = _position_dft(n_ctx)

    rows = -(-(bsz + 1) // 8) * 8
    cc = jnp.zeros((rows, D_MODEL), F32).at[:bsz].set(c).at[bsz].set(c_ctx)
    mods = _mods_call(cc, w_mod, b_mod).reshape(depth, rows, N_MOD, D_MODEL)

    x_lat, x_ctx = x, ctx
    for i in range(depth):
        last = i == depth - 1
        m_lat = mods[i, :bsz]
        m_ctx = mods[i, bsz:bsz + 1]
        ffn = lambda xx, mm, j, half, tm: _ffn_call(
            xx, mm, w_g[i, half], w_u[i, half], w_d[i, half], ln_g4[i, j], ln_b4[i, j], j=j, tm=tm)
        inproj = lambda xx, mm, tb, tm: _inproj_call(
            xx, mm, w_in_x[i], tb, sgu_w_b[i], sgu_b_col[i], sgu_g3[i], sgu_b3[i], dft_c, tm=tm)
        merge = lambda xx, mm, a, sg, cf, tm: _merge_call(
            xx, mm, a, sg, cf, w_gate_b[i], w_branch_b[i], w_out_b[i], ln_g4[i, 1], ln_b4[i, 1], tm=tm)

        x_lat = ffn(x_lat, m_lat, 0, 0, 512)
        x_ctx = ffn(x_ctx, m_ctx, 0, 0, 256)

        q_c, k_c, v_c, s_c, ab_c = inproj(x_ctx, m_ctx, None, 256)
        q_l, k_l, v_l, s_l, ab_l = inproj(x_lat, m_lat, tables, 512)

        a_l = _attn_call(attn_sink[i], q_l, k_l, v_l, k_c, v_c, band=True)
        f_l = _fourier_fact_call(m1_lat, g_lat, ab_l)
        x_lat = merge(x_lat, m_lat, a_l, s_l, f_l, 512)
        x_lat = ffn(x_lat, m_lat, 2, 1, 512)
        if not last:
            a_c = _attn_call(attn_sink[i], q_c, None, None, k_c, v_c, band=False)
            f_c = _fourier_call(cn_ctx, nsn_ctx, ab_c, tk=256)
            x_ctx = merge(x_ctx, m_ctx, a_c, s_c, f_c, 256)
            x_ctx = ffn(x_ctx, m_ctx, 2, 1, 256)
    return x_lat
```

```python
import functools
import math

import jax
import jax.numpy as jnp
from jax import lax
from jax.experimental import pallas as pl
from jax.experimental.pallas import tpu as pltpu

F32 = jnp.float32
BF16 = jnp.bfloat16

D_MODEL = 1024
GRID_W = 64
HEAD_DIM = 64
N_Q_HEADS = 8
N_KV_HEADS = 2
GQA_GROUP = N_Q_HEADS // N_KV_HEADS
ATT_BLOCK = 128
ROPE_THETA = 10000.0
Q_W = N_Q_HEADS * HEAD_DIM
KV_W = N_KV_HEADS * HEAD_DIM
CHUNK = 128
SGU_GROUPS = 4
SGU_WIDTH = 512
FNET_GROUPS = 4
FNET_WIDTH = 512
FNET_GROUP_W = FNET_WIDTH // FNET_GROUPS
N_BRANCH = 3
D_FF = 2752
N_MOD = 9
MODEL_DEPTH = 4
ALPHA = (2 * MODEL_DEPTH) ** 0.25
LN_EPS = 1e-5
NEG_INF = -1e30

LANES = 128
MXU_W = 256
D_FF_PAD = -(-D_FF // MXU_W) * MXU_W
KV_DUP_W = 2 * KV_W
IN_Q, IN_K, IN_V, IN_U, IN_Z, IN_F, IN_END = 0, 512, 768, 1024, 1536, 2048, 2560
VMEM_LIMIT = 56 * 1024 * 1024
N_RES = 32
K2_BLK = 16
STAGE_PITCH = 40


def _params(n_axes):
    return pltpu.CompilerParams(dimension_semantics=("parallel",) * n_axes,
                                vmem_limit_bytes=VMEM_LIMIT)


def _const_spec(shape):
    nd = len(shape)
    return pl.BlockSpec(shape, lambda *_: (0,) * nd, pipeline_mode=pl.Buffered(1))


def _row_tile(s, pref):
    t = min(s, pref)
    while s % t:
        t -= LANES
    return t


def _dot(a, b):
    return jnp.dot(a, b, preferred_element_type=F32)


def _layer_norm(y, g, b):
    mu = jnp.mean(y, axis=-1, keepdims=True)
    d = y - mu
    var = jnp.mean(d * d, axis=-1, keepdims=True)
    return d * lax.rsqrt(var + LN_EPS) * g + b


def _mods_body(cc_ref, w_ref, b_ref, o_ref):
    s = cc_ref[...]
    s = (s * jax.nn.sigmoid(s)).astype(BF16)
    o_ref[...] = _dot(s, w_ref[...].astype(BF16)) + b_ref[...]


def _mods_call(cc, w_mod, b_mod):
    depth, _, n = w_mod.shape
    rows = cc.shape[0]
    tn = 1024
    return pl.pallas_call(
        _mods_body,
        grid=(depth, n // tn),
        in_specs=[
            pl.BlockSpec((rows, D_MODEL), lambda l, j: (0, 0)),
            pl.BlockSpec((None, D_MODEL, tn), lambda l, j: (l, 0, j)),
            pl.BlockSpec((None, 1, tn), lambda l, j: (l, 0, j)),
        ],
        out_specs=pl.BlockSpec((None, rows, tn), lambda l, j: (l, 0, j)),
        out_shape=jax.ShapeDtypeStruct((depth, rows, n), F32),
        compiler_params=_params(2),
        name="mods",
    )(cc, w_mod, b_mod.reshape(depth, 1, n))


def _ffn_body(x_ref, m_ref, wgu_ref, wd_ref, g_ref, b_ref, o_ref, *, j, sub):
    shift = m_ref[3 * j:3 * j + 1, :]
    scale = m_ref[3 * j + 1:3 * j + 2, :]
    gate = m_ref[3 * j + 2:3 * j + 3, :]
    for r in range(x_ref.shape[0] // sub):
        rows = slice(r * sub, (r + 1) * sub)
        x = x_ref[rows, :]
        xm = (x * (1 + scale) + shift).astype(BF16)
        acc = None
        for c in range(D_FF_PAD // MXU_W):
            sl = slice(c * MXU_W, (c + 1) * MXU_W)
            su = slice(D_FF_PAD + c * MXU_W, D_FF_PAD + (c + 1) * MXU_W)
            gg = _dot(xm, wgu_ref[:, sl])
            uu = _dot(xm, wgu_ref[:, su])
            h = (gg * jax.nn.sigmoid(gg) * uu).astype(BF16)
            d = _dot(h, wd_ref[sl, :])
            acc = d if acc is None else acc + d
        y = ALPHA * x + (0.5 * gate) * acc
        o_ref[rows, :] = _layer_norm(y, g_ref[...], b_ref[...])


def _ffn_call(x, mods, wgu, wd, g, b, *, j, tm, sub):
    bsz, s, _ = x.shape
    tm = _row_tile(s, tm)
    sub = min(sub, tm)
    mrows = mods.shape[0]
    return pl.pallas_call(
        functools.partial(_ffn_body, j=j, sub=sub),
        grid=(bsz, s // tm),
        in_specs=[
            pl.BlockSpec((None, tm, D_MODEL), lambda bi, t: (bi, t, 0)),
            pl.BlockSpec((None, N_MOD, D_MODEL), lambda bi, t: (bi % mrows, 0, 0)),
            _const_spec(wgu.shape), _const_spec(wd.shape),
            _const_spec(g.shape), _const_spec(b.shape),
        ],
        out_specs=pl.BlockSpec((None, tm, D_MODEL), lambda bi, t: (bi, t, 0)),
        out_shape=jax.ShapeDtypeStruct(x.shape, F32),
        compiler_params=_params(2),
        name="ffn",
    )(x, mods, wgu, wd, g, b)


def _gelu(x):
    return jax.nn.gelu(x)


def _inproj_body(*refs, rope, sub):
    if rope:
        (x_ref, m_ref, w_ref, cos_ref, sin_ref, sw_ref, sb_ref, sg_ref, sbn_ref, dft_ref,
         q_ref, k_ref, v_ref, s_ref, ab_ref, stage_ref) = refs
    else:
        (x_ref, m_ref, w_ref, sw_ref, sb_ref, sg_ref, sbn_ref, dft_ref,
         q_ref, k_ref, v_ref, s_ref, ab_ref) = refs
    if rope:
        lane = lax.broadcasted_iota(jnp.int32, (sub, LANES), 1)
        first = (lane % 32) < 16

    for r in range(x_ref.shape[0] // sub):
        rows = slice(r * sub, (r + 1) * sub)
        x = x_ref[rows, :]
        xm = (x * (1 + m_ref[4:5, :]) + m_ref[3:4, :]).astype(BF16)

        if rope:
            cos = cos_ref[rows, :]
            sin = sin_ref[rows, :]

            def rot(t, cos=cos, sin=sin):
                partner = jnp.where(first, pltpu.roll(t, LANES - 16, axis=1), pltpu.roll(t, 16, axis=1))
                return t * cos + partner * sin
        else:
            def rot(t):
                return t

        q = _dot(xm, w_ref[:, IN_Q:IN_K])
        for c in range(Q_W // LANES):
            sl = slice(c * LANES, (c + 1) * LANES)
            q_ref[rows, sl] = (rot(q[:, sl]) * (HEAD_DIM ** -0.5)).astype(BF16)
        k = _dot(xm, w_ref[:, IN_K:IN_V])
        for c in range(KV_DUP_W // LANES):
            sl = slice(c * LANES, (c + 1) * LANES)
            k_ref[rows, sl] = rot(k[:, sl]).astype(BF16)
        v_ref[rows, :] = _dot(xm, w_ref[:, IN_V:IN_U]).astype(BF16)

        u = _gelu(_dot(xm, w_ref[:, IN_U:IN_Z]))
        z = _gelu(_dot(xm, w_ref[:, IN_Z:IN_F]))
        zn = _layer_norm(z, sg_ref[...], sbn_ref[...]).astype(BF16)
        for ci in range(sub // CHUNK):
            rs = slice(ci * CHUNK, (ci + 1) * CHUNK)
            ro = slice(r * sub + ci * CHUNK, r * sub + (ci + 1) * CHUNK)
            for gi in range(SGU_GROUPS):
                cs = slice(gi * LANES, (gi + 1) * LANES)
                mixed = _dot(sw_ref[gi], zn[rs, cs]) + sb_ref[gi]
                s_ref[ro, cs] = (u[rs, cs] * mixed).astype(BF16)

        f = _dot(xm, w_ref[:, IN_F:IN_END]).astype(BF16)
        ab = _dot(f, dft_ref[...])
        if rope:
            n_cb = 2 * FNET_WIDTH // LANES
            n2 = sub // N_RES
            base = r * n2 * STAGE_PITCH
            for g in range(n2):
                for cb in range(n_cb):
                    stage_ref[cb, base + g * STAGE_PITCH:base + g * STAGE_PITCH + N_RES, :] = (
                        ab[g * N_RES:(g + 1) * N_RES, cb * LANES:(cb + 1) * LANES])
            for j in range(N_RES):
                cols = slice(j * 2 * FNET_WIDTH, (j + 1) * 2 * FNET_WIDTH)
                piece = [stage_ref[cb, pl.ds(base + j, n2, stride=STAGE_PITCH), :] for cb in range(n_cb)]
                ab_ref[r * n2:(r + 1) * n2, cols] = jnp.concatenate(piece, axis=1).astype(BF16)
        else:
            ab_ref[rows, :] = ab.astype(BF16)


def _inproj_call(x, mods, w_in, tables, sw, sb, sg, sbn, dft_c, *, tm, sub):
    bsz, s, _ = x.shape
    tm = _row_tile(s, tm)
    sub = min(sub, tm)
    mrows = mods.shape[0]
    rope = tables is not None
    row = lambda w: pl.BlockSpec((None, tm, w), lambda bi, t: (bi, t, 0))
    in_specs = [
        row(D_MODEL),
        pl.BlockSpec((None, N_MOD, D_MODEL), lambda bi, t: (bi % mrows, 0, 0)),
        _const_spec(w_in.shape),
    ]
    args = [x, mods, w_in]
    if rope:
        in_specs += [pl.BlockSpec((tm, LANES), lambda bi, t: (t, 0))] * 2
        args += list(tables)
    in_specs += [_const_spec(a.shape) for a in (sw, sb, sg, sbn, dft_c)]
    args += [sw, sb, sg, sbn, dft_c]
    widths = (Q_W, KV_DUP_W, KV_DUP_W, SGU_WIDTH, 2 * FNET_WIDTH)
    out_specs = [row(w) for w in widths]
    out_shape = [jax.ShapeDtypeStruct((bsz, s, w), BF16) for w in widths]
    scratch = []
    if rope:
        ab_w = N_RES * 2 * FNET_WIDTH
        out_specs[-1] = pl.BlockSpec((None, tm // N_RES, ab_w), lambda bi, t: (bi, t, 0))
        out_shape[-1] = jax.ShapeDtypeStruct((bsz, s // N_RES, ab_w), BF16)
        scratch = [pltpu.VMEM((2 * FNET_WIDTH // LANES, tm // N_RES * STAGE_PITCH, LANES), F32)]
    return pl.pallas_call(
        functools.partial(_inproj_body, rope=rope, sub=sub),
        grid=(bsz, s // tm),
        in_specs=in_specs,
        out_specs=out_specs,
        out_shape=out_shape,
        scratch_shapes=scratch,
        compiler_params=_params(2),
        name="inproj",
    )(*args)


def _attn_body(*refs, band, nq, nb):
    if band:
        (sink_ref, q_ref, kp_ref, km_ref, kn_ref, vp_ref, vm_ref, vn_ref, kx_ref, vx_ref,
         o_ref, kw_ref, vw_ref) = refs
    else:
        sink_ref, q_ref, kx_ref, vx_ref, o_ref = refs
    blk = ATT_BLOCK
    n_ctx = kx_ref.shape[0]
    t = pl.program_id(1)
    row = lax.broadcasted_iota(jnp.int32, (blk, blk), 0)
    col = lax.broadcasted_iota(jnp.int32, (blk, blk), 1)
    lo = col < HEAD_DIM
    zero = jnp.zeros((blk, blk), BF16)
    nt_dims = (((1,), (1,)), ((), ()))
    if band:
        kw_ref[0:blk] = kp_ref[...]
        kw_ref[blk:(nq + 1) * blk] = km_ref[...]
        kw_ref[(nq + 1) * blk:(nq + 2) * blk] = kn_ref[...]
        vw_ref[0:blk] = vp_ref[...]
        vw_ref[blk:(nq + 1) * blk] = vm_ref[...]
        vw_ref[(nq + 1) * blk:(nq + 2) * blk] = vn_ref[...]

    for qi in range(nq):
        rows = slice(qi * blk, (qi + 1) * blk)
        if band:
            gi = t * nq + qi
            prev_ok = jnp.logical_and(col >= row, gi > 0)
            next_ok = jnp.logical_and(col <= row, gi < nb - 1)
        for kh in range(N_KV_HEADS):
            cs = slice(kh * LANES, (kh + 1) * LANES)
            qs = []
            for pr in range(GQA_GROUP // 2):
                pair = q_ref[rows, (2 * kh + pr) * LANES:(2 * kh + pr + 1) * LANES]
                qs.append(jnp.where(lo, pair, zero))
                qs.append(jnp.where(lo, zero, pair))
            qstack = jnp.concatenate(qs, axis=0)
            sc_ctx = lax.dot_general(qstack, kx_ref[:, cs], nt_dims, preferred_element_type=F32)
            if band:
                win = slice(qi * blk, (qi + 3) * blk)
                sc_band = lax.dot_general(qstack, kw_ref[win, cs], nt_dims, preferred_element_type=F32)
            ps, dens = [], []
            for g in range(GQA_GROUP):
                hr = slice(g * blk, (g + 1) * blk)
                pieces = [sc_ctx[hr, c * blk:(c + 1) * blk] for c in range(n_ctx // blk)]
                if band:
                    pieces = [jnp.where(prev_ok, sc_band[hr, 0:blk], NEG_INF),
                              sc_band[hr, blk:2 * blk],
                              jnp.where(next_ok, sc_band[hr, 2 * blk:3 * blk], NEG_INF)] + pieces
                sink = sink_ref[kh * GQA_GROUP + g]
                mx = pieces[0]
                for pc in pieces[1:]:
                    mx = jnp.maximum(mx, pc)
                m = jnp.maximum(jnp.max(mx, axis=1, keepdims=True), sink)
                es = [jnp.exp(pc - m) for pc in pieces]
                tot = es[0]
                for e in es[1:]:
                    tot = tot + e
                dens.append(jnp.sum(tot, axis=1, keepdims=True) + jnp.exp(sink - m))
                ps.append(jnp.concatenate([e.astype(BF16) for e in es], axis=1))
            pall = jnp.concatenate(ps, axis=0)
            if band:
                o = _dot(pall[:, :3 * blk], vw_ref[win, cs]) + _dot(pall[:, 3 * blk:], vx_ref[:, cs])
            else:
                o = _dot(pall, vx_ref[:, cs])
            outs = [o[g * blk:(g + 1) * blk] / dens[g] for g in range(GQA_GROUP)]
            for pr in range(GQA_GROUP // 2):
                pair = jnp.where(lo, outs[2 * pr], outs[2 * pr + 1])
                o_ref[rows, (2 * kh + pr) * LANES:(2 * kh + pr + 1) * LANES] = pair.astype(BF16)


def _attn_call(sink, q, k, v, kx, vx, *, band, tq):
    bsz, s, _ = q.shape
    n_ctx = kx.shape[1]
    nb = s // ATT_BLOCK
    tq = _row_tile(s, tq)
    nq = tq // ATT_BLOCK
    qspec = pl.BlockSpec((None, tq, Q_W), lambda bi, i: (bi, i, 0))
    xspec = pl.BlockSpec((None, n_ctx, KV_DUP_W), lambda bi, i: (bi, 0, 0))
    sspec = pl.BlockSpec(memory_space=pltpu.SMEM)
    scratch = []
    if band:
        edge = lambda f: pl.BlockSpec((None, ATT_BLOCK, KV_DUP_W), lambda bi, i: (bi, f(i), 0))
        prev = edge(lambda i: jnp.maximum(i * nq - 1, 0))
        nxt = edge(lambda i: jnp.minimum((i + 1) * nq, nb - 1))
        main = pl.BlockSpec((None, tq, KV_DUP_W), lambda bi, i: (bi, i, 0))
        in_specs = [sspec, qspec] + [prev, main, nxt] * 2 + [xspec, xspec]
        args = (sink, q, k, k, k, v, v, v, kx, vx)
        scratch = [pltpu.VMEM(((nq + 2) * ATT_BLOCK, KV_DUP_W), BF16)] * 2
    else:
        in_specs = [sspec, qspec, xspec, xspec]
        args = (sink, q, kx, vx)
    return pl.pallas_call(
        functools.partial(_attn_body, band=band, nq=nq, nb=nb),
        grid=(bsz, s // tq),
        in_specs=in_specs,
        out_specs=qspec,
        out_shape=jax.ShapeDtypeStruct((bsz, s, Q_W), BF16),
        scratch_shapes=scratch,
        compiler_params=_params(2),
        name="attn_band" if band else "attn_ctx",
    )(*args)


def _fourier_body(cn_ref, nsn_ref, ab_ref, o_ref, *, scale):
    y = _dot(cn_ref[...], ab_ref[:, :FNET_WIDTH]) + _dot(nsn_ref[...], ab_ref[:, FNET_WIDTH:])
    o_ref[...] = (y * scale).astype(BF16)


def _fourier_call(cn, nsn, ab, *, tk):
    bsz, s, _ = ab.shape
    tk = _row_tile(s, tk)
    scale = 1.0 / math.sqrt(s * FNET_GROUP_W)
    return pl.pallas_call(
        functools.partial(_fourier_body, scale=scale),
        grid=(bsz, s // tk),
        in_specs=[
            pl.BlockSpec((tk, s), lambda bi, t: (t, 0)),
            pl.BlockSpec((tk, s), lambda bi, t: (t, 0)),
            pl.BlockSpec((None, s, 2 * FNET_WIDTH), lambda bi, t: (bi, 0, 0)),
        ],
        out_specs=pl.BlockSpec((None, tk, FNET_WIDTH), lambda bi, t: (bi, t, 0)),
        out_shape=jax.ShapeDtypeStruct((bsz, s, FNET_WIDTH), BF16),
        compiler_params=_params(2),
        name="fourier",
    )(cn, nsn, ab)


def _fourier_fact_body(ab_ref, m1_ref, g_ref, o_ref, t_ref, *, n_low, scale):
    w = FNET_WIDTH
    for j in range(N_RES):
        blk = ab_ref[:, j * 2 * w:(j + 1) * 2 * w]
        d = jnp.concatenate([blk[:, :w], blk[:, w:]], axis=0)
        r = _dot(m1_ref[j], d)
        rows = slice(j * n_low, (j + 1) * n_low)
        t_ref[rows, :w] = r[:n_low].astype(BF16)
        t_ref[rows, w:] = r[n_low:].astype(BF16)
    for kb in range(n_low // K2_BLK):
        x = jnp.concatenate(
            [t_ref[j * n_low + kb * K2_BLK:j * n_low + (kb + 1) * K2_BLK, :] for j in range(N_RES)], axis=0)
        y = _dot(g_ref[0], x[:, :w]) + _dot(g_ref[1], x[:, w:])
        for k1 in range(N_RES):
            o_ref[k1 * n_low + kb * K2_BLK:k1 * n_low + (kb + 1) * K2_BLK, :] = (
                y[k1 * K2_BLK:(k1 + 1) * K2_BLK] * scale).astype(BF16)


def _fourier_fact_call(m1, g, ab_view):
    bsz, n_low, _ = ab_view.shape
    s = n_low * N_RES
    scale = 1.0 / math.sqrt(s * FNET_GROUP_W)
    return pl.pallas_call(
        functools.partial(_fourier_fact_body, n_low=n_low, scale=scale),
        grid=(bsz,),
        in_specs=[
            pl.BlockSpec((None, n_low, N_RES * 2 * FNET_WIDTH), lambda bi: (bi, 0, 0)),
            _const_spec(m1.shape), _const_spec(g.shape),
        ],
        out_specs=pl.BlockSpec((None, s, FNET_WIDTH), lambda bi: (bi, 0, 0)),
        out_shape=jax.ShapeDtypeStruct((bsz, s, FNET_WIDTH), BF16),
        scratch_shapes=[pltpu.VMEM((s, 2 * FNET_WIDTH), BF16)],
        compiler_params=_params(1),
        name="fourier_fact",
    )(ab_view, m1, g)


def _merge_body(x_ref, m_ref, a_ref, s_ref, c_ref, wg_ref, wb_ref, wo_ref, g_ref, b_ref, o_ref, mg_ref, *, sub):
    for st in range(x_ref.shape[0] // sub):
        rows = slice(st * sub, (st + 1) * sub)
        x = x_ref[rows, :]
        h = (x * (1 + m_ref[4:5, :]) + m_ref[3:4, :]).astype(BF16)
        branches = (a_ref[rows, :], s_ref[rows, :], c_ref[rows, :])
        for n in range(D_MODEL // MXU_W):
            sl = slice(n * MXU_W, (n + 1) * MXU_W)
            acc = None
            for r in range(N_BRANCH):
                t = jax.nn.sigmoid(_dot(h, wg_ref[r, :, sl])) * _dot(branches[r], wb_ref[r, :, sl])
                acc = t if acc is None else acc + t
            mg_ref[rows, sl] = acc.astype(BF16)
        y = _dot(mg_ref[rows, :], wo_ref[...])
        o_ref[rows, :] = _layer_norm(ALPHA * x + m_ref[5:6, :] * y, g_ref[...], b_ref[...])


def _merge_call(x, mods, a, sgu, c, wg, wb, wo, g, b, *, tm, sub):
    bsz, s, _ = x.shape
    tm = _row_tile(s, tm)
    sub = min(sub, tm)
    mrows = mods.shape[0]
    row = lambda w: pl.BlockSpec((None, tm, w), lambda bi, t: (bi, t, 0))
    return pl.pallas_call(
        functools.partial(_merge_body, sub=sub),
        grid=(bsz, s // tm),
        in_specs=[
            row(D_MODEL),
            pl.BlockSpec((None, N_MOD, D_MODEL), lambda bi, t: (bi % mrows, 0, 0)),
            row(Q_W), row(SGU_WIDTH), row(FNET_WIDTH),
            _const_spec(wg.shape), _const_spec(wb.shape), _const_spec(wo.shape),
            _const_spec(g.shape), _const_spec(b.shape),
        ],
        out_specs=row(D_MODEL),
        out_shape=jax.ShapeDtypeStruct(x.shape, F32),
        scratch_shapes=[pltpu.VMEM((tm, D_MODEL), BF16)],
        compiler_params=_params(2),
        name="merge",
    )(x, mods, a, sgu, c, wg, wb, wo, g, b)


def _rope_tables(s):
    pos = jnp.arange(s)
    row = (pos // GRID_W).astype(F32)
    col = (pos % GRID_W).astype(F32)
    axis_dim = HEAD_DIM // 2
    inv_freq = ROPE_THETA ** (-jnp.arange(0, axis_dim, 2, dtype=F32) / axis_dim)
    ang_r = row[:, None] * inv_freq[None, :]
    ang_c = col[:, None] * inv_freq[None, :]
    cos = jnp.concatenate([jnp.cos(ang_r)] * 2 + [jnp.cos(ang_c)] * 2, axis=1)
    sin = jnp.concatenate([-jnp.sin(ang_r), jnp.sin(ang_r), -jnp.sin(ang_c), jnp.sin(ang_c)], axis=1)
    return jnp.tile(cos, (1, 2)), jnp.tile(sin, (1, 2))


def _dft_angles(n):
    k = jnp.arange(n, dtype=jnp.int32)
    r = (k[:, None] * k[None, :]) % n
    return r.astype(F32) * (2.0 * math.pi / n)


def _position_dft(n):
    ang = _dft_angles(n)
    return jnp.cos(ang).astype(BF16), (-jnp.sin(ang)).astype(BF16)


def _factored_dft(s):
    n_low = s // N_RES
    idx = jnp.arange(n_low, dtype=jnp.int32)
    n = jnp.arange(N_RES, dtype=jnp.int32)[:, None, None] + N_RES * idx[None, None, :]
    ang = ((idx[None, :, None] * n) % s).astype(F32) * (2.0 * math.pi / s)
    c, sn = jnp.cos(ang), jnp.sin(ang)
    m1 = jnp.concatenate([jnp.concatenate([c, -sn], axis=2), jnp.concatenate([sn, c], axis=2)], axis=1)
    ang_r = _dft_angles(N_RES)
    eye = jnp.eye(K2_BLK, dtype=F32)
    g = jnp.stack([jnp.kron(jnp.cos(ang_r), eye), -jnp.kron(jnp.sin(ang_r), eye)])
    return m1.astype(BF16), g.astype(BF16)


def _channel_dft():
    ang = _dft_angles(FNET_GROUP_W)
    eye = jnp.eye(FNET_GROUPS, dtype=F32)
    return jnp.concatenate([jnp.kron(eye, jnp.cos(ang)), jnp.kron(eye, jnp.sin(ang))], axis=1).astype(BF16)


def _extended_w_in(w_in):
    def dup(w):
        h0, h1 = w[..., :HEAD_DIM], w[..., HEAD_DIM:]
        return jnp.concatenate([h0, h0, h1, h1], axis=-1)
    q = w_in[..., :Q_W]
    k = w_in[..., Q_W:Q_W + KV_W]
    v = w_in[..., Q_W + KV_W:Q_W + 2 * KV_W]
    rest = w_in[..., Q_W + 2 * KV_W:]
    return jnp.concatenate([q, dup(k), dup(v), rest], axis=-1).astype(BF16)


def kernel(x, c, ctx, c_ctx, w_mod, b_mod, w_ffn_up, w_ffn_down, ln_g, ln_b, w_in, attn_sink,
           sgu_w, sgu_b, sgu_ln_g, sgu_ln_b, w_gate, w_branch, w_out):
    depth = w_mod.shape[0]
    bsz, s, _ = x.shape
    n_ctx = ctx.shape[1]

    pad_ff = D_FF_PAD - D_FF
    w_gu = jnp.pad(w_ffn_up.reshape(depth, 2, D_MODEL, 2, D_FF), ((0, 0),) * 4 + ((0, pad_ff),))
    w_gu = w_gu.astype(BF16).reshape(depth, 2, D_MODEL, 2 * D_FF_PAD)
    w_d = jnp.pad(w_ffn_down, ((0, 0), (0, 0), (0, pad_ff), (0, 0))).astype(BF16)
    w_in_x = _extended_w_in(w_in)
    w_gate_b = w_gate.astype(BF16)
    w_branch_b = w_branch.astype(BF16)
    w_out_b = w_out.astype(BF16)
    sgu_w_b = sgu_w.astype(BF16)
    sgu_b_col = sgu_b[..., None]
    ln_g4 = ln_g[:, :, None, :]
    ln_b4 = ln_b[:, :, None, :]
    sgu_g3 = sgu_ln_g[:, None, :]
    sgu_b3 = sgu_ln_b[:, None, :]

    tables = _rope_tables(s)
    dft_c = _channel_dft()
    m1_lat, g_lat = _factored_dft(s)
    cn_ctx, nsn_ctx = _position_dft(n_ctx)

    rows = -(-(bsz + 1) // 8) * 8
    cc = jnp.zeros((rows, D_MODEL), F32).at[:bsz].set(c).at[bsz].set(c_ctx)
    mods = _mods_call(cc, w_mod, b_mod).reshape(depth, rows, N_MOD, D_MODEL)

    x_lat, x_ctx = x, ctx
    for i in range(depth):
        last = i == depth - 1
        m_lat = mods[i, :bsz]
        m_ctx = mods[i, bsz:bsz + 1]
        ffn = lambda xx, mm, j, half, tm: _ffn_call(
            xx, mm, w_gu[i, half], w_d[i, half], ln_g4[i, j], ln_b4[i, j], j=j, tm=tm, sub=512)
        inproj = lambda xx, mm, tb, tm: _inproj_call(
            xx, mm, w_in_x[i], tb, sgu_w_b[i], sgu_b_col[i], sgu_g3[i], sgu_b3[i], dft_c, tm=tm, sub=512)
        merge = lambda xx, mm, a, sg, cf, tm: _merge_call(
            xx, mm, a, sg, cf, w_gate_b[i], w_branch_b[i], w_out_b[i], ln_g4[i, 1], ln_b4[i, 1], tm=tm, sub=512)

        x_lat = ffn(x_lat, m_lat, 0, 0, 1024)
        x_ctx = ffn(x_ctx, m_ctx, 0, 0, 256)

        q_c, k_c, v_c, s_c, ab_c = inproj(x_ctx, m_ctx, None, 256)
        q_l, k_l, v_l, s_l, ab_l = inproj(x_lat, m_lat, tables, 1024)

        a_l = _attn_call(attn_sink[i], q_l, k_l, v_l, k_c, v_c, band=True, tq=512)
        f_l = _fourier_fact_call(m1_lat, g_lat, ab_l)
        x_lat = merge(x_lat, m_lat, a_l, s_l, f_l, 1024)
        x_lat = ffn(x_lat, m_lat, 2, 1, 1024)
        if not last:
            a_c = _attn_call(attn_sink[i], q_c, None, None, k_c, v_c, band=False, tq=256)
            f_c = _fourier_call(cn_ctx, nsn_ctx, ab_c, tk=256)
            x_ctx = merge(x_ctx, m_ctx, a_c, s_c, f_c, 256)
            x_ctx = ffn(x_ctx, m_ctx, 2, 1, 256)
    return x_lat
```

```python
import functools
import math

import jax
import jax.numpy as jnp
from jax import lax
from jax.experimental import pallas as pl
from jax.experimental.pallas import tpu as pltpu

F32 = jnp.float32
BF16 = jnp.bfloat16

D_MODEL = 1024
GRID_W = 64
HEAD_DIM = 64
N_Q_HEADS = 8
N_KV_HEADS = 2
GQA_GROUP = N_Q_HEADS // N_KV_HEADS
ATT_BLOCK = 128
ROPE_THETA = 10000.0
Q_W = N_Q_HEADS * HEAD_DIM
KV_W = N_KV_HEADS * HEAD_DIM
CHUNK = 128
SGU_GROUPS = 4
SGU_WIDTH = 512
FNET_GROUPS = 4
FNET_WIDTH = 512
FNET_GROUP_W = FNET_WIDTH // FNET_GROUPS
N_BRANCH = 3
D_FF = 2752
N_MOD = 9
MODEL_DEPTH = 4
ALPHA = (2 * MODEL_DEPTH) ** 0.25
LN_EPS = 1e-5
NEG_INF = -1e30

LANES = 128
MXU_W = 256
D_FF_PAD = -(-D_FF // MXU_W) * MXU_W
KV_DUP_W = 2 * KV_W
IN_Q, IN_K, IN_U, IN_Z, IN_F, IN_END = 0, 512, 768, 1280, 1792, 2304
VMEM_LIMIT = 56 * 1024 * 1024
N_RES = 32
K2_BLK = 16
STAGE_PITCH = 40


def _params(n_axes):
    return pltpu.CompilerParams(dimension_semantics=("parallel",) * n_axes,
                                vmem_limit_bytes=VMEM_LIMIT)


def _const_spec(shape, lead=()):
    nl = len(lead)
    block = (None,) * nl + tuple(shape[nl:])
    index = tuple(lead) + (0,) * (len(shape) - nl)
    return pl.BlockSpec(block, lambda *_: index, pipeline_mode=pl.Buffered(1))


def _mods_spec(layer, ctx_row):
    if ctx_row is None:
        return pl.BlockSpec((None, None, N_MOD, D_MODEL), lambda bi, t: (layer, bi, 0, 0))
    return pl.BlockSpec((None, None, N_MOD, D_MODEL), lambda bi, t: (layer, ctx_row, 0, 0))


def _row_tile(s, pref):
    t = min(s, pref)
    while s % t:
        t -= LANES
    return t


def _dot(a, b):
    return jnp.dot(a, b, preferred_element_type=F32)


def _layer_norm(y, g, b):
    mu = jnp.mean(y, axis=-1, keepdims=True)
    d = y - mu
    var = jnp.mean(d * d, axis=-1, keepdims=True)
    return d * lax.rsqrt(var + LN_EPS) * g + b


def _mods_body(cc_ref, w_ref, b_ref, o_ref):
    s = cc_ref[...]
    s = (s * jax.nn.sigmoid(s)).astype(BF16)
    o_ref[...] = _dot(s, w_ref[...].astype(BF16)) + b_ref[...]


def _mods_call(cc, w_mod, b_mod):
    depth, _, n = w_mod.shape
    rows = cc.shape[0]
    tn = 1024
    return pl.pallas_call(
        _mods_body,
        grid=(depth, n // tn),
        in_specs=[
            pl.BlockSpec((rows, D_MODEL), lambda l, j: (0, 0)),
            pl.BlockSpec((None, D_MODEL, tn), lambda l, j: (l, 0, j)),
            pl.BlockSpec((None, 1, tn), lambda l, j: (l, 0, j)),
        ],
        out_specs=pl.BlockSpec((None, rows, tn), lambda l, j: (l, 0, j)),
        out_shape=jax.ShapeDtypeStruct((depth, rows, n), F32),
        compiler_params=_params(2),
        name="mods",
    )(cc, w_mod, b_mod.reshape(depth, 1, n))


def _ffn_body(x_ref, m_ref, wgu_ref, wd_ref, g_ref, b_ref, o_ref, *, j, sub):
    shift = m_ref[3 * j:3 * j + 1, :]
    scale = m_ref[3 * j + 1:3 * j + 2, :]
    gate = m_ref[3 * j + 2:3 * j + 3, :]
    for r in range(x_ref.shape[0] // sub):
        rows = slice(r * sub, (r + 1) * sub)
        x = x_ref[rows, :]
        xm = (x * (1 + scale) + shift).astype(BF16)
        acc = None
        for c in range(D_FF_PAD // MXU_W):
            sl = slice(c * MXU_W, (c + 1) * MXU_W)
            su = slice(D_FF_PAD + c * MXU_W, D_FF_PAD + (c + 1) * MXU_W)
            gg = _dot(xm, wgu_ref[:, sl])
            uu = _dot(xm, wgu_ref[:, su])
            h = (gg * jax.nn.sigmoid(gg) * uu).astype(BF16)
            d = _dot(h, wd_ref[sl, :])
            acc = d if acc is None else acc + d
        y = ALPHA * x + (0.5 * gate) * acc
        o_ref[rows, :] = _layer_norm(y, g_ref[...], b_ref[...])


def _ffn_call(x, mods, wgu, wd, g, b, *, layer, ctx_row, half, j, tm, sub):
    bsz, s, _ = x.shape
    tm = _row_tile(s, tm)
    sub = min(sub, tm)
    return pl.pallas_call(
        functools.partial(_ffn_body, j=j, sub=sub),
        grid=(bsz, s // tm),
        in_specs=[
            pl.BlockSpec((None, tm, D_MODEL), lambda bi, t: (bi, t, 0)),
            _mods_spec(layer, ctx_row),
            _const_spec(wgu.shape, (layer, half)), _const_spec(wd.shape, (layer, half)),
            _const_spec(g.shape, (layer, j)), _const_spec(b.shape, (layer, j)),
        ],
        out_specs=pl.BlockSpec((None, tm, D_MODEL), lambda bi, t: (bi, t, 0)),
        out_shape=jax.ShapeDtypeStruct(x.shape, F32),
        compiler_params=_params(2),
        name="ffn",
    )(x, mods, wgu, wd, g, b)


def _gelu(x):
    return jax.nn.gelu(x)


def _inproj_body(*refs, rope, sub):
    if rope:
        (x_ref, m_ref, w_ref, cos_ref, sin_ref, sw_ref, sb_ref, sg_ref, sbn_ref, dft_ref,
         q_ref, k_ref, v_ref, s_ref, ab_ref, stage_ref) = refs
    else:
        (x_ref, m_ref, w_ref, sw_ref, sb_ref, sg_ref, sbn_ref, dft_ref,
         q_ref, k_ref, v_ref, s_ref, ab_ref) = refs
    lane = lax.broadcasted_iota(jnp.int32, (sub, LANES), 1)
    low_half = lane < HEAD_DIM
    first = (lane % 32) < 16

    for r in range(x_ref.shape[0] // sub):
        rows = slice(r * sub, (r + 1) * sub)
        x = x_ref[rows, :]
        xm = (x * (1 + m_ref[4:5, :]) + m_ref[3:4, :]).astype(BF16)

        if rope:
            cos = cos_ref[rows, :]
            sin = sin_ref[rows, :]

            def rot(t, cos=cos, sin=sin):
                partner = jnp.where(first, pltpu.roll(t, LANES - 16, axis=1), pltpu.roll(t, 16, axis=1))
                return t * cos + partner * sin
        else:
            def rot(t):
                return t

        q = _dot(xm, w_ref[:, IN_Q:IN_K])
        for c in range(Q_W // LANES):
            sl = slice(c * LANES, (c + 1) * LANES)
            q_ref[rows, sl] = (rot(q[:, sl]) * (HEAD_DIM ** -0.5)).astype(BF16)
        kv = _dot(xm, w_ref[:, IN_K:IN_U])
        for t, o_ref in ((rot(kv[:, :KV_W]), k_ref), (kv[:, KV_W:], v_ref)):
            swapped = pltpu.roll(t, HEAD_DIM, axis=1)
            o_ref[rows, :LANES] = jnp.where(low_half, t, swapped).astype(BF16)
            o_ref[rows, LANES:] = jnp.where(low_half, swapped, t).astype(BF16)

        u = _gelu(_dot(xm, w_ref[:, IN_U:IN_Z]))
        z = _gelu(_dot(xm, w_ref[:, IN_Z:IN_F]))
        zn = _layer_norm(z, sg_ref[...], sbn_ref[...]).astype(BF16)
        for ci in range(sub // CHUNK):
            rs = slice(ci * CHUNK, (ci + 1) * CHUNK)
            ro = slice(r * sub + ci * CHUNK, r * sub + (ci + 1) * CHUNK)
            for gi in range(SGU_GROUPS):
                cs = slice(gi * LANES, (gi + 1) * LANES)
                mixed = _dot(sw_ref[gi], zn[rs, cs]) + sb_ref[gi]
                s_ref[ro, cs] = (u[rs, cs] * mixed).astype(BF16)

        f = _dot(xm, w_ref[:, IN_F:IN_END]).astype(BF16)
        ab = _dot(f, dft_ref[...])
        if rope:
            n_cb = 2 * FNET_WIDTH // LANES
            n2 = sub // N_RES
            base = r * n2 * STAGE_PITCH
            for g in range(n2):
                for cb in range(n_cb):
                    stage_ref[cb, base + g * STAGE_PITCH:base + g * STAGE_PITCH + N_RES, :] = (
                        ab[g * N_RES:(g + 1) * N_RES, cb * LANES:(cb + 1) * LANES])
            for j in range(N_RES):
                cols = slice(j * 2 * FNET_WIDTH, (j + 1) * 2 * FNET_WIDTH)
                piece = [stage_ref[cb, pl.ds(base + j, n2, stride=STAGE_PITCH), :] for cb in range(n_cb)]
                ab_ref[r * n2:(r + 1) * n2, cols] = jnp.concatenate(piece, axis=1).astype(BF16)
        else:
            ab_ref[rows, :] = ab.astype(BF16)


def _inproj_call(x, mods, w_in, tables, sw, sb, sg, sbn, dft_c, *, layer, ctx_row, tm, sub):
    bsz, s, _ = x.shape
    tm = _row_tile(s, tm)
    sub = min(sub, tm)
    rope = tables is not None
    row = lambda w: pl.BlockSpec((None, tm, w), lambda bi, t: (bi, t, 0))
    in_specs = [row(D_MODEL), _mods_spec(layer, ctx_row), _const_spec(w_in.shape, (layer,))]
    args = [x, mods, w_in]
    if rope:
        in_specs += [pl.BlockSpec((tm, LANES), lambda bi, t: (t, 0))] * 2
        args += list(tables)
    in_specs += [_const_spec(a.shape, (layer,)) for a in (sw, sb, sg, sbn)] + [_const_spec(dft_c.shape)]
    args += [sw, sb, sg, sbn, dft_c]
    widths = (Q_W, KV_DUP_W, KV_DUP_W, SGU_WIDTH, 2 * FNET_WIDTH)
    out_specs = [row(w) for w in widths]
    out_shape = [jax.ShapeDtypeStruct((bsz, s, w), BF16) for w in widths]
    scratch = []
    if rope:
        ab_w = N_RES * 2 * FNET_WIDTH
        out_specs[-1] = pl.BlockSpec((None, tm // N_RES, ab_w), lambda bi, t: (bi, t, 0))
        out_shape[-1] = jax.ShapeDtypeStruct((bsz, s // N_RES, ab_w), BF16)
        scratch = [pltpu.VMEM((2 * FNET_WIDTH // LANES, tm // N_RES * STAGE_PITCH, LANES), F32)]
    return pl.pallas_call(
        functools.partial(_inproj_body, rope=rope, sub=sub),
        grid=(bsz, s // tm),
        in_specs=in_specs,
        out_specs=out_specs,
        out_shape=out_shape,
        scratch_shapes=scratch,
        compiler_params=_params(2),
        name="inproj",
    )(*args)


def _attn_body(*refs, band, nq, nb, layer):
    if band:
        (sink_ref, q_ref, kp_ref, km_ref, kn_ref, vp_ref, vm_ref, vn_ref, kx_ref, vx_ref,
         o_ref, kw_ref, vw_ref) = refs
    else:
        sink_ref, q_ref, kx_ref, vx_ref, o_ref = refs
    blk = ATT_BLOCK
    n_ctx = kx_ref.shape[0]
    t = pl.program_id(1)
    row = lax.broadcasted_iota(jnp.int32, (blk, blk), 0)
    col = lax.broadcasted_iota(jnp.int32, (blk, blk), 1)
    lo = col < HEAD_DIM
    zero = jnp.zeros((blk, blk), BF16)
    nt_dims = (((1,), (1,)), ((), ()))
    if band:
        kw_ref[0:blk] = kp_ref[...]
        kw_ref[blk:(nq + 1) * blk] = km_ref[...]
        kw_ref[(nq + 1) * blk:(nq + 2) * blk] = kn_ref[...]
        vw_ref[0:blk] = vp_ref[...]
        vw_ref[blk:(nq + 1) * blk] = vm_ref[...]
        vw_ref[(nq + 1) * blk:(nq + 2) * blk] = vn_ref[...]

    for qi in range(nq):
        rows = slice(qi * blk, (qi + 1) * blk)
        if band:
            gi = t * nq + qi
            prev_ok = jnp.logical_and(col >= row, gi > 0)
            next_ok = jnp.logical_and(col <= row, gi < nb - 1)
        for kh in range(N_KV_HEADS):
            cs = slice(kh * LANES, (kh + 1) * LANES)
            qs = []
            for pr in range(GQA_GROUP // 2):
                pair = q_ref[rows, (2 * kh + pr) * LANES:(2 * kh + pr + 1) * LANES]
                qs.append(jnp.where(lo, pair, zero))
                qs.append(jnp.where(lo, zero, pair))
            qstack = jnp.concatenate(qs, axis=0)
            sc_ctx = lax.dot_general(qstack, kx_ref[:, cs], nt_dims, preferred_element_type=F32)
            if band:
                win = slice(qi * blk, (qi + 3) * blk)
                sc_band = lax.dot_general(qstack, kw_ref[win, cs], nt_dims, preferred_element_type=F32)
            ps, dens = [], []
            for g in range(GQA_GROUP):
                hr = slice(g * blk, (g + 1) * blk)
                pieces = [sc_ctx[hr, c * blk:(c + 1) * blk] for c in range(n_ctx // blk)]
                if band:
                    pieces = [jnp.where(prev_ok, sc_band[hr, 0:blk], NEG_INF),
                              sc_band[hr, blk:2 * blk],
                              jnp.where(next_ok, sc_band[hr, 2 * blk:3 * blk], NEG_INF)] + pieces
                sink = sink_ref[layer, kh * GQA_GROUP + g]
                mx = pieces[0]
                for pc in pieces[1:]:
                    mx = jnp.maximum(mx, pc)
                m = jnp.maximum(jnp.max(mx, axis=1, keepdims=True), sink)
                es = [jnp.exp(pc - m) for pc in pieces]
                tot = es[0]
                for e in es[1:]:
                    tot = tot + e
                dens.append(jnp.sum(tot, axis=1, keepdims=True) + jnp.exp(sink - m))
                ps.append(jnp.concatenate([e.astype(BF16) for e in es], axis=1))
            pall = jnp.concatenate(ps, axis=0)
            if band:
                o = _dot(pall[:, :3 * blk], vw_ref[win, cs]) + _dot(pall[:, 3 * blk:], vx_ref[:, cs])
            else:
                o = _dot(pall, vx_ref[:, cs])
            outs = [o[g * blk:(g + 1) * blk] / dens[g] for g in range(GQA_GROUP)]
            for pr in range(GQA_GROUP // 2):
                pair = jnp.where(lo, outs[2 * pr], outs[2 * pr + 1])
                o_ref[rows, (2 * kh + pr) * LANES:(2 * kh + pr + 1) * LANES] = pair.astype(BF16)


def _attn_call(sink, q, k, v, kx, vx, *, layer, band, tq):
    bsz, s, _ = q.shape
    n_ctx = kx.shape[1]
    nb = s // ATT_BLOCK
    tq = _row_tile(s, tq)
    nq = tq // ATT_BLOCK
    qspec = pl.BlockSpec((None, tq, Q_W), lambda bi, i: (bi, i, 0))
    xspec = pl.BlockSpec((None, n_ctx, KV_DUP_W), lambda bi, i: (bi, 0, 0))
    sspec = pl.BlockSpec(memory_space=pltpu.SMEM)
    scratch = []
    if band:
        edge = lambda f: pl.BlockSpec((None, ATT_BLOCK, KV_DUP_W), lambda bi, i: (bi, f(i), 0))
        prev = edge(lambda i: jnp.maximum(i * nq - 1, 0))
        nxt = edge(lambda i: jnp.minimum((i + 1) * nq, nb - 1))
        main = pl.BlockSpec((None, tq, KV_DUP_W), lambda bi, i: (bi, i, 0))
        in_specs = [sspec, qspec] + [prev, main, nxt] * 2 + [xspec, xspec]
        args = (sink, q, k, k, k, v, v, v, kx, vx)
        scratch = [pltpu.VMEM(((nq + 2) * ATT_BLOCK, KV_DUP_W), BF16)] * 2
    else:
        in_specs = [sspec, qspec, xspec, xspec]
        args = (sink, q, kx, vx)
    return pl.pallas_call(
        functools.partial(_attn_body, band=band, nq=nq, nb=nb, layer=layer),
        grid=(bsz, s // tq),
        in_specs=in_specs,
        out_specs=qspec,
        out_shape=jax.ShapeDtypeStruct((bsz, s, Q_W), BF16),
        scratch_shapes=scratch,
        compiler_params=_params(2),
        name="attn_band" if band else "attn_ctx",
    )(*args)


def _fourier_body(cn_ref, nsn_ref, ab_ref, o_ref, *, scale):
    y = _dot(cn_ref[...], ab_ref[:, :FNET_WIDTH]) + _dot(nsn_ref[...], ab_ref[:, FNET_WIDTH:])
    o_ref[...] = (y * scale).astype(BF16)


def _fourier_call(cn, nsn, ab, *, tk):
    bsz, s, _ = ab.shape
    tk = _row_tile(s, tk)
    scale = 1.0 / math.sqrt(s * FNET_GROUP_W)
    return pl.pallas_call(
        functools.partial(_fourier_body, scale=scale),
        grid=(bsz, s // tk),
        in_specs=[
            pl.BlockSpec((tk, s), lambda bi, t: (t, 0)),
            pl.BlockSpec((tk, s), lambda bi, t: (t, 0)),
            pl.BlockSpec((None, s, 2 * FNET_WIDTH), lambda bi, t: (bi, 0, 0)),
        ],
        out_specs=pl.BlockSpec((None, tk, FNET_WIDTH), lambda bi, t: (bi, t, 0)),
        out_shape=jax.ShapeDtypeStruct((bsz, s, FNET_WIDTH), BF16),
        compiler_params=_params(2),
        name="fourier",
    )(cn, nsn, ab)


def _fourier_fact_body(ab_ref, m1_ref, g_ref, o_ref, t_ref, *, n_low, scale):
    w = FNET_WIDTH
    for j in range(N_RES):
        blk = ab_ref[:, j * 2 * w:(j + 1) * 2 * w]
        d = jnp.concatenate([blk[:, :w], blk[:, w:]], axis=0)
        r = _dot(m1_ref[j], d)
        rows = slice(j * n_low, (j + 1) * n_low)
        t_ref[rows, :w] = r[:n_low].astype(BF16)
        t_ref[rows, w:] = r[n_low:].astype(BF16)
    for kb in range(n_low // K2_BLK):
        x = jnp.concatenate(
            [t_ref[j * n_low + kb * K2_BLK:j * n_low + (kb + 1) * K2_BLK, :] for j in range(N_RES)], axis=0)
        y = _dot(g_ref[0], x[:, :w]) + _dot(g_ref[1], x[:, w:])
        for k1 in range(N_RES):
            o_ref[k1 * n_low + kb * K2_BLK:k1 * n_low + (kb + 1) * K2_BLK, :] = (
                y[k1 * K2_BLK:(k1 + 1) * K2_BLK] * scale).astype(BF16)


def _fourier_fact_call(m1, g, ab_view):
    bsz, n_low, _ = ab_view.shape
    s = n_low * N_RES
    scale = 1.0 / math.sqrt(s * FNET_GROUP_W)
    return pl.pallas_call(
        functools.partial(_fourier_fact_body, n_low=n_low, scale=scale),
        grid=(bsz,),
        in_specs=[
            pl.BlockSpec((None, n_low, N_RES * 2 * FNET_WIDTH), lambda bi: (bi, 0, 0)),
            _const_spec(m1.shape), _const_spec(g.shape),
        ],
        out_specs=pl.BlockSpec((None, s, FNET_WIDTH), lambda bi: (bi, 0, 0)),
        out_shape=jax.ShapeDtypeStruct((bsz, s, FNET_WIDTH), BF16),
        scratch_shapes=[pltpu.VMEM((s, 2 * FNET_WIDTH), BF16)],
        compiler_params=_params(1),
        name="fourier_fact",
    )(ab_view, m1, g)


def _merge_body(x_ref, m_ref, a_ref, s_ref, c_ref, wg_ref, wb_ref, wo_ref, g_ref, b_ref, o_ref, mg_ref, *, sub):
    for st in range(x_ref.shape[0] // sub):
        rows = slice(st * sub, (st + 1) * sub)
        x = x_ref[rows, :]
        h = (x * (1 + m_ref[4:5, :]) + m_ref[3:4, :]).astype(BF16)
        branches = (a_ref[rows, :], s_ref[rows, :], c_ref[rows, :])
        for n in range(D_MODEL // MXU_W):
            sl = slice(n * MXU_W, (n + 1) * MXU_W)
            acc = None
            for r in range(N_BRANCH):
                t = jax.nn.sigmoid(_dot(h, wg_ref[r, :, sl])) * _dot(branches[r], wb_ref[r, :, sl])
                acc = t if acc is None else acc + t
            mg_ref[rows, sl] = acc.astype(BF16)
        y = _dot(mg_ref[rows, :], wo_ref[...])
        o_ref[rows, :] = _layer_norm(ALPHA * x + m_ref[5:6, :] * y, g_ref[...], b_ref[...])


def _merge_call(x, mods, a, sgu, c, wg, wb, wo, g, b, *, layer, ctx_row, tm, sub):
    bsz, s, _ = x.shape
    tm = _row_tile(s, tm)
    sub = min(sub, tm)
    row = lambda w: pl.BlockSpec((None, tm, w), lambda bi, t: (bi, t, 0))
    return pl.pallas_call(
        functools.partial(_merge_body, sub=sub),
        grid=(bsz, s // tm),
        in_specs=[
            row(D_MODEL), _mods_spec(layer, ctx_row),
            row(Q_W), row(SGU_WIDTH), row(FNET_WIDTH),
            _const_spec(wg.shape, (layer,)), _const_spec(wb.shape, (layer,)), _const_spec(wo.shape, (layer,)),
            _const_spec(g.shape, (layer, 1)), _const_spec(b.shape, (layer, 1)),
        ],
        out_specs=row(D_MODEL),
        out_shape=jax.ShapeDtypeStruct(x.shape, F32),
        scratch_shapes=[pltpu.VMEM((tm, D_MODEL), BF16)],
        compiler_params=_params(2),
        name="merge",
    )(x, mods, a, sgu, c, wg, wb, wo, g, b)


def _rope_tables(s):
    pos = jnp.arange(s)
    row = (pos // GRID_W).astype(F32)
    col = (pos % GRID_W).astype(F32)
    axis_dim = HEAD_DIM // 2
    inv_freq = ROPE_THETA ** (-jnp.arange(0, axis_dim, 2, dtype=F32) / axis_dim)
    ang_r = row[:, None] * inv_freq[None, :]
    ang_c = col[:, None] * inv_freq[None, :]
    cos = jnp.concatenate([jnp.cos(ang_r)] * 2 + [jnp.cos(ang_c)] * 2, axis=1)
    sin = jnp.concatenate([-jnp.sin(ang_r), jnp.sin(ang_r), -jnp.sin(ang_c), jnp.sin(ang_c)], axis=1)
    return jnp.tile(cos, (1, 2)), jnp.tile(sin, (1, 2))


def _dft_angles(n):
    k = jnp.arange(n, dtype=jnp.int32)
    r = (k[:, None] * k[None, :]) % n
    return r.astype(F32) * (2.0 * math.pi / n)


def _position_dft(n):
    ang = _dft_angles(n)
    return jnp.cos(ang).astype(BF16), (-jnp.sin(ang)).astype(BF16)


def _factored_dft(s):
    n_low = s // N_RES
    idx = jnp.arange(n_low, dtype=jnp.int32)
    n = jnp.arange(N_RES, dtype=jnp.int32)[:, None, None] + N_RES * idx[None, None, :]
    ang = ((idx[None, :, None] * n) % s).astype(F32) * (2.0 * math.pi / s)
    c, sn = jnp.cos(ang), jnp.sin(ang)
    m1 = jnp.concatenate([jnp.concatenate([c, -sn], axis=2), jnp.concatenate([sn, c], axis=2)], axis=1)
    ang_r = _dft_angles(N_RES)
    eye = jnp.eye(K2_BLK, dtype=F32)
    g = jnp.stack([jnp.kron(jnp.cos(ang_r), eye), -jnp.kron(jnp.sin(ang_r), eye)])
    return m1.astype(BF16), g.astype(BF16)


def _channel_dft():
    ang = _dft_angles(FNET_GROUP_W)
    eye = jnp.eye(FNET_GROUPS, dtype=F32)
    return jnp.concatenate([jnp.kron(eye, jnp.cos(ang)), jnp.kron(eye, jnp.sin(ang))], axis=1).astype(BF16)


def kernel(x, c, ctx, c_ctx, w_mod, b_mod, w_ffn_up, w_ffn_down, ln_g, ln_b, w_in, attn_sink,
           sgu_w, sgu_b, sgu_ln_g, sgu_ln_b, w_gate, w_branch, w_out):
    depth = w_mod.shape[0]
    bsz, s, _ = x.shape
    n_ctx = ctx.shape[1]

    pad_ff = D_FF_PAD - D_FF
    zc = jnp.zeros(w_ffn_up.shape[:-1] + (pad_ff,), BF16)
    w_up_b = w_ffn_up.astype(BF16)
    w_gu = jnp.concatenate([w_up_b[..., :D_FF], zc, w_up_b[..., D_FF:], zc], axis=-1)
    zr = jnp.zeros(w_ffn_down.shape[:2] + (pad_ff, D_MODEL), BF16)
    w_d = jnp.concatenate([w_ffn_down.astype(BF16), zr], axis=2)
    w_in_b = w_in.astype(BF16)
    w_gate_b = w_gate.astype(BF16)
    w_branch_b = w_branch.astype(BF16)
    w_out_b = w_out.astype(BF16)
    sgu_w_b = sgu_w.astype(BF16)
    sgu_b_col = sgu_b[..., None]
    ln_g4 = ln_g[:, :, None, :]
    ln_b4 = ln_b[:, :, None, :]
    sgu_g3 = sgu_ln_g[:, None, :]
    sgu_b3 = sgu_ln_b[:, None, :]

    tables = _rope_tables(s)
    dft_c = _channel_dft()
    m1_lat, g_lat = _factored_dft(s)
    cn_ctx, nsn_ctx = _position_dft(n_ctx)

    rows = -(-(bsz + 1) // 8) * 8
    cc = jnp.zeros((rows, D_MODEL), F32).at[:bsz].set(c).at[bsz].set(c_ctx)
    mods = _mods_call(cc, w_mod, b_mod).reshape(depth, rows, N_MOD, D_MODEL)

    def ffn(xx, i, ctx_row, half, j, tm):
        return _ffn_call(xx, mods, w_gu, w_d, ln_g4, ln_b4, layer=i, ctx_row=ctx_row, half=half, j=j,
                         tm=tm, sub=512)

    def inproj(xx, i, ctx_row, tb, tm):
        return _inproj_call(xx, mods, w_in_b, tb, sgu_w_b, sgu_b_col, sgu_g3, sgu_b3, dft_c,
                            layer=i, ctx_row=ctx_row, tm=tm, sub=512)

    def merge(xx, i, ctx_row, a, sg, cf, tm):
        return _merge_call(xx, mods, a, sg, cf, w_gate_b, w_branch_b, w_out_b, ln_g4, ln_b4,
                           layer=i, ctx_row=ctx_row, tm=tm, sub=512)

    x_lat, x_ctx = x, ctx
    for i in range(depth):
        last = i == depth - 1
        x_lat = ffn(x_lat, i, None, 0, 0, 1024)
        x_ctx = ffn(x_ctx, i, bsz, 0, 0, 256)

        q_c, k_c, v_c, s_c, ab_c = inproj(x_ctx, i, bsz, None, 256)
        q_l, k_l, v_l, s_l, ab_l = inproj(x_lat, i, None, tables, 1024)

        a_l = _attn_call(attn_sink, q_l, k_l, v_l, k_c, v_c, layer=i, band=True, tq=512)
        f_l = _fourier_fact_call(m1_lat, g_lat, ab_l)
        x_lat = merge(x_lat, i, None, a_l, s_l, f_l, 1024)
        x_lat = ffn(x_lat, i, None, 1, 2, 1024)
        if not last:
            a_c = _attn_call(attn_sink, q_c, None, None, k_c, v_c, layer=i, band=False, tq=256)
            f_c = _fourier_call(cn_ctx, nsn_ctx, ab_c, tk=256)
            x_ctx = merge(x_ctx, i, bsz, a_c, s_c, f_c, 256)
            x_ctx = ffn(x_ctx, i, bsz, 1, 2, 256)
    return x_lat
```

```python
import functools
import math

import jax
import jax.numpy as jnp
from jax import lax
from jax.experimental import pallas as pl
from jax.experimental.pallas import tpu as pltpu

F32 = jnp.float32
BF16 = jnp.bfloat16

D_MODEL = 1024
GRID_W = 64
HEAD_DIM = 64
N_Q_HEADS = 8
N_KV_HEADS = 2
GQA_GROUP = N_Q_HEADS // N_KV_HEADS
ATT_BLOCK = 128
ROPE_THETA = 10000.0
Q_W = N_Q_HEADS * HEAD_DIM
KV_W = N_KV_HEADS * HEAD_DIM
CHUNK = 128
SGU_GROUPS = 4
SGU_WIDTH = 512
FNET_GROUPS = 4
FNET_WIDTH = 512
FNET_GROUP_W = FNET_WIDTH // FNET_GROUPS
N_BRANCH = 3
D_FF = 2752
N_MOD = 9
MODEL_DEPTH = 4
ALPHA = (2 * MODEL_DEPTH) ** 0.25
LN_EPS = 1e-5
NEG_INF = -1e30

LANES = 128
MXU_W = 256
D_FF_PAD = -(-D_FF // MXU_W) * MXU_W
UP_OFF = D_FF % LANES
UP0 = D_FF - UP_OFF
assert UP0 + D_FF_PAD == 2 * D_FF
KV_DUP_W = 2 * KV_W
IN_Q, IN_K, IN_U, IN_Z, IN_F, IN_END = 0, 512, 768, 1280, 1792, 2304
VMEM_LIMIT = 56 * 1024 * 1024
N_RES = 32
K2_BLK = 16
STAGE_PITCH = 40


def _params(n_axes):
    return pltpu.CompilerParams(dimension_semantics=("parallel",) * n_axes,
                                vmem_limit_bytes=VMEM_LIMIT)


def _const_spec(shape, lead=()):
    nl = len(lead)
    block = (None,) * nl + tuple(shape[nl:])
    index = tuple(lead) + (0,) * (len(shape) - nl)
    return pl.BlockSpec(block, lambda *_: index, pipeline_mode=pl.Buffered(1))


def _mods_spec(layer, ctx_row):
    if ctx_row is None:
        return pl.BlockSpec((None, None, N_MOD, D_MODEL), lambda bi, t: (layer, bi, 0, 0))
    return pl.BlockSpec((None, None, N_MOD, D_MODEL), lambda bi, t: (layer, ctx_row, 0, 0))


def _row_tile(s, pref):
    t = min(s, pref)
    while s % t:
        t -= LANES
    return t


def _dot(a, b):
    return jnp.dot(a, b, preferred_element_type=F32)


def _layer_norm(y, g, b):
    mu = jnp.mean(y, axis=-1, keepdims=True)
    d = y - mu
    var = jnp.mean(d * d, axis=-1, keepdims=True)
    return d * lax.rsqrt(var + LN_EPS) * g + b


def _mods_body(cc_ref, w_ref, b_ref, o_ref):
    s = cc_ref[...]
    s = (s * jax.nn.sigmoid(s)).astype(BF16)
    o_ref[...] = _dot(s, w_ref[...].astype(BF16)) + b_ref[...]


def _mods_call(cc, w_mod, b_mod):
    depth, _, n = w_mod.shape
    rows = cc.shape[0]
    tn = 1024
    return pl.pallas_call(
        _mods_body,
        grid=(depth, n // tn),
        in_specs=[
            pl.BlockSpec((rows, D_MODEL), lambda l, j: (0, 0)),
            pl.BlockSpec((None, D_MODEL, tn), lambda l, j: (l, 0, j)),
            pl.BlockSpec((None, 1, tn), lambda l, j: (l, 0, j)),
        ],
        out_specs=pl.BlockSpec((None, rows, tn), lambda l, j: (l, 0, j)),
        out_shape=jax.ShapeDtypeStruct((depth, rows, n), F32),
        compiler_params=_params(2),
        name="mods",
    )(cc, w_mod, b_mod.reshape(depth, 1, n))


def _pack_up_body(w_ref, o_ref):
    rows = w_ref.shape[0]
    lane = lax.broadcasted_iota(jnp.int32, (rows, D_FF_PAD), 1)
    real = lane < D_FF
    o_ref[:, :D_FF_PAD] = jnp.where(real, w_ref[:, :D_FF_PAD], 0.0).astype(BF16)
    up = pltpu.roll(w_ref[:, UP0:], D_FF_PAD - UP_OFF, axis=1)
    o_ref[:, D_FF_PAD:] = jnp.where(real, up, 0.0).astype(BF16)


def _pack_up_call(w_up):
    depth, two, d, n = w_up.shape
    tr = 256
    return pl.pallas_call(
        _pack_up_body,
        grid=(depth, two, d // tr),
        in_specs=[pl.BlockSpec((None, None, tr, n), lambda l, h, r: (l, h, r, 0))],
        out_specs=pl.BlockSpec((None, None, tr, 2 * D_FF_PAD), lambda l, h, r: (l, h, r, 0)),
        out_shape=jax.ShapeDtypeStruct((depth, two, d, 2 * D_FF_PAD), BF16),
        compiler_params=_params(3),
        name="pack_up",
    )(w_up)


def _pack_down_body(w_ref, o_ref):
    o_ref[:D_FF, :] = w_ref[...].astype(BF16)
    o_ref[D_FF:, :] = jnp.zeros((D_FF_PAD - D_FF, w_ref.shape[1]), BF16)


def _pack_down_call(w_down):
    depth, two, n, d = w_down.shape
    return pl.pallas_call(
        _pack_down_body,
        grid=(depth, two),
        in_specs=[pl.BlockSpec((None, None, n, d), lambda l, h: (l, h, 0, 0))],
        out_specs=pl.BlockSpec((None, None, D_FF_PAD, d), lambda l, h: (l, h, 0, 0)),
        out_shape=jax.ShapeDtypeStruct((depth, two, D_FF_PAD, d), BF16),
        compiler_params=_params(2),
        name="pack_down",
    )(w_down)


def _ffn_body(x_ref, m_ref, wgu_ref, wd_ref, g_ref, b_ref, o_ref, *, j, sub):
    shift = m_ref[3 * j:3 * j + 1, :]
    scale = m_ref[3 * j + 1:3 * j + 2, :]
    gate = m_ref[3 * j + 2:3 * j + 3, :]
    for r in range(x_ref.shape[0] // sub):
        rows = slice(r * sub, (r + 1) * sub)
        x = x_ref[rows, :]
        xm = (x * (1 + scale) + shift).astype(BF16)
        acc = None
        for c in range(D_FF_PAD // MXU_W):
            sl = slice(c * MXU_W, (c + 1) * MXU_W)
            su = slice(D_FF_PAD + c * MXU_W, D_FF_PAD + (c + 1) * MXU_W)
            gg = _dot(xm, wgu_ref[:, sl])
            uu = _dot(xm, wgu_ref[:, su])
            h = (gg * jax.nn.sigmoid(gg) * uu).astype(BF16)
            d = _dot(h, wd_ref[sl, :])
            acc = d if acc is None else acc + d
        y = ALPHA * x + (0.5 * gate) * acc
        o_ref[rows, :] = _layer_norm(y, g_ref[...], b_ref[...])


def _ffn_call(x, mods, wgu, wd, g, b, *, layer, ctx_row, half, j, tm, sub):
    bsz, s, _ = x.shape
    tm = _row_tile(s, tm)
    sub = min(sub, tm)
    return pl.pallas_call(
        functools.partial(_ffn_body, j=j, sub=sub),
        grid=(bsz, s // tm),
        in_specs=[
            pl.BlockSpec((None, tm, D_MODEL), lambda bi, t: (bi, t, 0)),
            _mods_spec(layer, ctx_row),
            _const_spec(wgu.shape, (layer, half)), _const_spec(wd.shape, (layer, half)),
            _const_spec(g.shape, (layer, j)), _const_spec(b.shape, (layer, j)),
        ],
        out_specs=pl.BlockSpec((None, tm, D_MODEL), lambda bi, t: (bi, t, 0)),
        out_shape=jax.ShapeDtypeStruct(x.shape, F32),
        compiler_params=_params(2),
        name="ffn",
    )(x, mods, wgu, wd, g, b)


def _gelu(x):
    return jax.nn.gelu(x)


def _inproj_body(*refs, rope, sub):
    if rope:
        (x_ref, m_ref, w_ref, cos_ref, sin_ref, sw_ref, sb_ref, sg_ref, sbn_ref, dft_ref,
         q_ref, k_ref, v_ref, s_ref, ab_ref, stage_ref) = refs
    else:
        (x_ref, m_ref, w_ref, sw_ref, sb_ref, sg_ref, sbn_ref, dft_ref,
         q_ref, k_ref, v_ref, s_ref, ab_ref) = refs
    lane = lax.broadcasted_iota(jnp.int32, (sub, LANES), 1)
    low_half = lane < HEAD_DIM
    first = (lane % 32) < 16

    for r in range(x_ref.shape[0] // sub):
        rows = slice(r * sub, (r + 1) * sub)
        x = x_ref[rows, :]
        xm = (x * (1 + m_ref[4:5, :]) + m_ref[3:4, :]).astype(BF16)

        if rope:
            cos = cos_ref[rows, :]
            sin = sin_ref[rows, :]

            def rot(t, cos=cos, sin=sin):
                partner = jnp.where(first, pltpu.roll(t, LANES - 16, axis=1), pltpu.roll(t, 16, axis=1))
                return t * cos + partner * sin
        else:
            def rot(t):
                return t

        q = _dot(xm, w_ref[:, IN_Q:IN_K])
        for c in range(Q_W // LANES):
            sl = slice(c * LANES, (c + 1) * LANES)
            q_ref[rows, sl] = (rot(q[:, sl]) * (HEAD_DIM ** -0.5)).astype(BF16)
        kv = _dot(xm, w_ref[:, IN_K:IN_U])
        for t, o_ref in ((rot(kv[:, :KV_W]), k_ref), (kv[:, KV_W:], v_ref)):
            swapped = pltpu.roll(t, HEAD_DIM, axis=1)
            o_ref[rows, :LANES] = jnp.where(low_half, t, swapped).astype(BF16)
            o_ref[rows, LANES:] = jnp.where(low_half, swapped, t).astype(BF16)

        u = _gelu(_dot(xm, w_ref[:, IN_U:IN_Z]))
        z = _gelu(_dot(xm, w_ref[:, IN_Z:IN_F]))
        zn = _layer_norm(z, sg_ref[...], sbn_ref[...]).astype(BF16)
        for ci in range(sub // CHUNK):
            rs = slice(ci * CHUNK, (ci + 1) * CHUNK)
            ro = slice(r * sub + ci * CHUNK, r * sub + (ci + 1) * CHUNK)
            for gi in range(SGU_GROUPS):
                cs = slice(gi * LANES, (gi + 1) * LANES)
                mixed = _dot(sw_ref[gi], zn[rs, cs]) + sb_ref[gi]
                s_ref[ro, cs] = (u[rs, cs] * mixed).astype(BF16)

        f = _dot(xm, w_ref[:, IN_F:IN_END]).astype(BF16)
        ab = _dot(f, dft_ref[...])
        if rope:
            n_cb = 2 * FNET_WIDTH // LANES
            n2 = sub // N_RES
            base = r * n2 * STAGE_PITCH
            for g in range(n2):
                for cb in range(n_cb):
                    stage_ref[cb, base + g * STAGE_PITCH:base + g * STAGE_PITCH + N_RES, :] = (
                        ab[g * N_RES:(g + 1) * N_RES, cb * LANES:(cb + 1) * LANES])
            for j in range(N_RES):
                cols = slice(j * 2 * FNET_WIDTH, (j + 1) * 2 * FNET_WIDTH)
                piece = [stage_ref[cb, pl.ds(base + j, n2, stride=STAGE_PITCH), :] for cb in range(n_cb)]
                ab_ref[r * n2:(r + 1) * n2, cols] = jnp.concatenate(piece, axis=1).astype(BF16)
        else:
            ab_ref[rows, :] = ab.astype(BF16)


def _inproj_call(x, mods, w_in, tables, sw, sb, sg, sbn, dft_c, *, layer, ctx_row, tm, sub):
    bsz, s, _ = x.shape
    tm = _row_tile(s, tm)
    sub = min(sub, tm)
    rope = tables is not None
    row = lambda w: pl.BlockSpec((None, tm, w), lambda bi, t: (bi, t, 0))
    in_specs = [row(D_MODEL), _mods_spec(layer, ctx_row), _const_spec(w_in.shape, (layer,))]
    args = [x, mods, w_in]
    if rope:
        in_specs += [pl.BlockSpec((tm, LANES), lambda bi, t: (t, 0))] * 2
        args += list(tables)
    in_specs += [_const_spec(a.shape, (layer,)) for a in (sw, sb, sg, sbn)] + [_const_spec(dft_c.shape)]
    args += [sw, sb, sg, sbn, dft_c]
    widths = (Q_W, KV_DUP_W, KV_DUP_W, SGU_WIDTH, 2 * FNET_WIDTH)
    out_specs = [row(w) for w in widths]
    out_shape = [jax.ShapeDtypeStruct((bsz, s, w), BF16) for w in widths]
    scratch = []
    if rope:
        ab_w = N_RES * 2 * FNET_WIDTH
        out_specs[-1] = pl.BlockSpec((None, tm // N_RES, ab_w), lambda bi, t: (bi, t, 0))
        out_shape[-1] = jax.ShapeDtypeStruct((bsz, s // N_RES, ab_w), BF16)
        scratch = [pltpu.VMEM((2 * FNET_WIDTH // LANES, tm // N_RES * STAGE_PITCH, LANES), F32)]
    return pl.pallas_call(
        functools.partial(_inproj_body, rope=rope, sub=sub),
        grid=(bsz, s // tm),
        in_specs=in_specs,
        out_specs=out_specs,
        out_shape=out_shape,
        scratch_shapes=scratch,
        compiler_params=_params(2),
        name="inproj",
    )(*args)


def _attn_body(*refs, band, nq, nb, layer):
    if band:
        (sink_ref, q_ref, kp_ref, km_ref, kn_ref, vp_ref, vm_ref, vn_ref, kx_ref, vx_ref,
         o_ref, kw_ref, vw_ref) = refs
    else:
        sink_ref, q_ref, kx_ref, vx_ref, o_ref = refs
    blk = ATT_BLOCK
    n_ctx = kx_ref.shape[0]
    t = pl.program_id(1)
    row = lax.broadcasted_iota(jnp.int32, (blk, blk), 0)
    col = lax.broadcasted_iota(jnp.int32, (blk, blk), 1)
    lo = col < HEAD_DIM
    zero = jnp.zeros((blk, blk), BF16)
    nt_dims = (((1,), (1,)), ((), ()))
    if band:
        kw_ref[0:blk] = kp_ref[...]
        kw_ref[blk:(nq + 1) * blk] = km_ref[...]
        kw_ref[(nq + 1) * blk:(nq + 2) * blk] = kn_ref[...]
        vw_ref[0:blk] = vp_ref[...]
        vw_ref[blk:(nq + 1) * blk] = vm_ref[...]
        vw_ref[(nq + 1) * blk:(nq + 2) * blk] = vn_ref[...]

    for qi in range(nq):
        rows = slice(qi * blk, (qi + 1) * blk)
        if band:
            gi = t * nq + qi
            prev_ok = jnp.logical_and(col >= row, gi > 0)
            next_ok = jnp.logical_and(col <= row, gi < nb - 1)
        for kh in range(N_KV_HEADS):
            cs = slice(kh * LANES, (kh + 1) * LANES)
            qs = []
            for pr in range(GQA_GROUP // 2):
                pair = q_ref[rows, (2 * kh + pr) * LANES:(2 * kh + pr + 1) * LANES]
                qs.append(jnp.where(lo, pair, zero))
                qs.append(jnp.where(lo, zero, pair))
            qstack = jnp.concatenate(qs, axis=0)
            sc_ctx = lax.dot_general(qstack, kx_ref[:, cs], nt_dims, preferred_element_type=F32)
            if band:
                win = slice(qi * blk, (qi + 3) * blk)
                sc_band = lax.dot_general(qstack, kw_ref[win, cs], nt_dims, preferred_element_type=F32)
            ps, dens = [], []
            for g in range(GQA_GROUP):
                hr = slice(g * blk, (g + 1) * blk)
                pieces = [sc_ctx[hr, c * blk:(c + 1) * blk] for c in range(n_ctx // blk)]
                if band:
                    pieces = [jnp.where(prev_ok, sc_band[hr, 0:blk], NEG_INF),
                              sc_band[hr, blk:2 * blk],
                              jnp.where(next_ok, sc_band[hr, 2 * blk:3 * blk], NEG_INF)] + pieces
                sink = sink_ref[layer, kh * GQA_GROUP + g]
                mx = pieces[0]
                for pc in pieces[1:]:
                    mx = jnp.maximum(mx, pc)
                m = jnp.maximum(jnp.max(mx, axis=1, keepdims=True), sink)
                es = [jnp.exp(pc - m) for pc in pieces]
                tot = es[0]
                for e in es[1:]:
                    tot = tot + e
                dens.append(jnp.sum(tot, axis=1, keepdims=True) + jnp.exp(sink - m))
                ps.append(jnp.concatenate([e.astype(BF16) for e in es], axis=1))
            pall = jnp.concatenate(ps, axis=0)
            if band:
                o = _dot(pall[:, :3 * blk], vw_ref[win, cs]) + _dot(pall[:, 3 * blk:], vx_ref[:, cs])
            else:
                o = _dot(pall, vx_ref[:, cs])
            outs = [o[g * blk:(g + 1) * blk] / dens[g] for g in range(GQA_GROUP)]
            for pr in range(GQA_GROUP // 2):
                pair = jnp.where(lo, outs[2 * pr], outs[2 * pr + 1])
                o_ref[rows, (2 * kh + pr) * LANES:(2 * kh + pr + 1) * LANES] = pair.astype(BF16)


def _attn_call(sink, q, k, v, kx, vx, *, layer, band, tq):
    bsz, s, _ = q.shape
    n_ctx = kx.shape[1]
    nb = s // ATT_BLOCK
    tq = _row_tile(s, tq)
    nq = tq // ATT_BLOCK
    qspec = pl.BlockSpec((None, tq, Q_W), lambda bi, i: (bi, i, 0))
    xspec = pl.BlockSpec((None, n_ctx, KV_DUP_W), lambda bi, i: (bi, 0, 0))
    sspec = pl.BlockSpec(memory_space=pltpu.SMEM)
    scratch = []
    if band:
        edge = lambda f: pl.BlockSpec((None, ATT_BLOCK, KV_DUP_W), lambda bi, i: (bi, f(i), 0))
        prev = edge(lambda i: jnp.maximum(i * nq - 1, 0))
        nxt = edge(lambda i: jnp.minimum((i + 1) * nq, nb - 1))
        main = pl.BlockSpec((None, tq, KV_DUP_W), lambda bi, i: (bi, i, 0))
        in_specs = [sspec, qspec] + [prev, main, nxt] * 2 + [xspec, xspec]
        args = (sink, q, k, k, k, v, v, v, kx, vx)
        scratch = [pltpu.VMEM(((nq + 2) * ATT_BLOCK, KV_DUP_W), BF16)] * 2
    else:
        in_specs = [sspec, qspec, xspec, xspec]
        args = (sink, q, kx, vx)
    return pl.pallas_call(
        functools.partial(_attn_body, band=band, nq=nq, nb=nb, layer=layer),
        grid=(bsz, s // tq),
        in_specs=in_specs,
        out_specs=qspec,
        out_shape=jax.ShapeDtypeStruct((bsz, s, Q_W), BF16),
        scratch_shapes=scratch,
        compiler_params=_params(2),
        name="attn_band" if band else "attn_ctx",
    )(*args)


def _fourier_body(cn_ref, nsn_ref, ab_ref, o_ref, *, scale):
    y = _dot(cn_ref[...], ab_ref[:, :FNET_WIDTH]) + _dot(nsn_ref[...], ab_ref[:, FNET_WIDTH:])
    o_ref[...] = (y * scale).astype(BF16)


def _fourier_call(cn, nsn, ab, *, tk):
    bsz, s, _ = ab.shape
    tk = _row_tile(s, tk)
    scale = 1.0 / math.sqrt(s * FNET_GROUP_W)
    return pl.pallas_call(
        functools.partial(_fourier_body, scale=scale),
        grid=(bsz, s // tk),
        in_specs=[
            pl.BlockSpec((tk, s), lambda bi, t: (t, 0)),
            pl.BlockSpec((tk, s), lambda bi, t: (t, 0)),
            pl.BlockSpec((None, s, 2 * FNET_WIDTH), lambda bi, t: (bi, 0, 0)),
        ],
        out_specs=pl.BlockSpec((None, tk, FNET_WIDTH), lambda bi, t: (bi, t, 0)),
        out_shape=jax.ShapeDtypeStruct((bsz, s, FNET_WIDTH), BF16),
        compiler_params=_params(2),
        name="fourier",
    )(cn, nsn, ab)


def _fourier_fact_body(ab_ref, m1_ref, g_ref, o_ref, t_ref, *, n_low, scale):
    w = FNET_WIDTH
    for j in range(N_RES):
        blk = ab_ref[:, j * 2 * w:(j + 1) * 2 * w]
        d = jnp.concatenate([blk[:, :w], blk[:, w:]], axis=0)
        r = _dot(m1_ref[j], d)
        rows = slice(j * n_low, (j + 1) * n_low)
        t_ref[rows, :w] = r[:n_low].astype(BF16)
        t_ref[rows, w:] = r[n_low:].astype(BF16)
    for kb in range(n_low // K2_BLK):
        x = jnp.concatenate(
            [t_ref[j * n_low + kb * K2_BLK:j * n_low + (kb + 1) * K2_BLK, :] for j in range(N_RES)], axis=0)
        y = _dot(g_ref[0], x[:, :w]) + _dot(g_ref[1], x[:, w:])
        for k1 in range(N_RES):
            o_ref[k1 * n_low + kb * K2_BLK:k1 * n_low + (kb + 1) * K2_BLK, :] = (
                y[k1 * K2_BLK:(k1 + 1) * K2_BLK] * scale).astype(BF16)


def _fourier_fact_call(m1, g, ab_view):
    bsz, n_low, _ = ab_view.shape
    s = n_low * N_RES
    scale = 1.0 / math.sqrt(s * FNET_GROUP_W)
    return pl.pallas_call(
        functools.partial(_fourier_fact_body, n_low=n_low, scale=scale),
        grid=(bsz,),
        in_specs=[
            pl.BlockSpec((None, n_low, N_RES * 2 * FNET_WIDTH), lambda bi: (bi, 0, 0)),
            _const_spec(m1.shape), _const_spec(g.shape),
        ],
        out_specs=pl.BlockSpec((None, s, FNET_WIDTH), lambda bi: (bi, 0, 0)),
        out_shape=jax.ShapeDtypeStruct((bsz, s, FNET_WIDTH), BF16),
        scratch_shapes=[pltpu.VMEM((s, 2 * FNET_WIDTH), BF16)],
        compiler_params=_params(1),
        name="fourier_fact",
    )(ab_view, m1, g)


def _merge_body(x_ref, m_ref, a_ref, s_ref, c_ref, wg_ref, wb_ref, wo_ref, g_ref, b_ref, o_ref, mg_ref, *, sub):
    for st in range(x_ref.shape[0] // sub):
        rows = slice(st * sub, (st + 1) * sub)
        x = x_ref[rows, :]
        h = (x * (1 + m_ref[4:5, :]) + m_ref[3:4, :]).astype(BF16)
        branches = (a_ref[rows, :], s_ref[rows, :], c_ref[rows, :])
        for n in range(D_MODEL // MXU_W):
            sl = slice(n * MXU_W, (n + 1) * MXU_W)
            acc = None
            for r in range(N_BRANCH):
                t = jax.nn.sigmoid(_dot(h, wg_ref[r, :, sl])) * _dot(branches[r], wb_ref[r, :, sl])
                acc = t if acc is None else acc + t
            mg_ref[rows, sl] = acc.astype(BF16)
        y = _dot(mg_ref[rows, :], wo_ref[...])
        o_ref[rows, :] = _layer_norm(ALPHA * x + m_ref[5:6, :] * y, g_ref[...], b_ref[...])


def _merge_call(x, mods, a, sgu, c, wg, wb, wo, g, b, *, layer, ctx_row, tm, sub):
    bsz, s, _ = x.shape
    tm = _row_tile(s, tm)
    sub = min(sub, tm)
    row = lambda w: pl.BlockSpec((None, tm, w), lambda bi, t: (bi, t, 0))
    return pl.pallas_call(
        functools.partial(_merge_body, sub=sub),
        grid=(bsz, s // tm),
        in_specs=[
            row(D_MODEL), _mods_spec(layer, ctx_row),
            row(Q_W), row(SGU_WIDTH), row(FNET_WIDTH),
            _const_spec(wg.shape, (layer,)), _const_spec(wb.shape, (layer,)), _const_spec(wo.shape, (layer,)),
            _const_spec(g.shape, (layer, 1)), _const_spec(b.shape, (layer, 1)),
        ],
        out_specs=row(D_MODEL),
        out_shape=jax.ShapeDtypeStruct(x.shape, F32),
        scratch_shapes=[pltpu.VMEM((tm, D_MODEL), BF16)],
        compiler_params=_params(2),
        name="merge",
    )(x, mods, a, sgu, c, wg, wb, wo, g, b)


def _rope_tables(s):
    pos = jnp.arange(s)
    row = (pos // GRID_W).astype(F32)
    col = (pos % GRID_W).astype(F32)
    axis_dim = HEAD_DIM // 2
    inv_freq = ROPE_THETA ** (-jnp.arange(0, axis_dim, 2, dtype=F32) / axis_dim)
    ang_r = row[:, None] * inv_freq[None, :]
    ang_c = col[:, None] * inv_freq[None, :]
    cos = jnp.concatenate([jnp.cos(ang_r)] * 2 + [jnp.cos(ang_c)] * 2, axis=1)
    sin = jnp.concatenate([-jnp.sin(ang_r), jnp.sin(ang_r), -jnp.sin(ang_c), jnp.sin(ang_c)], axis=1)
    return jnp.tile(cos, (1, 2)), jnp.tile(sin, (1, 2))


def _dft_angles(n):
    k = jnp.arange(n, dtype=jnp.int32)
    r = (k[:, None] * k[None, :]) % n
    return r.astype(F32) * (2.0 * math.pi / n)


def _position_dft(n):
    ang = _dft_angles(n)
    return jnp.cos(ang).astype(BF16), (-jnp.sin(ang)).astype(BF16)


def _factored_dft(s):
    n_low = s // N_RES
    idx = jnp.arange(n_low, dtype=jnp.int32)
    n = jnp.arange(N_RES, dtype=jnp.int32)[:, None, None] + N_RES * idx[None, None, :]
    ang = ((idx[None, :, None] * n) % s).astype(F32) * (2.0 * math.pi / s)
    c, sn = jnp.cos(ang), jnp.sin(ang)
    m1 = jnp.concatenate([jnp.concatenate([c, -sn], axis=2), jnp.concatenate([sn, c], axis=2)], axis=1)
    ang_r = _dft_angles(N_RES)
    eye = jnp.eye(K2_BLK, dtype=F32)
    g = jnp.stack([jnp.kron(jnp.cos(ang_r), eye), -jnp.kron(jnp.sin(ang_r), eye)])
    return m1.astype(BF16), g.astype(BF16)


def _channel_dft():
    ang = _dft_angles(FNET_GROUP_W)
    eye = jnp.eye(FNET_GROUPS, dtype=F32)
    return jnp.concatenate([jnp.kron(eye, jnp.cos(ang)), jnp.kron(eye, jnp.sin(ang))], axis=1).astype(BF16)


def kernel(x, c, ctx, c_ctx, w_mod, b_mod, w_ffn_up, w_ffn_down, ln_g, ln_b, w_in, attn_sink,
           sgu_w, sgu_b, sgu_ln_g, sgu_ln_b, w_gate, w_branch, w_out):
    depth = w_mod.shape[0]
    bsz, s, _ = x.shape
    n_ctx = ctx.shape[1]

    w_gu = _pack_up_call(w_ffn_up)
    w_d = _pack_down_call(w_ffn_down)
    w_in_b = w_in.astype(BF16)
    w_gate_b = w_gate.astype(BF16)
    w_branch_b = w_branch.astype(BF16)
    w_out_b = w_out.astype(BF16)
    sgu_w_b = sgu_w.astype(BF16)
    sgu_b_col = sgu_b[..., None]
    ln_g4 = ln_g[:, :, None, :]
    ln_b4 = ln_b[:, :, None, :]
    sgu_g3 = sgu_ln_g[:, None, :]
    sgu_b3 = sgu_ln_b[:, None, :]

    tables = _rope_tables(s)
    dft_c = _channel_dft()
    m1_lat, g_lat = _factored_dft(s)
    cn_ctx, nsn_ctx = _position_dft(n_ctx)

    rows = -(-(bsz + 1) // 8) * 8
    cc = jnp.zeros((rows, D_MODEL), F32).at[:bsz].set(c).at[bsz].set(c_ctx)
    mods = _mods_call(cc, w_mod, b_mod).reshape(depth, rows, N_MOD, D_MODEL)

    def ffn(xx, i, ctx_row, half, j, tm):
        return _ffn_call(xx, mods, w_gu, w_d, ln_g4, ln_b4, layer=i, ctx_row=ctx_row, half=half, j=j,
                         tm=tm, sub=512)

    def inproj(xx, i, ctx_row, tb, tm):
        return _inproj_call(xx, mods, w_in_b, tb, sgu_w_b, sgu_b_col, sgu_g3, sgu_b3, dft_c,
                            layer=i, ctx_row=ctx_row, tm=tm, sub=512)

    def merge(xx, i, ctx_row, a, sg, cf, tm):
        return _merge_call(xx, mods, a, sg, cf, w_gate_b, w_branch_b, w_out_b, ln_g4, ln_b4,
                           layer=i, ctx_row=ctx_row, tm=tm, sub=512)

    x_lat, x_ctx = x, ctx
    for i in range(depth):
        last = i == depth - 1
        x_lat = ffn(x_lat, i, None, 0, 0, 1024)
        x_ctx = ffn(x_ctx, i, bsz, 0, 0, 256)

        q_c, k_c, v_c, s_c, ab_c = inproj(x_ctx, i, bsz, None, 256)
        q_l, k_l, v_l, s_l, ab_l = inproj(x_lat, i, None, tables, 1024)

        a_l = _attn_call(attn_sink, q_l, k_l, v_l, k_c, v_c, layer=i, band=True, tq=512)
        f_l = _fourier_fact_call(m1_lat, g_lat, ab_l)
        x_lat = merge(x_lat, i, None, a_l, s_l, f_l, 1024)
        x_lat = ffn(x_lat, i, None, 1, 2, 1024)
        if not last:
            a_c = _attn_call(attn_sink, q_c, None, None, k_c, v_c, layer=i, band=False, tq=256)
            f_c = _fourier_call(cn_ctx, nsn_ctx, ab_c, tk=256)
            x_ctx = merge(x_ctx, i, bsz, a_c, s_c, f_c, 256)
            x_ctx = ffn(x_ctx, i, bsz, 1, 2, 256)
    return x_lat
```

```python
import functools
import math

import jax
import jax.numpy as jnp
from jax import lax
from jax.experimental import pallas as pl
from jax.experimental.pallas import tpu as pltpu

F32 = jnp.float32
BF16 = jnp.bfloat16

D_MODEL = 1024
GRID_W = 64
HEAD_DIM = 64
N_Q_HEADS = 8
N_KV_HEADS = 2
GQA_GROUP = N_Q_HEADS // N_KV_HEADS
ATT_BLOCK = 128
ROPE_THETA = 10000.0
Q_W = N_Q_HEADS * HEAD_DIM
KV_W = N_KV_HEADS * HEAD_DIM
CHUNK = 128
SGU_GROUPS = 4
SGU_WIDTH = 512
FNET_GROUPS = 4
FNET_WIDTH = 512
FNET_GROUP_W = FNET_WIDTH // FNET_GROUPS
N_BRANCH = 3
D_FF = 2752
N_MOD = 9
MODEL_DEPTH = 4
ALPHA = (2 * MODEL_DEPTH) ** 0.25
LN_EPS = 1e-5
NEG_INF = -1e30

LANES = 128
MXU_W = 256
D_FF_PAD = -(-D_FF // MXU_W) * MXU_W
UP_OFF = D_FF % LANES
UP0 = D_FF - UP_OFF
assert UP0 + D_FF_PAD == 2 * D_FF
KV_DUP_W = 2 * KV_W
IN_Q, IN_K, IN_U, IN_Z, IN_F, IN_END = 0, 512, 768, 1280, 1792, 2304
VMEM_LIMIT = 56 * 1024 * 1024
N_RES = 32
K2_BLK = 16
STAGE_PITCH = 40


def _params(n_axes):
    return pltpu.CompilerParams(dimension_semantics=("parallel",) * n_axes,
                                vmem_limit_bytes=VMEM_LIMIT)


def _const_spec(shape, lead=()):
    nl = len(lead)
    block = (None,) * nl + tuple(shape[nl:])
    index = tuple(lead) + (0,) * (len(shape) - nl)
    return pl.BlockSpec(block, lambda *_: index, pipeline_mode=pl.Buffered(1))


def _mods_spec(layer, ctx_row):
    if ctx_row is None:
        return pl.BlockSpec((None, None, N_MOD, D_MODEL), lambda bi, t: (layer, bi, 0, 0))
    return pl.BlockSpec((None, None, N_MOD, D_MODEL), lambda bi, t: (layer, ctx_row, 0, 0))


def _row_tile(s, pref):
    t = min(s, pref)
    while s % t:
        t -= LANES
    return t


def _dot(a, b):
    return jnp.dot(a, b, preferred_element_type=F32)


def _layer_norm(y, g, b):
    mu = jnp.mean(y, axis=-1, keepdims=True)
    d = y - mu
    var = jnp.mean(d * d, axis=-1, keepdims=True)
    return d * lax.rsqrt(var + LN_EPS) * g + b


def _mods_body(cc_ref, w_ref, b_ref, o_ref):
    s = cc_ref[...]
    s = (s * jax.nn.sigmoid(s)).astype(BF16)
    o_ref[...] = _dot(s, w_ref[...].astype(BF16)) + b_ref[...]


def _mods_call(cc, w_mod, b_mod):
    depth, _, n = w_mod.shape
    rows = cc.shape[0]
    tn = 1024
    return pl.pallas_call(
        _mods_body,
        grid=(depth, n // tn),
        in_specs=[
            pl.BlockSpec((rows, D_MODEL), lambda l, j: (0, 0)),
            pl.BlockSpec((None, D_MODEL, tn), lambda l, j: (l, 0, j)),
            pl.BlockSpec((None, 1, tn), lambda l, j: (l, 0, j)),
        ],
        out_specs=pl.BlockSpec((None, rows, tn), lambda l, j: (l, 0, j)),
        out_shape=jax.ShapeDtypeStruct((depth, rows, n), F32),
        compiler_params=_params(2),
        name="mods",
    )(cc, w_mod, b_mod.reshape(depth, 1, n))


def _pack_up_body(w_ref, o_ref):
    rows = w_ref.shape[0]
    lane = lax.broadcasted_iota(jnp.int32, (rows, D_FF_PAD), 1)
    real = lane < D_FF
    o_ref[:, :D_FF_PAD] = jnp.where(real, w_ref[:, :D_FF_PAD], 0.0).astype(BF16)
    up = pltpu.roll(w_ref[:, UP0:], D_FF_PAD - UP_OFF, axis=1)
    o_ref[:, D_FF_PAD:] = jnp.where(real, up, 0.0).astype(BF16)


def _pack_up_call(w_up):
    depth, two, d, n = w_up.shape
    tr = 256
    return pl.pallas_call(
        _pack_up_body,
        grid=(depth, two, d // tr),
        in_specs=[pl.BlockSpec((None, None, tr, n), lambda l, h, r: (l, h, r, 0))],
        out_specs=pl.BlockSpec((None, None, tr, 2 * D_FF_PAD), lambda l, h, r: (l, h, r, 0)),
        out_shape=jax.ShapeDtypeStruct((depth, two, d, 2 * D_FF_PAD), BF16),
        compiler_params=_params(3),
        name="pack_up",
    )(w_up)


def _pack_down_body(w_ref, o_ref):
    o_ref[:D_FF, :] = w_ref[...].astype(BF16)
    o_ref[D_FF:, :] = jnp.zeros((D_FF_PAD - D_FF, w_ref.shape[1]), BF16)


def _pack_down_call(w_down):
    depth, two, n, d = w_down.shape
    return pl.pallas_call(
        _pack_down_body,
        grid=(depth, two),
        in_specs=[pl.BlockSpec((None, None, n, d), lambda l, h: (l, h, 0, 0))],
        out_specs=pl.BlockSpec((None, None, D_FF_PAD, d), lambda l, h: (l, h, 0, 0)),
        out_shape=jax.ShapeDtypeStruct((depth, two, D_FF_PAD, d), BF16),
        compiler_params=_params(2),
        name="pack_down",
    )(w_down)


def _ffn_body(x_ref, m_ref, wgu_ref, wd_ref, g_ref, b_ref, o_ref, *, j, sub):
    shift = m_ref[3 * j:3 * j + 1, :]
    scale = m_ref[3 * j + 1:3 * j + 2, :]
    gate = m_ref[3 * j + 2:3 * j + 3, :]
    for r in range(x_ref.shape[0] // sub):
        rows = slice(r * sub, (r + 1) * sub)
        x = x_ref[rows, :]
        xm = (x * (1 + scale) + shift).astype(BF16)
        acc = None
        for c in range(D_FF_PAD // MXU_W):
            sl = slice(c * MXU_W, (c + 1) * MXU_W)
            su = slice(D_FF_PAD + c * MXU_W, D_FF_PAD + (c + 1) * MXU_W)
            gg = _dot(xm, wgu_ref[:, sl])
            uu = _dot(xm, wgu_ref[:, su])
            h = (gg * jax.nn.sigmoid(gg) * uu).astype(BF16)
            d = _dot(h, wd_ref[sl, :])
            acc = d if acc is None else acc + d
        y = ALPHA * x + (0.5 * gate) * acc
        o_ref[rows, :] = _layer_norm(y, g_ref[...], b_ref[...])


def _ffn_call(x, mods, wgu, wd, g, b, *, layer, ctx_row, half, j, tm, sub):
    bsz, s, _ = x.shape
    tm = _row_tile(s, tm)
    sub = min(sub, tm)
    return pl.pallas_call(
        functools.partial(_ffn_body, j=j, sub=sub),
        grid=(bsz, s // tm),
        in_specs=[
            pl.BlockSpec((None, tm, D_MODEL), lambda bi, t: (bi, t, 0)),
            _mods_spec(layer, ctx_row),
            _const_spec(wgu.shape, (layer, half)), _const_spec(wd.shape, (layer, half)),
            _const_spec(g.shape, (layer, j)), _const_spec(b.shape, (layer, j)),
        ],
        out_specs=pl.BlockSpec((None, tm, D_MODEL), lambda bi, t: (bi, t, 0)),
        out_shape=jax.ShapeDtypeStruct(x.shape, F32),
        compiler_params=_params(2),
        name="ffn",
    )(x, mods, wgu, wd, g, b)


def _gelu(x):
    return jax.nn.gelu(x)


def _inproj_body(*refs, rope, sub):
    if rope:
        (x_ref, m_ref, w_ref, cos_ref, sin_ref, sw_ref, sb_ref, sg_ref, sbn_ref, dft_ref,
         q_ref, k_ref, v_ref, s_ref, ab_ref, stage_ref) = refs
    else:
        (x_ref, m_ref, w_ref, sw_ref, sb_ref, sg_ref, sbn_ref, dft_ref,
         q_ref, k_ref, v_ref, s_ref, ab_ref) = refs
    lane = lax.broadcasted_iota(jnp.int32, (sub, LANES), 1)
    low_half = lane < HEAD_DIM
    first = (lane % 32) < 16

    for r in range(x_ref.shape[0] // sub):
        rows = slice(r * sub, (r + 1) * sub)
        x = x_ref[rows, :]
        xm = (x * (1 + m_ref[4:5, :]) + m_ref[3:4, :]).astype(BF16)

        if rope:
            cos = cos_ref[rows, :]
            sin = sin_ref[rows, :]

            def rot(t, cos=cos, sin=sin):
                partner = jnp.where(first, pltpu.roll(t, LANES - 16, axis=1), pltpu.roll(t, 16, axis=1))
                return t * cos + partner * sin
        else:
            def rot(t):
                return t

        q = _dot(xm, w_ref[:, IN_Q:IN_K])
        for c in range(Q_W // LANES):
            sl = slice(c * LANES, (c + 1) * LANES)
            q_ref[rows, sl] = (rot(q[:, sl]) * (HEAD_DIM ** -0.5)).astype(BF16)
        kv = _dot(xm, w_ref[:, IN_K:IN_U])
        for t, o_ref in ((rot(kv[:, :KV_W]), k_ref), (kv[:, KV_W:], v_ref)):
            swapped = pltpu.roll(t, HEAD_DIM, axis=1)
            o_ref[rows, :LANES] = jnp.where(low_half, t, swapped).astype(BF16)
            o_ref[rows, LANES:] = jnp.where(low_half, swapped, t).astype(BF16)

        u = _gelu(_dot(xm, w_ref[:, IN_U:IN_Z]))
        z = _gelu(_dot(xm, w_ref[:, IN_Z:IN_F]))
        zn = _layer_norm(z, sg_ref[...], sbn_ref[...]).astype(BF16)
        for ci in range(sub // CHUNK):
            rs = slice(ci * CHUNK, (ci + 1) * CHUNK)
            ro = slice(r * sub + ci * CHUNK, r * sub + (ci + 1) * CHUNK)
            for gi in range(SGU_GROUPS):
                cs = slice(gi * LANES, (gi + 1) * LANES)
                mixed = _dot(sw_ref[gi], zn[rs, cs]) + sb_ref[gi]
                s_ref[ro, cs] = (u[rs, cs] * mixed).astype(BF16)

        f = _dot(xm, w_ref[:, IN_F:IN_END]).astype(BF16)
        n_cb = 2 * FNET_WIDTH // LANES
        ab = [None] * n_cb
        for gi in range(FNET_GROUPS):
            cs_ = _dot(f[:, gi * FNET_GROUP_W:(gi + 1) * FNET_GROUP_W], dft_ref[...])
            ab[gi] = cs_[:, :FNET_GROUP_W]
            ab[FNET_GROUPS + gi] = cs_[:, FNET_GROUP_W:]
        if rope:
            n2 = sub // N_RES
            base = r * n2 * STAGE_PITCH
            for g in range(n2):
                for cb in range(n_cb):
                    stage_ref[cb, base + g * STAGE_PITCH:base + g * STAGE_PITCH + N_RES, :] = (
                        ab[cb][g * N_RES:(g + 1) * N_RES, :])
            for j in range(N_RES):
                cols = slice(j * 2 * FNET_WIDTH, (j + 1) * 2 * FNET_WIDTH)
                piece = [stage_ref[cb, pl.ds(base + j, n2, stride=STAGE_PITCH), :] for cb in range(n_cb)]
                ab_ref[r * n2:(r + 1) * n2, cols] = jnp.concatenate(piece, axis=1).astype(BF16)
        else:
            for cb in range(n_cb):
                ab_ref[rows, cb * LANES:(cb + 1) * LANES] = ab[cb].astype(BF16)


def _inproj_call(x, mods, w_in, tables, sw, sb, sg, sbn, dft_c, *, layer, ctx_row, tm, sub):
    bsz, s, _ = x.shape
    tm = _row_tile(s, tm)
    sub = min(sub, tm)
    rope = tables is not None
    row = lambda w: pl.BlockSpec((None, tm, w), lambda bi, t: (bi, t, 0))
    in_specs = [row(D_MODEL), _mods_spec(layer, ctx_row), _const_spec(w_in.shape, (layer,))]
    args = [x, mods, w_in]
    if rope:
        in_specs += [pl.BlockSpec((tm, LANES), lambda bi, t: (t, 0))] * 2
        args += list(tables)
    in_specs += [_const_spec(a.shape, (layer,)) for a in (sw, sb, sg, sbn)] + [_const_spec(dft_c.shape)]
    args += [sw, sb, sg, sbn, dft_c]
    widths = (Q_W, KV_DUP_W, KV_DUP_W, SGU_WIDTH, 2 * FNET_WIDTH)
    out_specs = [row(w) for w in widths]
    out_shape = [jax.ShapeDtypeStruct((bsz, s, w), BF16) for w in widths]
    scratch = []
    if rope:
        ab_w = N_RES * 2 * FNET_WIDTH
        out_specs[-1] = pl.BlockSpec((None, tm // N_RES, ab_w), lambda bi, t: (bi, t, 0))
        out_shape[-1] = jax.ShapeDtypeStruct((bsz, s // N_RES, ab_w), BF16)
        scratch = [pltpu.VMEM((2 * FNET_WIDTH // LANES, tm // N_RES * STAGE_PITCH, LANES), F32)]
    return pl.pallas_call(
        functools.partial(_inproj_body, rope=rope, sub=sub),
        grid=(bsz, s // tm),
        in_specs=in_specs,
        out_specs=out_specs,
        out_shape=out_shape,
        scratch_shapes=scratch,
        compiler_params=_params(2),
        name="inproj",
    )(*args)


def _attn_body(*refs, band, nq, nb, layer):
    if band:
        (sink_ref, q_ref, kp_ref, km_ref, kn_ref, vp_ref, vm_ref, vn_ref, kx_ref, vx_ref,
         o_ref, kw_ref, vw_ref) = refs
    else:
        sink_ref, q_ref, kx_ref, vx_ref, o_ref = refs
    blk = ATT_BLOCK
    n_ctx = kx_ref.shape[0]
    t = pl.program_id(1)
    row = lax.broadcasted_iota(jnp.int32, (blk, blk), 0)
    col = lax.broadcasted_iota(jnp.int32, (blk, blk), 1)
    lo = col < HEAD_DIM
    zero = jnp.zeros((blk, blk), BF16)
    nt_dims = (((1,), (1,)), ((), ()))
    if band:
        kw_ref[0:blk] = kp_ref[...]
        kw_ref[blk:(nq + 1) * blk] = km_ref[...]
        kw_ref[(nq + 1) * blk:(nq + 2) * blk] = kn_ref[...]
        vw_ref[0:blk] = vp_ref[...]
        vw_ref[blk:(nq + 1) * blk] = vm_ref[...]
        vw_ref[(nq + 1) * blk:(nq + 2) * blk] = vn_ref[...]

    for qi in range(nq):
        rows = slice(qi * blk, (qi + 1) * blk)
        if band:
            gi = t * nq + qi
            prev_ok = jnp.logical_and(col >= row, gi > 0)
            next_ok = jnp.logical_and(col <= row, gi < nb - 1)
        for kh in range(N_KV_HEADS):
            cs = slice(kh * LANES, (kh + 1) * LANES)
            qs = []
            for pr in range(GQA_GROUP // 2):
                pair = q_ref[rows, (2 * kh + pr) * LANES:(2 * kh + pr + 1) * LANES]
                qs.append(jnp.where(lo, pair, zero))
                qs.append(jnp.where(lo, zero, pair))
            qstack = jnp.concatenate(qs, axis=0)
            sc_ctx = lax.dot_general(qstack, kx_ref[:, cs], nt_dims, preferred_element_type=F32)
            if band:
                win = slice(qi * blk, (qi + 3) * blk)
                sc_band = lax.dot_general(qstack, kw_ref[win, cs], nt_dims, preferred_element_type=F32)
            ps, dens = [], []
            for g in range(GQA_GROUP):
                hr = slice(g * blk, (g + 1) * blk)
                pieces = [sc_ctx[hr, c * blk:(c + 1) * blk] for c in range(n_ctx // blk)]
                if band:
                    pieces = [jnp.where(prev_ok, sc_band[hr, 0:blk], NEG_INF),
                              sc_band[hr, blk:2 * blk],
                              jnp.where(next_ok, sc_band[hr, 2 * blk:3 * blk], NEG_INF)] + pieces
                sink = sink_ref[layer, kh * GQA_GROUP + g]
                mx = pieces[0]
                for pc in pieces[1:]:
                    mx = jnp.maximum(mx, pc)
                m = jnp.maximum(jnp.max(mx, axis=1, keepdims=True), sink)
                es = [jnp.exp(pc - m) for pc in pieces]
                tot = es[0]
                for e in es[1:]:
                    tot = tot + e
                dens.append(jnp.sum(tot, axis=1, keepdims=True) + jnp.exp(sink - m))
                ps.append(jnp.concatenate([e.astype(BF16) for e in es], axis=1))
            pall = jnp.concatenate(ps, axis=0)
            if band:
                o = _dot(pall[:, :3 * blk], vw_ref[win, cs]) + _dot(pall[:, 3 * blk:], vx_ref[:, cs])
            else:
                o = _dot(pall, vx_ref[:, cs])
            outs = [o[g * blk:(g + 1) * blk] / dens[g] for g in range(GQA_GROUP)]
            for pr in range(GQA_GROUP // 2):
                pair = jnp.where(lo, outs[2 * pr], outs[2 * pr + 1])
                o_ref[rows, (2 * kh + pr) * LANES:(2 * kh + pr + 1) * LANES] = pair.astype(BF16)


def _attn_call(sink, q, k, v, kx, vx, *, layer, band, tq):
    bsz, s, _ = q.shape
    n_ctx = kx.shape[1]
    nb = s // ATT_BLOCK
    tq = _row_tile(s, tq)
    nq = tq // ATT_BLOCK
    qspec = pl.BlockSpec((None, tq, Q_W), lambda bi, i: (bi, i, 0))
    xspec = pl.BlockSpec((None, n_ctx, KV_DUP_W), lambda bi, i: (bi, 0, 0))
    sspec = pl.BlockSpec(memory_space=pltpu.SMEM)
    scratch = []
    if band:
        edge = lambda f: pl.BlockSpec((None, ATT_BLOCK, KV_DUP_W), lambda bi, i: (bi, f(i), 0))
        prev = edge(lambda i: jnp.maximum(i * nq - 1, 0))
        nxt = edge(lambda i: jnp.minimum((i + 1) * nq, nb - 1))
        main = pl.BlockSpec((None, tq, KV_DUP_W), lambda bi, i: (bi, i, 0))
        in_specs = [sspec, qspec] + [prev, main, nxt] * 2 + [xspec, xspec]
        args = (sink, q, k, k, k, v, v, v, kx, vx)
        scratch = [pltpu.VMEM(((nq + 2) * ATT_BLOCK, KV_DUP_W), BF16)] * 2
    else:
        in_specs = [sspec, qspec, xspec, xspec]
        args = (sink, q, kx, vx)
    return pl.pallas_call(
        functools.partial(_attn_body, band=band, nq=nq, nb=nb, layer=layer),
        grid=(bsz, s // tq),
        in_specs=in_specs,
        out_specs=qspec,
        out_shape=jax.ShapeDtypeStruct((bsz, s, Q_W), BF16),
        scratch_shapes=scratch,
        compiler_params=_params(2),
        name="attn_band" if band else "attn_ctx",
    )(*args)


def _fourier_body(cn_ref, nsn_ref, ab_ref, o_ref, *, scale):
    y = _dot(cn_ref[...], ab_ref[:, :FNET_WIDTH]) + _dot(nsn_ref[...], ab_ref[:, FNET_WIDTH:])
    o_ref[...] = (y * scale).astype(BF16)


def _fourier_call(cn, nsn, ab, *, tk):
    bsz, s, _ = ab.shape
    tk = _row_tile(s, tk)
    scale = 1.0 / math.sqrt(s * FNET_GROUP_W)
    return pl.pallas_call(
        functools.partial(_fourier_body, scale=scale),
        grid=(bsz, s // tk),
        in_specs=[
            pl.BlockSpec((tk, s), lambda bi, t: (t, 0)),
            pl.BlockSpec((tk, s), lambda bi, t: (t, 0)),
            pl.BlockSpec((None, s, 2 * FNET_WIDTH), lambda bi, t: (bi, 0, 0)),
        ],
        out_specs=pl.BlockSpec((None, tk, FNET_WIDTH), lambda bi, t: (bi, t, 0)),
        out_shape=jax.ShapeDtypeStruct((bsz, s, FNET_WIDTH), BF16),
        compiler_params=_params(2),
        name="fourier",
    )(cn, nsn, ab)


def _fourier_fact_body(ab_ref, m1_ref, g_ref, o_ref, t_ref, *, n_low, scale):
    w = FNET_WIDTH
    for j in range(N_RES):
        blk = ab_ref[:, j * 2 * w:(j + 1) * 2 * w]
        d = jnp.concatenate([blk[:, :w], blk[:, w:]], axis=0)
        r = _dot(m1_ref[j], d)
        rows = slice(j * n_low, (j + 1) * n_low)
        t_ref[rows, :w] = r[:n_low].astype(BF16)
        t_ref[rows, w:] = r[n_low:].astype(BF16)
    for kb in range(n_low // K2_BLK):
        x = jnp.concatenate(
            [t_ref[j * n_low + kb * K2_BLK:j * n_low + (kb + 1) * K2_BLK, :] for j in range(N_RES)], axis=0)
        y = _dot(g_ref[0], x[:, :w]) + _dot(g_ref[1], x[:, w:])
        for k1 in range(N_RES):
            o_ref[k1 * n_low + kb * K2_BLK:k1 * n_low + (kb + 1) * K2_BLK, :] = (
                y[k1 * K2_BLK:(k1 + 1) * K2_BLK] * scale).astype(BF16)


def _fourier_fact_call(m1, g, ab_view):
    bsz, n_low, _ = ab_view.shape
    s = n_low * N_RES
    scale = 1.0 / math.sqrt(s * FNET_GROUP_W)
    return pl.pallas_call(
        functools.partial(_fourier_fact_body, n_low=n_low, scale=scale),
        grid=(bsz,),
        in_specs=[
            pl.BlockSpec((None, n_low, N_RES * 2 * FNET_WIDTH), lambda bi: (bi, 0, 0)),
            _const_spec(m1.shape), _const_spec(g.shape),
        ],
        out_specs=pl.BlockSpec((None, s, FNET_WIDTH), lambda bi: (bi, 0, 0)),
        out_shape=jax.ShapeDtypeStruct((bsz, s, FNET_WIDTH), BF16),
        scratch_shapes=[pltpu.VMEM((s, 2 * FNET_WIDTH), BF16)],
        compiler_params=_params(1),
        name="fourier_fact",
    )(ab_view, m1, g)


def _merge_body(x_ref, m_ref, a_ref, s_ref, c_ref, wg_ref, wb_ref, wo_ref, g_ref, b_ref, o_ref, mg_ref, *, sub):
    for st in range(x_ref.shape[0] // sub):
        rows = slice(st * sub, (st + 1) * sub)
        x = x_ref[rows, :]
        h = (x * (1 + m_ref[4:5, :]) + m_ref[3:4, :]).astype(BF16)
        branches = (a_ref[rows, :], s_ref[rows, :], c_ref[rows, :])
        for n in range(D_MODEL // MXU_W):
            sl = slice(n * MXU_W, (n + 1) * MXU_W)
            acc = None
            for r in range(N_BRANCH):
                t = jax.nn.sigmoid(_dot(h, wg_ref[r, :, sl])) * _dot(branches[r], wb_ref[r, :, sl])
                acc = t if acc is None else acc + t
            mg_ref[rows, sl] = acc.astype(BF16)
        y = _dot(mg_ref[rows, :], wo_ref[...])
        o_ref[rows, :] = _layer_norm(ALPHA * x + m_ref[5:6, :] * y, g_ref[...], b_ref[...])


def _merge_call(x, mods, a, sgu, c, wg, wb, wo, g, b, *, layer, ctx_row, tm, sub):
    bsz, s, _ = x.shape
    tm = _row_tile(s, tm)
    sub = min(sub, tm)
    row = lambda w: pl.BlockSpec((None, tm, w), lambda bi, t: (bi, t, 0))
    return pl.pallas_call(
        functools.partial(_merge_body, sub=sub),
        grid=(bsz, s // tm),
        in_specs=[
            row(D_MODEL), _mods_spec(layer, ctx_row),
            row(Q_W), row(SGU_WIDTH), row(FNET_WIDTH),
            _const_spec(wg.shape, (layer,)), _const_spec(wb.shape, (layer,)), _const_spec(wo.shape, (layer,)),
            _const_spec(g.shape, (layer, 1)), _const_spec(b.shape, (layer, 1)),
        ],
        out_specs=row(D_MODEL),
        out_shape=jax.ShapeDtypeStruct(x.shape, F32),
        scratch_shapes=[pltpu.VMEM((tm, D_MODEL), BF16)],
        compiler_params=_params(2),
        name="merge",
    )(x, mods, a, sgu, c, wg, wb, wo, g, b)


def _rope_tables(s):
    pos = jnp.arange(s)
    row = (pos // GRID_W).astype(F32)
    col = (pos % GRID_W).astype(F32)
    axis_dim = HEAD_DIM // 2
    inv_freq = ROPE_THETA ** (-jnp.arange(0, axis_dim, 2, dtype=F32) / axis_dim)
    ang_r = row[:, None] * inv_freq[None, :]
    ang_c = col[:, None] * inv_freq[None, :]
    cos = jnp.concatenate([jnp.cos(ang_r)] * 2 + [jnp.cos(ang_c)] * 2, axis=1)
    sin = jnp.concatenate([-jnp.sin(ang_r), jnp.sin(ang_r), -jnp.sin(ang_c), jnp.sin(ang_c)], axis=1)
    return jnp.tile(cos, (1, 2)), jnp.tile(sin, (1, 2))


def _dft_angles(n):
    k = jnp.arange(n, dtype=jnp.int32)
    r = (k[:, None] * k[None, :]) % n
    return r.astype(F32) * (2.0 * math.pi / n)


def _position_dft(n):
    ang = _dft_angles(n)
    return jnp.cos(ang).astype(BF16), (-jnp.sin(ang)).astype(BF16)


def _factored_dft(s):
    n_low = s // N_RES
    idx = jnp.arange(n_low, dtype=jnp.int32)
    n = jnp.arange(N_RES, dtype=jnp.int32)[:, None, None] + N_RES * idx[None, None, :]
    ang = ((idx[None, :, None] * n) % s).astype(F32) * (2.0 * math.pi / s)
    c, sn = jnp.cos(ang), jnp.sin(ang)
    m1 = jnp.concatenate([jnp.concatenate([c, -sn], axis=2), jnp.concatenate([sn, c], axis=2)], axis=1)
    ang_r = _dft_angles(N_RES)
    eye = jnp.eye(K2_BLK, dtype=F32)
    g = jnp.stack([jnp.kron(jnp.cos(ang_r), eye), -jnp.kron(jnp.sin(ang_r), eye)])
    return m1.astype(BF16), g.astype(BF16)


def _channel_dft():
    ang = _dft_angles(FNET_GROUP_W)
    return jnp.concatenate([jnp.cos(ang), jnp.sin(ang)], axis=1).astype(BF16)


def kernel(x, c, ctx, c_ctx, w_mod, b_mod, w_ffn_up, w_ffn_down, ln_g, ln_b, w_in, attn_sink,
           sgu_w, sgu_b, sgu_ln_g, sgu_ln_b, w_gate, w_branch, w_out):
    depth = w_mod.shape[0]
    bsz, s, _ = x.shape
    n_ctx = ctx.shape[1]

    w_gu = _pack_up_call(w_ffn_up)
    w_d = _pack_down_call(w_ffn_down)
    w_in_b = w_in.astype(BF16)
    w_gate_b = w_gate.astype(BF16)
    w_branch_b = w_branch.astype(BF16)
    w_out_b = w_out.astype(BF16)
    sgu_w_b = sgu_w.astype(BF16)
    sgu_b_col = sgu_b[..., None]
    ln_g4 = ln_g[:, :, None, :]
    ln_b4 = ln_b[:, :, None, :]
    sgu_g3 = sgu_ln_g[:, None, :]
    sgu_b3 = sgu_ln_b[:, None, :]

    tables = _rope_tables(s)
    dft_c = _channel_dft()
    m1_lat, g_lat = _factored_dft(s)
    cn_ctx, nsn_ctx = _position_dft(n_ctx)

    rows = -(-(bsz + 1) // 8) * 8
    cc = jnp.zeros((rows, D_MODEL), F32).at[:bsz].set(c).at[bsz].set(c_ctx)
    mods = _mods_call(cc, w_mod, b_mod).reshape(depth, rows, N_MOD, D_MODEL)

    def ffn(xx, i, ctx_row, half, j, tm):
        return _ffn_call(xx, mods, w_gu, w_d, ln_g4, ln_b4, layer=i, ctx_row=ctx_row, half=half, j=j,
                         tm=tm, sub=512)

    def inproj(xx, i, ctx_row, tb, tm):
        return _inproj_call(xx, mods, w_in_b, tb, sgu_w_b, sgu_b_col, sgu_g3, sgu_b3, dft_c,
                            layer=i, ctx_row=ctx_row, tm=tm, sub=512)

    def merge(xx, i, ctx_row, a, sg, cf, tm):
        return _merge_call(xx, mods, a, sg, cf, w_gate_b, w_branch_b, w_out_b, ln_g4, ln_b4,
                           layer=i, ctx_row=ctx_row, tm=tm, sub=512)

    flat = lambda a: a.reshape(1, bsz * n_ctx, a.shape[-1])
    unflat = lambda a: a.reshape(bsz, n_ctx, a.shape[-1])
    x_lat, x_ctx = x, flat(ctx)
    for i in range(depth):
        last = i == depth - 1
        x_lat = ffn(x_lat, i, None, 0, 0, 1024)
        x_ctx = ffn(x_ctx, i, bsz, 0, 0, 1024)

        q_c, k_c, v_c, s_c, ab_c = inproj(x_ctx, i, bsz, None, 1024)
        q_c, k_c, v_c, ab_c = unflat(q_c), unflat(k_c), unflat(v_c), unflat(ab_c)
        q_l, k_l, v_l, s_l, ab_l = inproj(x_lat, i, None, tables, 1024)

        a_l = _attn_call(attn_sink, q_l, k_l, v_l, k_c, v_c, layer=i, band=True, tq=512)
        f_l = _fourier_fact_call(m1_lat, g_lat, ab_l)
        x_lat = merge(x_lat, i, None, a_l, s_l, f_l, 1024)
        x_lat = ffn(x_lat, i, None, 1, 2, 1024)
        if not last:
            a_c = _attn_call(attn_sink, q_c, None, None, k_c, v_c, layer=i, band=False, tq=256)
            f_c = _fourier_call(cn_ctx, nsn_ctx, ab_c, tk=256)
            x_ctx = merge(x_ctx, i, bsz, flat(a_c), s_c, flat(f_c), 1024)
            x_ctx = ffn(x_ctx, i, bsz, 1, 2, 1024)
    return x_lat
```

```python
import functools
import math

import jax
import jax.numpy as jnp
from jax import lax
from jax.experimental import pallas as pl
from jax.experimental.pallas import tpu as pltpu

F32 = jnp.float32
BF16 = jnp.bfloat16

D_MODEL = 1024
GRID_W = 64
HEAD_DIM = 64
N_Q_HEADS = 8
N_KV_HEADS = 2
GQA_GROUP = N_Q_HEADS // N_KV_HEADS
ATT_BLOCK = 128
ROPE_THETA = 10000.0
Q_W = N_Q_HEADS * HEAD_DIM
KV_W = N_KV_HEADS * HEAD_DIM
CHUNK = 128
SGU_GROUPS = 4
SGU_WIDTH = 512
FNET_GROUPS = 4
FNET_WIDTH = 512
FNET_GROUP_W = FNET_WIDTH // FNET_GROUPS
N_BRANCH = 3
D_FF = 2752
N_MOD = 9
MODEL_DEPTH = 4
ALPHA = (2 * MODEL_DEPTH) ** 0.25
LN_EPS = 1e-5
NEG_INF = -1e30

LANES = 128
MXU_W = 256
D_FF_PAD = -(-D_FF // MXU_W) * MXU_W
UP_OFF = D_FF % LANES
UP0 = D_FF - UP_OFF
assert UP0 + D_FF_PAD == 2 * D_FF
KV_DUP_W = 2 * KV_W
IN_Q, IN_K, IN_U, IN_Z, IN_F, IN_END = 0, 512, 768, 1280, 1792, 2304
VMEM_LIMIT = 56 * 1024 * 1024
N_RES = 32
K2_BLK = 16
STAGE_PITCH = 40


def _params(n_axes):
    return pltpu.CompilerParams(dimension_semantics=("parallel",) * n_axes,
                                vmem_limit_bytes=VMEM_LIMIT)


def _const_spec(shape, lead=()):
    nl = len(lead)
    block = (None,) * nl + tuple(shape[nl:])
    index = tuple(lead) + (0,) * (len(shape) - nl)
    return pl.BlockSpec(block, lambda *_: index, pipeline_mode=pl.Buffered(1))


def _mods_spec(layer, ctx_row):
    if ctx_row is None:
        return pl.BlockSpec((None, None, N_MOD, D_MODEL), lambda bi, t: (layer, bi, 0, 0))
    return pl.BlockSpec((None, None, N_MOD, D_MODEL), lambda bi, t: (layer, ctx_row, 0, 0))


def _row_tile(s, pref):
    t = min(s, pref)
    while s % t:
        t -= LANES
    return t


def _dot(a, b):
    return jnp.dot(a, b, preferred_element_type=F32)


def _layer_norm(y, g, b):
    mu = jnp.mean(y, axis=-1, keepdims=True)
    d = y - mu
    var = jnp.mean(d * d, axis=-1, keepdims=True)
    return d * lax.rsqrt(var + LN_EPS) * g + b


def _mods_body(cc_ref, w_ref, b_ref, o_ref):
    s = cc_ref[...]
    s = (s * jax.nn.sigmoid(s)).astype(BF16)
    o_ref[...] = _dot(s, w_ref[...].astype(BF16)) + b_ref[...]


def _mods_call(cc, w_mod, b_mod):
    depth, _, n = w_mod.shape
    rows = cc.shape[0]
    tn = 1024
    return pl.pallas_call(
        _mods_body,
        grid=(depth, n // tn),
        in_specs=[
            pl.BlockSpec((rows, D_MODEL), lambda l, j: (0, 0)),
            pl.BlockSpec((None, D_MODEL, tn), lambda l, j: (l, 0, j)),
            pl.BlockSpec((None, 1, tn), lambda l, j: (l, 0, j)),
        ],
        out_specs=pl.BlockSpec((None, rows, tn), lambda l, j: (l, 0, j)),
        out_shape=jax.ShapeDtypeStruct((depth, rows, n), F32),
        compiler_params=_params(2),
        name="mods",
    )(cc, w_mod, b_mod.reshape(depth, 1, n))


def _pack_up_body(w_ref, o_ref):
    rows = w_ref.shape[0]
    lane = lax.broadcasted_iota(jnp.int32, (rows, D_FF_PAD), 1)
    real = lane < D_FF
    o_ref[:, :D_FF_PAD] = jnp.where(real, w_ref[:, :D_FF_PAD], 0.0).astype(BF16)
    up = pltpu.roll(w_ref[:, UP0:], D_FF_PAD - UP_OFF, axis=1)
    o_ref[:, D_FF_PAD:] = jnp.where(real, up, 0.0).astype(BF16)


def _pack_up_call(w_up):
    depth, two, d, n = w_up.shape
    tr = 256
    return pl.pallas_call(
        _pack_up_body,
        grid=(depth, two, d // tr),
        in_specs=[pl.BlockSpec((None, None, tr, n), lambda l, h, r: (l, h, r, 0))],
        out_specs=pl.BlockSpec((None, None, tr, 2 * D_FF_PAD), lambda l, h, r: (l, h, r, 0)),
        out_shape=jax.ShapeDtypeStruct((depth, two, d, 2 * D_FF_PAD), BF16),
        compiler_params=_params(3),
        name="pack_up",
    )(w_up)


def _pack_down_body(w_ref, o_ref):
    o_ref[:D_FF, :] = w_ref[...].astype(BF16)
    o_ref[D_FF:, :] = jnp.zeros((D_FF_PAD - D_FF, w_ref.shape[1]), BF16)


def _pack_down_call(w_down):
    depth, two, n, d = w_down.shape
    return pl.pallas_call(
        _pack_down_body,
        grid=(depth, two),
        in_specs=[pl.BlockSpec((None, None, n, d), lambda l, h: (l, h, 0, 0))],
        out_specs=pl.BlockSpec((None, None, D_FF_PAD, d), lambda l, h: (l, h, 0, 0)),
        out_shape=jax.ShapeDtypeStruct((depth, two, D_FF_PAD, d), BF16),
        compiler_params=_params(2),
        name="pack_down",
    )(w_down)


def _ffn_body(x_ref, m_ref, wgu_ref, wd_ref, g_ref, b_ref, o_ref, *, j, sub):
    shift = m_ref[3 * j:3 * j + 1, :]
    scale = m_ref[3 * j + 1:3 * j + 2, :]
    gate = m_ref[3 * j + 2:3 * j + 3, :]
    for r in range(x_ref.shape[0] // sub):
        rows = slice(r * sub, (r + 1) * sub)
        x = x_ref[rows, :]
        xm = (x * (1 + scale) + shift).astype(BF16)
        acc = None
        for c in range(D_FF_PAD // MXU_W):
            sl = slice(c * MXU_W, (c + 1) * MXU_W)
            su = slice(D_FF_PAD + c * MXU_W, D_FF_PAD + (c + 1) * MXU_W)
            gg = _dot(xm, wgu_ref[:, sl])
            uu = _dot(xm, wgu_ref[:, su])
            h = (gg * jax.nn.sigmoid(gg) * uu).astype(BF16)
            d = _dot(h, wd_ref[sl, :])
            acc = d if acc is None else acc + d
        y = ALPHA * x + (0.5 * gate) * acc
        o_ref[rows, :] = _layer_norm(y, g_ref[...], b_ref[...])


def _ffn_call(x, mods, wgu, wd, g, b, *, layer, ctx_row, half, j, tm, sub):
    bsz, s, _ = x.shape
    tm = _row_tile(s, tm)
    sub = min(sub, tm)
    return pl.pallas_call(
        functools.partial(_ffn_body, j=j, sub=sub),
        grid=(bsz, s // tm),
        in_specs=[
            pl.BlockSpec((None, tm, D_MODEL), lambda bi, t: (bi, t, 0)),
            _mods_spec(layer, ctx_row),
            _const_spec(wgu.shape, (layer, half)), _const_spec(wd.shape, (layer, half)),
            _const_spec(g.shape, (layer, j)), _const_spec(b.shape, (layer, j)),
        ],
        out_specs=pl.BlockSpec((None, tm, D_MODEL), lambda bi, t: (bi, t, 0)),
        out_shape=jax.ShapeDtypeStruct(x.shape, F32),
        compiler_params=_params(2),
        name="ffn",
    )(x, mods, wgu, wd, g, b)


def _gelu(x):
    return jax.nn.gelu(x)


def _inproj_body(*refs, rope, sub):
    if rope:
        (x_ref, m_ref, w_ref, cos_ref, sin_ref, sw_ref, sb_ref, sg_ref, sbn_ref, dft_ref,
         q_ref, k_ref, v_ref, s_ref, ab_ref, stage_ref) = refs
    else:
        (x_ref, m_ref, w_ref, sw_ref, sb_ref, sg_ref, sbn_ref, dft_ref,
         q_ref, k_ref, v_ref, s_ref, ab_ref) = refs
    lane = lax.broadcasted_iota(jnp.int32, (sub, LANES), 1)
    low_half = lane < HEAD_DIM
    first = (lane % 32) < 16

    for r in range(x_ref.shape[0] // sub):
        rows = slice(r * sub, (r + 1) * sub)
        x = x_ref[rows, :]
        xm = (x * (1 + m_ref[4:5, :]) + m_ref[3:4, :]).astype(BF16)

        if rope:
            cos = cos_ref[rows, :]
            sin = sin_ref[rows, :]

            def rot(t, cos=cos, sin=sin):
                partner = jnp.where(first, pltpu.roll(t, LANES - 16, axis=1), pltpu.roll(t, 16, axis=1))
                return t * cos + partner * sin
        else:
            def rot(t):
                return t

        f = _dot(xm, w_ref[:, IN_F:IN_END]).astype(BF16)
        n_cb = 2 * FNET_WIDTH // LANES
        ab = [None] * n_cb
        for gi in range(FNET_GROUPS):
            cs_ = _dot(f[:, gi * FNET_GROUP_W:(gi + 1) * FNET_GROUP_W], dft_ref[...])
            ab[gi] = cs_[:, :FNET_GROUP_W]
            ab[FNET_GROUPS + gi] = cs_[:, FNET_GROUP_W:]
        if rope:
            n2 = sub // N_RES
            base = r * n2 * STAGE_PITCH
            for g in range(n2):
                for cb in range(n_cb):
                    stage_ref[cb, base + g * STAGE_PITCH:base + g * STAGE_PITCH + N_RES, :] = (
                        ab[cb][g * N_RES:(g + 1) * N_RES, :])
            for j in range(N_RES):
                cols = slice(j * 2 * FNET_WIDTH, (j + 1) * 2 * FNET_WIDTH)
                piece = [stage_ref[cb, pl.ds(base + j, n2, stride=STAGE_PITCH), :] for cb in range(n_cb)]
                ab_ref[r * n2:(r + 1) * n2, cols] = jnp.concatenate(piece, axis=1).astype(BF16)
        else:
            for cb in range(n_cb):
                ab_ref[rows, cb * LANES:(cb + 1) * LANES] = ab[cb].astype(BF16)

        u = _gelu(_dot(xm, w_ref[:, IN_U:IN_Z]))
        z = _gelu(_dot(xm, w_ref[:, IN_Z:IN_F]))
        zn = _layer_norm(z, sg_ref[...], sbn_ref[...]).astype(BF16)
        for ci in range(sub // CHUNK):
            rs = slice(ci * CHUNK, (ci + 1) * CHUNK)
            ro = slice(r * sub + ci * CHUNK, r * sub + (ci + 1) * CHUNK)
            for gi in range(SGU_GROUPS):
                cs = slice(gi * LANES, (gi + 1) * LANES)
                mixed = _dot(sw_ref[gi], zn[rs, cs]) + sb_ref[gi]
                s_ref[ro, cs] = (u[rs, cs] * mixed).astype(BF16)

        q = _dot(xm, w_ref[:, IN_Q:IN_K])
        for c in range(Q_W // LANES):
            sl = slice(c * LANES, (c + 1) * LANES)
            q_ref[rows, sl] = (rot(q[:, sl]) * (HEAD_DIM ** -0.5)).astype(BF16)
        kv = _dot(xm, w_ref[:, IN_K:IN_U])
        for t, o_ref in ((rot(kv[:, :KV_W]), k_ref), (kv[:, KV_W:], v_ref)):
            swapped = pltpu.roll(t, HEAD_DIM, axis=1)
            o_ref[rows, :LANES] = jnp.where(low_half, t, swapped).astype(BF16)
            o_ref[rows, LANES:] = jnp.where(low_half, swapped, t).astype(BF16)


def _inproj_call(x, mods, w_in, tables, sw, sb, sg, sbn, dft_c, *, layer, ctx_row, tm, sub):
    bsz, s, _ = x.shape
    tm = _row_tile(s, tm)
    sub = min(sub, tm)
    rope = tables is not None
    row = lambda w: pl.BlockSpec((None, tm, w), lambda bi, t: (bi, t, 0))
    in_specs = [row(D_MODEL), _mods_spec(layer, ctx_row), _const_spec(w_in.shape, (layer,))]
    args = [x, mods, w_in]
    if rope:
        in_specs += [pl.BlockSpec((tm, LANES), lambda bi, t: (t, 0))] * 2
        args += list(tables)
    in_specs += [_const_spec(a.shape, (layer,)) for a in (sw, sb, sg, sbn)] + [_const_spec(dft_c.shape)]
    args += [sw, sb, sg, sbn, dft_c]
    widths = (Q_W, KV_DUP_W, KV_DUP_W, SGU_WIDTH, 2 * FNET_WIDTH)
    out_specs = [row(w) for w in widths]
    out_shape = [jax.ShapeDtypeStruct((bsz, s, w), BF16) for w in widths]
    scratch = []
    if rope:
        ab_w = N_RES * 2 * FNET_WIDTH
        out_specs[-1] = pl.BlockSpec((None, tm // N_RES, ab_w), lambda bi, t: (bi, t, 0))
        out_shape[-1] = jax.ShapeDtypeStruct((bsz, s // N_RES, ab_w), BF16)
        scratch = [pltpu.VMEM((2 * FNET_WIDTH // LANES, tm // N_RES * STAGE_PITCH, LANES), F32)]
    return pl.pallas_call(
        functools.partial(_inproj_body, rope=rope, sub=sub),
        grid=(bsz, s // tm),
        in_specs=in_specs,
        out_specs=out_specs,
        out_shape=out_shape,
        scratch_shapes=scratch,
        compiler_params=_params(2),
        name="inproj",
    )(*args)


def _attn_body(*refs, band, nq, nb, layer):
    if band:
        (sink_ref, q_ref, kp_ref, km_ref, kn_ref, vp_ref, vm_ref, vn_ref, kx_ref, vx_ref,
         o_ref, kw_ref, vw_ref) = refs
    else:
        sink_ref, q_ref, kx_ref, vx_ref, o_ref = refs
    blk = ATT_BLOCK
    n_ctx = kx_ref.shape[0]
    t = pl.program_id(1)
    row = lax.broadcasted_iota(jnp.int32, (blk, blk), 0)
    col = lax.broadcasted_iota(jnp.int32, (blk, blk), 1)
    lo = col < HEAD_DIM
    zero = jnp.zeros((blk, blk), BF16)
    nt_dims = (((1,), (1,)), ((), ()))
    if band:
        kw_ref[0:blk] = kp_ref[...]
        kw_ref[blk:(nq + 1) * blk] = km_ref[...]
        kw_ref[(nq + 1) * blk:(nq + 2) * blk] = kn_ref[...]
        vw_ref[0:blk] = vp_ref[...]
        vw_ref[blk:(nq + 1) * blk] = vm_ref[...]
        vw_ref[(nq + 1) * blk:(nq + 2) * blk] = vn_ref[...]

    for qi in range(nq):
        rows = slice(qi * blk, (qi + 1) * blk)
        if band:
            gi = t * nq + qi
            prev_ok = jnp.logical_and(col >= row, gi > 0)
            next_ok = jnp.logical_and(col <= row, gi < nb - 1)
        for kh in range(N_KV_HEADS):
            cs = slice(kh * LANES, (kh + 1) * LANES)
            qs = []
            for pr in range(GQA_GROUP // 2):
                pair = q_ref[rows, (2 * kh + pr) * LANES:(2 * kh + pr + 1) * LANES]
                qs.append(jnp.where(lo, pair, zero))
                qs.append(jnp.where(lo, zero, pair))
            qstack = jnp.concatenate(qs, axis=0)
            sc_ctx = lax.dot_general(qstack, kx_ref[:, cs], nt_dims, preferred_element_type=F32)
            if band:
                win = slice(qi * blk, (qi + 3) * blk)
                sc_band = lax.dot_general(qstack, kw_ref[win, cs], nt_dims, preferred_element_type=F32)
            ps, dens = [], []
            for g in range(GQA_GROUP):
                hr = slice(g * blk, (g + 1) * blk)
                pieces = [sc_ctx[hr, c * blk:(c + 1) * blk] for c in range(n_ctx // blk)]
                if band:
                    pieces = [jnp.where(prev_ok, sc_band[hr, 0:blk], NEG_INF),
                              sc_band[hr, blk:2 * blk],
                              jnp.where(next_ok, sc_band[hr, 2 * blk:3 * blk], NEG_INF)] + pieces
                sink = sink_ref[layer, kh * GQA_GROUP + g]
                mx = pieces[0]
                for pc in pieces[1:]:
                    mx = jnp.maximum(mx, pc)
                m = jnp.maximum(jnp.max(mx, axis=1, keepdims=True), sink)
                es = [jnp.exp(pc - m) for pc in pieces]
                tot = es[0]
                for e in es[1:]:
                    tot = tot + e
                dens.append(jnp.sum(tot, axis=1, keepdims=True) + jnp.exp(sink - m))
                ps.append(jnp.concatenate([e.astype(BF16) for e in es], axis=1))
            pall = jnp.concatenate(ps, axis=0)
            if band:
                o = _dot(pall[:, :3 * blk], vw_ref[win, cs]) + _dot(pall[:, 3 * blk:], vx_ref[:, cs])
            else:
                o = _dot(pall, vx_ref[:, cs])
            outs = [o[g * blk:(g + 1) * blk] / dens[g] for g in range(GQA_GROUP)]
            for pr in range(GQA_GROUP // 2):
                pair = jnp.where(lo, outs[2 * pr], outs[2 * pr + 1])
                o_ref[rows, (2 * kh + pr) * LANES:(2 * kh + pr + 1) * LANES] = pair.astype(BF16)


def _attn_call(sink, q, k, v, kx, vx, *, layer, band, tq):
    bsz, s, _ = q.shape
    n_ctx = kx.shape[1]
    nb = s // ATT_BLOCK
    tq = _row_tile(s, tq)
    nq = tq // ATT_BLOCK
    qspec = pl.BlockSpec((None, tq, Q_W), lambda bi, i: (bi, i, 0))
    xspec = pl.BlockSpec((None, n_ctx, KV_DUP_W), lambda bi, i: (bi, 0, 0))
    sspec = pl.BlockSpec(memory_space=pltpu.SMEM)
    scratch = []
    if band:
        edge = lambda f: pl.BlockSpec((None, ATT_BLOCK, KV_DUP_W), lambda bi, i: (bi, f(i), 0))
        prev = edge(lambda i: jnp.maximum(i * nq - 1, 0))
        nxt = edge(lambda i: jnp.minimum((i + 1) * nq, nb - 1))
        main = pl.BlockSpec((None, tq, KV_DUP_W), lambda bi, i: (bi, i, 0))
        in_specs = [sspec, qspec] + [prev, main, nxt] * 2 + [xspec, xspec]
        args = (sink, q, k, k, k, v, v, v, kx, vx)
        scratch = [pltpu.VMEM(((nq + 2) * ATT_BLOCK, KV_DUP_W), BF16)] * 2
    else:
        in_specs = [sspec, qspec, xspec, xspec]
        args = (sink, q, kx, vx)
    return pl.pallas_call(
        functools.partial(_attn_body, band=band, nq=nq, nb=nb, layer=layer),
        grid=(bsz, s // tq),
        in_specs=in_specs,
        out_specs=qspec,
        out_shape=jax.ShapeDtypeStruct((bsz, s, Q_W), BF16),
        scratch_shapes=scratch,
        compiler_params=_params(2),
        name="attn_band" if band else "attn_ctx",
    )(*args)


def _fourier_body(cn_ref, nsn_ref, ab_ref, o_ref, *, scale):
    y = _dot(cn_ref[...], ab_ref[:, :FNET_WIDTH]) + _dot(nsn_ref[...], ab_ref[:, FNET_WIDTH:])
    o_ref[...] = (y * scale).astype(BF16)


def _fourier_call(cn, nsn, ab, *, tk):
    bsz, s, _ = ab.shape
    tk = _row_tile(s, tk)
    scale = 1.0 / math.sqrt(s * FNET_GROUP_W)
    return pl.pallas_call(
        functools.partial(_fourier_body, scale=scale),
        grid=(bsz, s // tk),
        in_specs=[
            pl.BlockSpec((tk, s), lambda bi, t: (t, 0)),
            pl.BlockSpec((tk, s), lambda bi, t: (t, 0)),
            pl.BlockSpec((None, s, 2 * FNET_WIDTH), lambda bi, t: (bi, 0, 0)),
        ],
        out_specs=pl.BlockSpec((None, tk, FNET_WIDTH), lambda bi, t: (bi, t, 0)),
        out_shape=jax.ShapeDtypeStruct((bsz, s, FNET_WIDTH), BF16),
        compiler_params=_params(2),
        name="fourier",
    )(cn, nsn, ab)


def _fourier_fact_body(ab_ref, m1_ref, g_ref, o_ref, t_ref, *, n_low, scale):
    w = FNET_WIDTH
    for j in range(N_RES):
        blk = ab_ref[:, j * 2 * w:(j + 1) * 2 * w]
        d = jnp.concatenate([blk[:, :w], blk[:, w:]], axis=0)
        r = _dot(m1_ref[j], d)
        rows = slice(j * n_low, (j + 1) * n_low)
        t_ref[rows, :w] = r[:n_low].astype(BF16)
        t_ref[rows, w:] = r[n_low:].astype(BF16)
    for kb in range(n_low // K2_BLK):
        x = jnp.concatenate(
            [t_ref[j * n_low + kb * K2_BLK:j * n_low + (kb + 1) * K2_BLK, :] for j in range(N_RES)], axis=0)
        y = _dot(g_ref[0], x[:, :w]) + _dot(g_ref[1], x[:, w:])
        for k1 in range(N_RES):
            o_ref[k1 * n_low + kb * K2_BLK:k1 * n_low + (kb + 1) * K2_BLK, :] = (
                y[k1 * K2_BLK:(k1 + 1) * K2_BLK] * scale).astype(BF16)


def _fourier_fact_call(m1, g, ab_view):
    bsz, n_low, _ = ab_view.shape
    s = n_low * N_RES
    scale = 1.0 / math.sqrt(s * FNET_GROUP_W)
    return pl.pallas_call(
        functools.partial(_fourier_fact_body, n_low=n_low, scale=scale),
        grid=(bsz,),
        in_specs=[
            pl.BlockSpec((None, n_low, N_RES * 2 * FNET_WIDTH), lambda bi: (bi, 0, 0)),
            _const_spec(m1.shape), _const_spec(g.shape),
        ],
        out_specs=pl.BlockSpec((None, s, FNET_WIDTH), lambda bi: (bi, 0, 0)),
        out_shape=jax.ShapeDtypeStruct((bsz, s, FNET_WIDTH), BF16),
        scratch_shapes=[pltpu.VMEM((s, 2 * FNET_WIDTH), BF16)],
        compiler_params=_params(1),
        name="fourier_fact",
    )(ab_view, m1, g)


def _merge_body(x_ref, m_ref, a_ref, s_ref, c_ref, wg_ref, wb_ref, wo_ref, g_ref, b_ref, o_ref, mg_ref, *, sub):
    for st in range(x_ref.shape[0] // sub):
        rows = slice(st * sub, (st + 1) * sub)
        x = x_ref[rows, :]
        h = (x * (1 + m_ref[4:5, :]) + m_ref[3:4, :]).astype(BF16)
        branches = (a_ref[rows, :], s_ref[rows, :], c_ref[rows, :])
        for n in range(D_MODEL // MXU_W):
            sl = slice(n * MXU_W, (n + 1) * MXU_W)
            acc = None
            for r in range(N_BRANCH):
                t = jax.nn.sigmoid(_dot(h, wg_ref[r, :, sl])) * _dot(branches[r], wb_ref[r, :, sl])
                acc = t if acc is None else acc + t
            mg_ref[rows, sl] = acc.astype(BF16)
        y = _dot(mg_ref[rows, :], wo_ref[...])
        o_ref[rows, :] = _layer_norm(ALPHA * x + m_ref[5:6, :] * y, g_ref[...], b_ref[...])


def _merge_call(x, mods, a, sgu, c, wg, wb, wo, g, b, *, layer, ctx_row, tm, sub):
    bsz, s, _ = x.shape
    tm = _row_tile(s, tm)
    sub = min(sub, tm)
    row = lambda w: pl.BlockSpec((None, tm, w), lambda bi, t: (bi, t, 0))
    return pl.pallas_call(
        functools.partial(_merge_body, sub=sub),
        grid=(bsz, s // tm),
        in_specs=[
            row(D_MODEL), _mods_spec(layer, ctx_row),
            row(Q_W), row(SGU_WIDTH), row(FNET_WIDTH),
            _const_spec(wg.shape, (layer,)), _const_spec(wb.shape, (layer,)), _const_spec(wo.shape, (layer,)),
            _const_spec(g.shape, (layer, 1)), _const_spec(b.shape, (layer, 1)),
        ],
        out_specs=row(D_MODEL),
        out_shape=jax.ShapeDtypeStruct(x.shape, F32),
        scratch_shapes=[pltpu.VMEM((tm, D_MODEL), BF16)],
        compiler_params=_params(2),
        name="merge",
    )(x, mods, a, sgu, c, wg, wb, wo, g, b)


def _rope_tables(s):
    pos = jnp.arange(s)
    row = (pos // GRID_W).astype(F32)
    col = (pos % GRID_W).astype(F32)
    axis_dim = HEAD_DIM // 2
    inv_freq = ROPE_THETA ** (-jnp.arange(0, axis_dim, 2, dtype=F32) / axis_dim)
    ang_r = row[:, None] * inv_freq[None, :]
    ang_c = col[:, None] * inv_freq[None, :]
    cos = jnp.concatenate([jnp.cos(ang_r)] * 2 + [jnp.cos(ang_c)] * 2, axis=1)
    sin = jnp.concatenate([-jnp.sin(ang_r), jnp.sin(ang_r), -jnp.sin(ang_c), jnp.sin(ang_c)], axis=1)
    return jnp.tile(cos, (1, 2)), jnp.tile(sin, (1, 2))


def _dft_angles(n):
    k = jnp.arange(n, dtype=jnp.int32)
    r = (k[:, None] * k[None, :]) % n
    return r.astype(F32) * (2.0 * math.pi / n)


def _position_dft(n):
    ang = _dft_angles(n)
    return jnp.cos(ang).astype(BF16), (-jnp.sin(ang)).astype(BF16)


def _factored_dft(s):
    n_low = s // N_RES
    idx = jnp.arange(n_low, dtype=jnp.int32)
    n = jnp.arange(N_RES, dtype=jnp.int32)[:, None, None] + N_RES * idx[None, None, :]
    ang = ((idx[None, :, None] * n) % s).astype(F32) * (2.0 * math.pi / s)
    c, sn = jnp.cos(ang), jnp.sin(ang)
    m1 = jnp.concatenate([jnp.concatenate([c, -sn], axis=2), jnp.concatenate([sn, c], axis=2)], axis=1)
    ang_r = _dft_angles(N_RES)
    eye = jnp.eye(K2_BLK, dtype=F32)
    g = jnp.stack([jnp.kron(jnp.cos(ang_r), eye), -jnp.kron(jnp.sin(ang_r), eye)])
    return m1.astype(BF16), g.astype(BF16)


def _channel_dft():
    ang = _dft_angles(FNET_GROUP_W)
    return jnp.concatenate([jnp.cos(ang), jnp.sin(ang)], axis=1).astype(BF16)


def kernel(x, c, ctx, c_ctx, w_mod, b_mod, w_ffn_up, w_ffn_down, ln_g, ln_b, w_in, attn_sink,
           sgu_w, sgu_b, sgu_ln_g, sgu_ln_b, w_gate, w_branch, w_out):
    depth = w_mod.shape[0]
    bsz, s, _ = x.shape
    n_ctx = ctx.shape[1]

    w_gu = _pack_up_call(w_ffn_up)
    w_d = _pack_down_call(w_ffn_down)
    w_in_b = w_in.astype(BF16)
    w_gate_b = w_gate.astype(BF16)
    w_branch_b = w_branch.astype(BF16)
    w_out_b = w_out.astype(BF16)
    sgu_w_b = sgu_w.astype(BF16)
    sgu_b_col = sgu_b[..., None]
    ln_g4 = ln_g[:, :, None, :]
    ln_b4 = ln_b[:, :, None, :]
    sgu_g3 = sgu_ln_g[:, None, :]
    sgu_b3 = sgu_ln_b[:, None, :]

    tables = _rope_tables(s)
    dft_c = _channel_dft()
    m1_lat, g_lat = _factored_dft(s)
    cn_ctx, nsn_ctx = _position_dft(n_ctx)

    rows = -(-(bsz + 1) // 8) * 8
    cc = jnp.zeros((rows, D_MODEL), F32).at[:bsz].set(c).at[bsz].set(c_ctx)
    mods = _mods_call(cc, w_mod, b_mod).reshape(depth, rows, N_MOD, D_MODEL)

    def ffn(xx, i, ctx_row, half, j, tm):
        return _ffn_call(xx, mods, w_gu, w_d, ln_g4, ln_b4, layer=i, ctx_row=ctx_row, half=half, j=j,
                         tm=tm, sub=512)

    def inproj(xx, i, ctx_row, tb, tm):
        return _inproj_call(xx, mods, w_in_b, tb, sgu_w_b, sgu_b_col, sgu_g3, sgu_b3, dft_c,
                            layer=i, ctx_row=ctx_row, tm=tm, sub=512)

    def merge(xx, i, ctx_row, a, sg, cf, tm):
        return _merge_call(xx, mods, a, sg, cf, w_gate_b, w_branch_b, w_out_b, ln_g4, ln_b4,
                           layer=i, ctx_row=ctx_row, tm=tm, sub=512)

    flat = lambda a: a.reshape(1, bsz * n_ctx, a.shape[-1])
    unflat = lambda a: a.reshape(bsz, n_ctx, a.shape[-1])
    x_lat, x_ctx = x, flat(ctx)
    for i in range(depth):
        last = i == depth - 1
        x_lat = ffn(x_lat, i, None, 0, 0, 1024)
        x_ctx = ffn(x_ctx, i, bsz, 0, 0, 1024)

        q_c, k_c, v_c, s_c, ab_c = inproj(x_ctx, i, bsz, None, 1024)
        q_c, k_c, v_c, ab_c = unflat(q_c), unflat(k_c), unflat(v_c), unflat(ab_c)
        q_l, k_l, v_l, s_l, ab_l = inproj(x_lat, i, None, tables, 1024)

        a_l = _attn_call(attn_sink, q_l, k_l, v_l, k_c, v_c, layer=i, band=True, tq=1024)
        f_l = _fourier_fact_call(m1_lat, g_lat, ab_l)
        x_lat = merge(x_lat, i, None, a_l, s_l, f_l, 1024)
        x_lat = ffn(x_lat, i, None, 1, 2, 1024)
        if not last:
            a_c = _attn_call(attn_sink, q_c, None, None, k_c, v_c, layer=i, band=False, tq=256)
            f_c = _fourier_call(cn_ctx, nsn_ctx, ab_c, tk=256)
            x_ctx = merge(x_ctx, i, bsz, flat(a_c), s_c, flat(f_c), 1024)
            x_ctx = ffn(x_ctx, i, bsz, 1, 2, 1024)
    return x_lat
```

```python
import functools
import math

import jax
import jax.numpy as jnp
from jax import lax
from jax.experimental import pallas as pl
from jax.experimental.pallas import tpu as pltpu

F32 = jnp.float32
BF16 = jnp.bfloat16

D_MODEL = 1024
GRID_W = 64
HEAD_DIM = 64
N_Q_HEADS = 8
N_KV_HEADS = 2
GQA_GROUP = N_Q_HEADS // N_KV_HEADS
ATT_BLOCK = 128
ROPE_THETA = 10000.0
Q_W = N_Q_HEADS * HEAD_DIM
KV_W = N_KV_HEADS * HEAD_DIM
CHUNK = 128
SGU_GROUPS = 4
SGU_WIDTH = 512
FNET_GROUPS = 4
FNET_WIDTH = 512
FNET_GROUP_W = FNET_WIDTH // FNET_GROUPS
N_BRANCH = 3
D_FF = 2752
N_MOD = 9
MODEL_DEPTH = 4
ALPHA = (2 * MODEL_DEPTH) ** 0.25
LN_EPS = 1e-5
NEG_INF = -1e30

LANES = 128
MXU_W = 256
D_FF_PAD = -(-D_FF // MXU_W) * MXU_W
UP_OFF = D_FF % LANES
UP0 = D_FF - UP_OFF
assert UP0 + D_FF_PAD == 2 * D_FF
KV_DUP_W = 2 * KV_W
IN_Q, IN_K, IN_U, IN_Z, IN_F, IN_END = 0, 512, 768, 1280, 1792, 2304
VMEM_LIMIT = 56 * 1024 * 1024
N_RES = 32
K2_BLK = 16
STAGE_PITCH = 40
DEN_FLOOR = 2.0 ** -100
BOUND_SLACK = 1.01


def _params(n_axes):
    return pltpu.CompilerParams(dimension_semantics=("parallel",) * n_axes,
                                vmem_limit_bytes=VMEM_LIMIT)


def _const_spec(shape, lead=()):
    nl = len(lead)
    block = (None,) * nl + tuple(shape[nl:])
    index = tuple(lead) + (0,) * (len(shape) - nl)
    return pl.BlockSpec(block, lambda *_: index, pipeline_mode=pl.Buffered(1))


def _mods_spec(layer, ctx_row):
    if ctx_row is None:
        return pl.BlockSpec((None, None, N_MOD, D_MODEL), lambda bi, t: (layer, bi, 0, 0))
    return pl.BlockSpec((None, None, N_MOD, D_MODEL), lambda bi, t: (layer, ctx_row, 0, 0))


def _row_tile(s, pref):
    t = min(s, pref)
    while s % t:
        t -= LANES
    return t


def _dot(a, b):
    return jnp.dot(a, b, preferred_element_type=F32)


def _layer_norm(y, g, b):
    mu = jnp.mean(y, axis=-1, keepdims=True)
    d = y - mu
    var = jnp.mean(d * d, axis=-1, keepdims=True)
    return d * lax.rsqrt(var + LN_EPS) * g + b


def _mods_body(cc_ref, w_ref, b_ref, o_ref):
    s = cc_ref[...]
    s = (s * jax.nn.sigmoid(s)).astype(BF16)
    o_ref[...] = _dot(s, w_ref[...].astype(BF16)) + b_ref[...]


def _mods_call(cc, w_mod, b_mod):
    depth, _, n = w_mod.shape
    rows = cc.shape[0]
    tn = 1024
    return pl.pallas_call(
        _mods_body,
        grid=(depth, n // tn),
        in_specs=[
            pl.BlockSpec((rows, D_MODEL), lambda l, j: (0, 0)),
            pl.BlockSpec((None, D_MODEL, tn), lambda l, j: (l, 0, j)),
            pl.BlockSpec((None, 1, tn), lambda l, j: (l, 0, j)),
        ],
        out_specs=pl.BlockSpec((None, rows, tn), lambda l, j: (l, 0, j)),
        out_shape=jax.ShapeDtypeStruct((depth, rows, n), F32),
        compiler_params=_params(2),
        name="mods",
    )(cc, w_mod, b_mod.reshape(depth, 1, n))


def _pack_up_body(w_ref, o_ref):
    rows = w_ref.shape[0]
    lane = lax.broadcasted_iota(jnp.int32, (rows, D_FF_PAD), 1)
    real = lane < D_FF
    o_ref[:, :D_FF_PAD] = jnp.where(real, w_ref[:, :D_FF_PAD], 0.0).astype(BF16)
    up = pltpu.roll(w_ref[:, UP0:], D_FF_PAD - UP_OFF, axis=1)
    o_ref[:, D_FF_PAD:] = jnp.where(real, up, 0.0).astype(BF16)


def _pack_up_call(w_up):
    depth, two, d, n = w_up.shape
    tr = 256
    return pl.pallas_call(
        _pack_up_body,
        grid=(depth, two, d // tr),
        in_specs=[pl.BlockSpec((None, None, tr, n), lambda l, h, r: (l, h, r, 0))],
        out_specs=pl.BlockSpec((None, None, tr, 2 * D_FF_PAD), lambda l, h, r: (l, h, r, 0)),
        out_shape=jax.ShapeDtypeStruct((depth, two, d, 2 * D_FF_PAD), BF16),
        compiler_params=_params(3),
        name="pack_up",
    )(w_up)


def _pack_down_body(w_ref, o_ref):
    o_ref[:D_FF, :] = w_ref[...].astype(BF16)
    o_ref[D_FF:, :] = jnp.zeros((D_FF_PAD - D_FF, w_ref.shape[1]), BF16)


def _pack_down_call(w_down):
    depth, two, n, d = w_down.shape
    return pl.pallas_call(
        _pack_down_body,
        grid=(depth, two),
        in_specs=[pl.BlockSpec((None, None, n, d), lambda l, h: (l, h, 0, 0))],
        out_specs=pl.BlockSpec((None, None, D_FF_PAD, d), lambda l, h: (l, h, 0, 0)),
        out_shape=jax.ShapeDtypeStruct((depth, two, D_FF_PAD, d), BF16),
        compiler_params=_params(2),
        name="pack_down",
    )(w_down)


def _ffn_body(x_ref, m_ref, wgu_ref, wd_ref, g_ref, b_ref, o_ref, *, j, sub):
    shift = m_ref[3 * j:3 * j + 1, :]
    scale = m_ref[3 * j + 1:3 * j + 2, :]
    gate = m_ref[3 * j + 2:3 * j + 3, :]
    for r in range(x_ref.shape[0] // sub):
        rows = slice(r * sub, (r + 1) * sub)
        x = x_ref[rows, :]
        xm = (x * (1 + scale) + shift).astype(BF16)
        acc = None
        for c in range(D_FF_PAD // MXU_W):
            sl = slice(c * MXU_W, (c + 1) * MXU_W)
            su = slice(D_FF_PAD + c * MXU_W, D_FF_PAD + (c + 1) * MXU_W)
            gg = _dot(xm, wgu_ref[:, sl])
            uu = _dot(xm, wgu_ref[:, su])
            h = (gg * jax.nn.sigmoid(gg) * uu).astype(BF16)
            d = _dot(h, wd_ref[sl, :])
            acc = d if acc is None else acc + d
        y = ALPHA * x + (0.5 * gate) * acc
        o_ref[rows, :] = _layer_norm(y, g_ref[...], b_ref[...])


def _ffn_call(x, mods, wgu, wd, g, b, *, layer, ctx_row, half, j, tm, sub):
    bsz, s, _ = x.shape
    tm = _row_tile(s, tm)
    sub = min(sub, tm)
    return pl.pallas_call(
        functools.partial(_ffn_body, j=j, sub=sub),
        grid=(bsz, s // tm),
        in_specs=[
            pl.BlockSpec((None, tm, D_MODEL), lambda bi, t: (bi, t, 0)),
            _mods_spec(layer, ctx_row),
            _const_spec(wgu.shape, (layer, half)), _const_spec(wd.shape, (layer, half)),
            _const_spec(g.shape, (layer, j)), _const_spec(b.shape, (layer, j)),
        ],
        out_specs=pl.BlockSpec((None, tm, D_MODEL), lambda bi, t: (bi, t, 0)),
        out_shape=jax.ShapeDtypeStruct(x.shape, F32),
        compiler_params=_params(2),
        name="ffn",
    )(x, mods, wgu, wd, g, b)


def _gelu(x):
    return jax.nn.gelu(x)


def _inproj_body(*refs, rope, sub):
    if rope:
        (x_ref, m_ref, w_ref, cos_ref, sin_ref, sw_ref, sb_ref, sg_ref, sbn_ref, dft_ref,
         q_ref, k_ref, v_ref, s_ref, ab_ref, stage_ref) = refs
    else:
        (x_ref, m_ref, w_ref, sw_ref, sb_ref, sg_ref, sbn_ref, dft_ref,
         q_ref, k_ref, v_ref, s_ref, ab_ref) = refs
    lane = lax.broadcasted_iota(jnp.int32, (sub, LANES), 1)
    low_half = lane < HEAD_DIM
    first = (lane % 32) < 16

    for r in range(x_ref.shape[0] // sub):
        rows = slice(r * sub, (r + 1) * sub)
        x = x_ref[rows, :]
        xm = (x * (1 + m_ref[4:5, :]) + m_ref[3:4, :]).astype(BF16)

        if rope:
            cos = cos_ref[rows, :]
            sin = sin_ref[rows, :]

            def rot(t, cos=cos, sin=sin):
                partner = jnp.where(first, pltpu.roll(t, LANES - 16, axis=1), pltpu.roll(t, 16, axis=1))
                return t * cos + partner * sin
        else:
            def rot(t):
                return t

        q = _dot(xm, w_ref[:, IN_Q:IN_K])
        for c in range(Q_W // LANES):
            sl = slice(c * LANES, (c + 1) * LANES)
            q_ref[rows, sl] = (rot(q[:, sl]) * (HEAD_DIM ** -0.5)).astype(BF16)
        kv = _dot(xm, w_ref[:, IN_K:IN_U])
        for t, o_ref in ((rot(kv[:, :KV_W]), k_ref), (kv[:, KV_W:], v_ref)):
            swapped = pltpu.roll(t, HEAD_DIM, axis=1)
            o_ref[rows, :LANES] = jnp.where(low_half, t, swapped).astype(BF16)
            o_ref[rows, LANES:] = jnp.where(low_half, swapped, t).astype(BF16)

        u = _gelu(_dot(xm, w_ref[:, IN_U:IN_Z]))
        z = _gelu(_dot(xm, w_ref[:, IN_Z:IN_F]))
        zn = _layer_norm(z, sg_ref[...], sbn_ref[...]).astype(BF16)
        for ci in range(sub // CHUNK):
            rs = slice(ci * CHUNK, (ci + 1) * CHUNK)
            ro = slice(r * sub + ci * CHUNK, r * sub + (ci + 1) * CHUNK)
            for gi in range(SGU_GROUPS):
                cs = slice(gi * LANES, (gi + 1) * LANES)
                mixed = _dot(sw_ref[gi], zn[rs, cs]) + sb_ref[gi]
                s_ref[ro, cs] = (u[rs, cs] * mixed).astype(BF16)

        f = _dot(xm, w_ref[:, IN_F:IN_END]).astype(BF16)
        n_cb = 2 * FNET_WIDTH // LANES
        ab = [None] * n_cb
        for gi in range(FNET_GROUPS):
            cs_ = _dot(f[:, gi * FNET_GROUP_W:(gi + 1) * FNET_GROUP_W], dft_ref[...])
            ab[gi] = cs_[:, :FNET_GROUP_W]
            ab[FNET_GROUPS + gi] = cs_[:, FNET_GROUP_W:]
        if rope:
            n2 = sub // N_RES
            base = r * n2 * STAGE_PITCH
            for g in range(n2):
                for cb in range(n_cb):
                    stage_ref[cb, base + g * STAGE_PITCH:base + g * STAGE_PITCH + N_RES, :] = (
                        ab[cb][g * N_RES:(g + 1) * N_RES, :])
            for j in range(N_RES):
                cols = slice(j * 2 * FNET_WIDTH, (j + 1) * 2 * FNET_WIDTH)
                piece = [stage_ref[cb, pl.ds(base + j, n2, stride=STAGE_PITCH), :] for cb in range(n_cb)]
                ab_ref[r * n2:(r + 1) * n2, cols] = jnp.concatenate(piece, axis=1).astype(BF16)
        else:
            for cb in range(n_cb):
                ab_ref[rows, cb * LANES:(cb + 1) * LANES] = ab[cb].astype(BF16)


def _inproj_call(x, mods, w_in, tables, sw, sb, sg, sbn, dft_c, *, layer, ctx_row, tm, sub):
    bsz, s, _ = x.shape
    tm = _row_tile(s, tm)
    sub = min(sub, tm)
    rope = tables is not None
    row = lambda w: pl.BlockSpec((None, tm, w), lambda bi, t: (bi, t, 0))
    in_specs = [row(D_MODEL), _mods_spec(layer, ctx_row), _const_spec(w_in.shape, (layer,))]
    args = [x, mods, w_in]
    if rope:
        in_specs += [pl.BlockSpec((tm, LANES), lambda bi, t: (t, 0))] * 2
        args += list(tables)
    in_specs += [_const_spec(a.shape, (layer,)) for a in (sw, sb, sg, sbn)] + [_const_spec(dft_c.shape)]
    args += [sw, sb, sg, sbn, dft_c]
    widths = (Q_W, KV_DUP_W, KV_DUP_W, SGU_WIDTH, 2 * FNET_WIDTH)
    out_specs = [row(w) for w in widths]
    out_shape = [jax.ShapeDtypeStruct((bsz, s, w), BF16) for w in widths]
    scratch = []
    if rope:
        ab_w = N_RES * 2 * FNET_WIDTH
        out_specs[-1] = pl.BlockSpec((None, tm // N_RES, ab_w), lambda bi, t: (bi, t, 0))
        out_shape[-1] = jax.ShapeDtypeStruct((bsz, s // N_RES, ab_w), BF16)
        scratch = [pltpu.VMEM((2 * FNET_WIDTH // LANES, tm // N_RES * STAGE_PITCH, LANES), F32)]
    return pl.pallas_call(
        functools.partial(_inproj_body, rope=rope, sub=sub),
        grid=(bsz, s // tm),
        in_specs=in_specs,
        out_specs=out_specs,
        out_shape=out_shape,
        scratch_shapes=scratch,
        compiler_params=_params(2),
        name="inproj",
    )(*args)


def _attn_body(*refs, band, nq, nb, layer):
    if band:
        (sink_ref, q_ref, kp_ref, km_ref, kn_ref, vp_ref, vm_ref, vn_ref, kx_ref, vx_ref,
         o_ref, kw_ref, vw_ref) = refs
    else:
        sink_ref, q_ref, kx_ref, vx_ref, o_ref = refs
    blk = ATT_BLOCK
    n_ctx = kx_ref.shape[0]
    t = pl.program_id(1)
    row = lax.broadcasted_iota(jnp.int32, (blk, blk), 0)
    col = lax.broadcasted_iota(jnp.int32, (blk, blk), 1)
    lo = col < HEAD_DIM
    zero = jnp.zeros((blk, blk), BF16)
    nt_dims = (((1,), (1,)), ((), ()))
    if band:
        kw_ref[0:blk] = kp_ref[...]
        kw_ref[blk:(nq + 1) * blk] = km_ref[...]
        kw_ref[(nq + 1) * blk:(nq + 2) * blk] = kn_ref[...]
        vw_ref[0:blk] = vp_ref[...]
        vw_ref[blk:(nq + 1) * blk] = vm_ref[...]
        vw_ref[(nq + 1) * blk:(nq + 2) * blk] = vn_ref[...]

    def max_key_norm_sq(kh):
        cs = slice(kh * LANES, (kh + 1) * LANES)
        best = None
        for ref in ((kw_ref, kx_ref) if band else (kx_ref,)):
            kf = ref[:, cs].astype(F32)
            sq = jnp.max(jnp.sum(kf * kf, axis=1, keepdims=True), axis=0, keepdims=True)
            best = sq if best is None else jnp.maximum(best, sq)
        return 0.5 * best

    def run(exact):
        den_min = None
        k2 = None if exact else [max_key_norm_sq(kh) for kh in range(N_KV_HEADS)]
        for qi in range(nq):
            rows = slice(qi * blk, (qi + 1) * blk)
            if band:
                gi = t * nq + qi
                prev_ok = jnp.logical_and(col >= row, gi > 0)
                next_ok = jnp.logical_and(col <= row, gi < nb - 1)
            for kh in range(N_KV_HEADS):
                cs = slice(kh * LANES, (kh + 1) * LANES)
                qs = []
                for pr in range(GQA_GROUP // 2):
                    pair = q_ref[rows, (2 * kh + pr) * LANES:(2 * kh + pr + 1) * LANES]
                    qs.append(jnp.where(lo, pair, zero))
                    qs.append(jnp.where(lo, zero, pair))
                qstack = jnp.concatenate(qs, axis=0)
                if not exact:
                    qf = qstack.astype(F32)
                    q2 = jnp.sum(qf * qf, axis=1, keepdims=True)
                sc_ctx = lax.dot_general(qstack, kx_ref[:, cs], nt_dims, preferred_element_type=F32)
                if band:
                    win = slice(qi * blk, (qi + 3) * blk)
                    sc_band = lax.dot_general(qstack, kw_ref[win, cs], nt_dims, preferred_element_type=F32)
                ps, dens = [], []
                for g in range(GQA_GROUP):
                    hr = slice(g * blk, (g + 1) * blk)
                    pieces = [sc_ctx[hr, c * blk:(c + 1) * blk] for c in range(n_ctx // blk)]
                    if band:
                        pieces = [jnp.where(prev_ok, sc_band[hr, 0:blk], NEG_INF),
                                  sc_band[hr, blk:2 * blk],
                                  jnp.where(next_ok, sc_band[hr, 2 * blk:3 * blk], NEG_INF)] + pieces
                    sink = sink_ref[layer, kh * GQA_GROUP + g]
                    if exact:
                        mx = pieces[0]
                        for pc in pieces[1:]:
                            mx = jnp.maximum(mx, pc)
                        m = jnp.maximum(jnp.max(mx, axis=1, keepdims=True), sink)
                    else:
                        q2max = jnp.max(q2[hr], axis=0, keepdims=True)
                        m = jnp.maximum(jnp.sqrt(q2max * k2[kh]) * BOUND_SLACK, sink)
                    es = [jnp.exp(pc - m) for pc in pieces]
                    tot = es[0]
                    for e in es[1:]:
                        tot = tot + e
                    den = jnp.sum(tot, axis=1, keepdims=True) + jnp.exp(sink - m)
                    dens.append(den)
                    if not exact:
                        den_min = den if den_min is None else jnp.minimum(den_min, den)
                    ps.append(jnp.concatenate([e.astype(BF16) for e in es], axis=1))
                pall = jnp.concatenate(ps, axis=0)
                if band:
                    o = _dot(pall[:, :3 * blk], vw_ref[win, cs]) + _dot(pall[:, 3 * blk:], vx_ref[:, cs])
                else:
                    o = _dot(pall, vx_ref[:, cs])
                outs = [o[g * blk:(g + 1) * blk] / dens[g] for g in range(GQA_GROUP)]
                for pr in range(GQA_GROUP // 2):
                    pair = jnp.where(lo, outs[2 * pr], outs[2 * pr + 1])
                    o_ref[rows, (2 * kh + pr) * LANES:(2 * kh + pr + 1) * LANES] = pair.astype(BF16)
        return den_min

    den_min = run(exact=False)
    ok = jnp.min(den_min) >= DEN_FLOOR

    @pl.when(jnp.logical_not(ok))
    def _():
        run(exact=True)


def _attn_call(sink, q, k, v, kx, vx, *, layer, band, tq):
    bsz, s, _ = q.shape
    n_ctx = kx.shape[1]
    nb = s // ATT_BLOCK
    tq = _row_tile(s, tq)
    nq = tq // ATT_BLOCK
    qspec = pl.BlockSpec((None, tq, Q_W), lambda bi, i: (bi, i, 0))
    xspec = pl.BlockSpec((None, n_ctx, KV_DUP_W), lambda bi, i: (bi, 0, 0))
    sspec = pl.BlockSpec(memory_space=pltpu.SMEM)
    scratch = []
    if band:
        edge = lambda f: pl.BlockSpec((None, ATT_BLOCK, KV_DUP_W), lambda bi, i: (bi, f(i), 0))
        prev = edge(lambda i: jnp.maximum(i * nq - 1, 0))
        nxt = edge(lambda i: jnp.minimum((i + 1) * nq, nb - 1))
        main = pl.BlockSpec((None, tq, KV_DUP_W), lambda bi, i: (bi, i, 0))
        in_specs = [sspec, qspec] + [prev, main, nxt] * 2 + [xspec, xspec]
        args = (sink, q, k, k, k, v, v, v, kx, vx)
        scratch = [pltpu.VMEM(((nq + 2) * ATT_BLOCK, KV_DUP_W), BF16)] * 2
    else:
        in_specs = [sspec, qspec, xspec, xspec]
        args = (sink, q, kx, vx)
    return pl.pallas_call(
        functools.partial(_attn_body, band=band, nq=nq, nb=nb, layer=layer),
        grid=(bsz, s // tq),
        in_specs=in_specs,
        out_specs=qspec,
        out_shape=jax.ShapeDtypeStruct((bsz, s, Q_W), BF16),
        scratch_shapes=scratch,
        compiler_params=_params(2),
        name="attn_band" if band else "attn_ctx",
    )(*args)


def _fourier_body(cn_ref, nsn_ref, ab_ref, o_ref, *, scale):
    y = _dot(cn_ref[...], ab_ref[:, :FNET_WIDTH]) + _dot(nsn_ref[...], ab_ref[:, FNET_WIDTH:])
    o_ref[...] = (y * scale).astype(BF16)


def _fourier_call(cn, nsn, ab, *, tk):
    bsz, s, _ = ab.shape
    tk = _row_tile(s, tk)
    scale = 1.0 / math.sqrt(s * FNET_GROUP_W)
    return pl.pallas_call(
        functools.partial(_fourier_body, scale=scale),
        grid=(bsz, s // tk),
        in_specs=[
            pl.BlockSpec((tk, s), lambda bi, t: (t, 0)),
            pl.BlockSpec((tk, s), lambda bi, t: (t, 0)),
            pl.BlockSpec((None, s, 2 * FNET_WIDTH), lambda bi, t: (bi, 0, 0)),
        ],
        out_specs=pl.BlockSpec((None, tk, FNET_WIDTH), lambda bi, t: (bi, t, 0)),
        out_shape=jax.ShapeDtypeStruct((bsz, s, FNET_WIDTH), BF16),
        compiler_params=_params(2),
        name="fourier",
    )(cn, nsn, ab)


def _fourier_fact_body(ab_ref, m1_ref, g_ref, o_ref, t_ref, *, n_low, scale):
    w = FNET_WIDTH
    for j in range(N_RES):
        blk = ab_ref[:, j * 2 * w:(j + 1) * 2 * w]
        d = jnp.concatenate([blk[:, :w], blk[:, w:]], axis=0)
        r = _dot(m1_ref[j], d)
        rows = slice(j * n_low, (j + 1) * n_low)
        t_ref[rows, :w] = r[:n_low].astype(BF16)
        t_ref[rows, w:] = r[n_low:].astype(BF16)
    for kb in range(n_low // K2_BLK):
        x = jnp.concatenate(
            [t_ref[j * n_low + kb * K2_BLK:j * n_low + (kb + 1) * K2_BLK, :] for j in range(N_RES)], axis=0)
        y = _dot(g_ref[0], x[:, :w]) + _dot(g_ref[1], x[:, w:])
        for k1 in range(N_RES):
            o_ref[k1 * n_low + kb * K2_BLK:k1 * n_low + (kb + 1) * K2_BLK, :] = (
                y[k1 * K2_BLK:(k1 + 1) * K2_BLK] * scale).astype(BF16)


def _fourier_fact_call(m1, g, ab_view):
    bsz, n_low, _ = ab_view.shape
    s = n_low * N_RES
    scale = 1.0 / math.sqrt(s * FNET_GROUP_W)
    return pl.pallas_call(
        functools.partial(_fourier_fact_body, n_low=n_low, scale=scale),
        grid=(bsz,),
        in_specs=[
            pl.BlockSpec((None, n_low, N_RES * 2 * FNET_WIDTH), lambda bi: (bi, 0, 0)),
            _const_spec(m1.shape), _const_spec(g.shape),
        ],
        out_specs=pl.BlockSpec((None, s, FNET_WIDTH), lambda bi: (bi, 0, 0)),
        out_shape=jax.ShapeDtypeStruct((bsz, s, FNET_WIDTH), BF16),
        scratch_shapes=[pltpu.VMEM((s, 2 * FNET_WIDTH), BF16)],
        compiler_params=_params(1),
        name="fourier_fact",
    )(ab_view, m1, g)


def _merge_body(x_ref, m_ref, a_ref, s_ref, c_ref, wg_ref, wb_ref, wo_ref, g_ref, b_ref, o_ref, mg_ref, *, sub):
    for st in range(x_ref.shape[0] // sub):
        rows = slice(st * sub, (st + 1) * sub)
        x = x_ref[rows, :]
        h = (x * (1 + m_ref[4:5, :]) + m_ref[3:4, :]).astype(BF16)
        branches = (a_ref[rows, :], s_ref[rows, :], c_ref[rows, :])
        for n in range(D_MODEL // MXU_W):
            sl = slice(n * MXU_W, (n + 1) * MXU_W)
            acc = None
            for r in range(N_BRANCH):
                t = jax.nn.sigmoid(_dot(h, wg_ref[r, :, sl])) * _dot(branches[r], wb_ref[r, :, sl])
                acc = t if acc is None else acc + t
            mg_ref[rows, sl] = acc.astype(BF16)
        y = _dot(mg_ref[rows, :], wo_ref[...])
        o_ref[rows, :] = _layer_norm(ALPHA * x + m_ref[5:6, :] * y, g_ref[...], b_ref[...])


def _merge_call(x, mods, a, sgu, c, wg, wb, wo, g, b, *, layer, ctx_row, tm, sub):
    bsz, s, _ = x.shape
    tm = _row_tile(s, tm)
    sub = min(sub, tm)
    row = lambda w: pl.BlockSpec((None, tm, w), lambda bi, t: (bi, t, 0))
    return pl.pallas_call(
        functools.partial(_merge_body, sub=sub),
        grid=(bsz, s // tm),
        in_specs=[
            row(D_MODEL), _mods_spec(layer, ctx_row),
            row(Q_W), row(SGU_WIDTH), row(FNET_WIDTH),
            _const_spec(wg.shape, (layer,)), _const_spec(wb.shape, (layer,)), _const_spec(wo.shape, (layer,)),
            _const_spec(g.shape, (layer, 1)), _const_spec(b.shape, (layer, 1)),
        ],
        out_specs=row(D_MODEL),
        out_shape=jax.ShapeDtypeStruct(x.shape, F32),
        scratch_shapes=[pltpu.VMEM((tm, D_MODEL), BF16)],
        compiler_params=_params(2),
        name="merge",
    )(x, mods, a, sgu, c, wg, wb, wo, g, b)


def _rope_tables(s):
    pos = jnp.arange(s)
    row = (pos // GRID_W).astype(F32)
    col = (pos % GRID_W).astype(F32)
    axis_dim = HEAD_DIM // 2
    inv_freq = ROPE_THETA ** (-jnp.arange(0, axis_dim, 2, dtype=F32) / axis_dim)
    ang_r = row[:, None] * inv_freq[None, :]
    ang_c = col[:, None] * inv_freq[None, :]
    cos = jnp.concatenate([jnp.cos(ang_r)] * 2 + [jnp.cos(ang_c)] * 2, axis=1)
    sin = jnp.concatenate([-jnp.sin(ang_r), jnp.sin(ang_r), -jnp.sin(ang_c), jnp.sin(ang_c)], axis=1)
    return jnp.tile(cos, (1, 2)), jnp.tile(sin, (1, 2))


def _dft_angles(n):
    k = jnp.arange(n, dtype=jnp.int32)
    r = (k[:, None] * k[None, :]) % n
    return r.astype(F32) * (2.0 * math.pi / n)


def _position_dft(n):
    ang = _dft_angles(n)
    return jnp.cos(ang).astype(BF16), (-jnp.sin(ang)).astype(BF16)


def _factored_dft(s):
    n_low = s // N_RES
    idx = jnp.arange(n_low, dtype=jnp.int32)
    n = jnp.arange(N_RES, dtype=jnp.int32)[:, None, None] + N_RES * idx[None, None, :]
    ang = ((idx[None, :, None] * n) % s).astype(F32) * (2.0 * math.pi / s)
    c, sn = jnp.cos(ang), jnp.sin(ang)
    m1 = jnp.concatenate([jnp.concatenate([c, -sn], axis=2), jnp.concatenate([sn, c], axis=2)], axis=1)
    ang_r = _dft_angles(N_RES)
    eye = jnp.eye(K2_BLK, dtype=F32)
    g = jnp.stack([jnp.kron(jnp.cos(ang_r), eye), -jnp.kron(jnp.sin(ang_r), eye)])
    return m1.astype(BF16), g.astype(BF16)


def _channel_dft():
    ang = _dft_angles(FNET_GROUP_W)
    return jnp.concatenate([jnp.cos(ang), jnp.sin(ang)], axis=1).astype(BF16)


def kernel(x, c, ctx, c_ctx, w_mod, b_mod, w_ffn_up, w_ffn_down, ln_g, ln_b, w_in, attn_sink,
           sgu_w, sgu_b, sgu_ln_g, sgu_ln_b, w_gate, w_branch, w_out):
    depth = w_mod.shape[0]
    bsz, s, _ = x.shape
    n_ctx = ctx.shape[1]

    w_gu = _pack_up_call(w_ffn_up)
    w_d = _pack_down_call(w_ffn_down)
    w_in_b = w_in.astype(BF16)
    w_gate_b = w_gate.astype(BF16)
    w_branch_b = w_branch.astype(BF16)
    w_out_b = w_out.astype(BF16)
    sgu_w_b = sgu_w.astype(BF16)
    sgu_b_col = sgu_b[..., None]
    ln_g4 = ln_g[:, :, None, :]
    ln_b4 = ln_b[:, :, None, :]
    sgu_g3 = sgu_ln_g[:, None, :]
    sgu_b3 = sgu_ln_b[:, None, :]

    tables = _rope_tables(s)
    dft_c = _channel_dft()
    m1_lat, g_lat = _factored_dft(s)
    cn_ctx, nsn_ctx = _position_dft(n_ctx)

    rows = -(-(bsz + 1) // 8) * 8
    cc = jnp.zeros((rows, D_MODEL), F32).at[:bsz].set(c).at[bsz].set(c_ctx)
    mods = _mods_call(cc, w_mod, b_mod).reshape(depth, rows, N_MOD, D_MODEL)

    def ffn(xx, i, ctx_row, half, j, tm):
        return _ffn_call(xx, mods, w_gu, w_d, ln_g4, ln_b4, layer=i, ctx_row=ctx_row, half=half, j=j,
                         tm=tm, sub=512)

    def inproj(xx, i, ctx_row, tb, tm):
        return _inproj_call(xx, mods, w_in_b, tb, sgu_w_b, sgu_b_col, sgu_g3, sgu_b3, dft_c,
                            layer=i, ctx_row=ctx_row, tm=tm, sub=512)

    def merge(xx, i, ctx_row, a, sg, cf, tm):
        return _merge_call(xx, mods, a, sg, cf, w_gate_b, w_branch_b, w_out_b, ln_g4, ln_b4,
                           layer=i, ctx_row=ctx_row, tm=tm, sub=512)

    flat = lambda a: a.reshape(1, bsz * n_ctx, a.shape[-1])
    unflat = lambda a: a.reshape(bsz, n_ctx, a.shape[-1])
    x_lat, x_ctx = x, flat(ctx)
    for i in range(depth):
        last = i == depth - 1
        x_lat = ffn(x_lat, i, None, 0, 0, 1024)
        x_ctx = ffn(x_ctx, i, bsz, 0, 0, 1024)

        q_c, k_c, v_c, s_c, ab_c = inproj(x_ctx, i, bsz, None, 1024)
        q_c, k_c, v_c, ab_c = unflat(q_c), unflat(k_c), unflat(v_c), unflat(ab_c)
        q_l, k_l, v_l, s_l, ab_l = inproj(x_lat, i, None, tables, 1024)

        a_l = _attn_call(attn_sink, q_l, k_l, v_l, k_c, v_c, layer=i, band=True, tq=1024)
        f_l = _fourier_fact_call(m1_lat, g_lat, ab_l)
        x_lat = merge(x_lat, i, None, a_l, s_l, f_l, 1024)
        x_lat = ffn(x_lat, i, None, 1, 2, 1024)
        if not last:
            a_c = _attn_call(attn_sink, q_c, None, None, k_c, v_c, layer=i, band=False, tq=256)
            f_c = _fourier_call(cn_ctx, nsn_ctx, ab_c, tk=256)
            x_ctx = merge(x_ctx, i, bsz, flat(a_c), s_c, flat(f_c), 1024)
            x_ctx = ffn(x_ctx, i, bsz, 1, 2, 1024)
    return x_lat
```

```python
import functools
import math

import jax
import jax.numpy as jnp
from jax import lax
from jax.experimental import pallas as pl
from jax.experimental.pallas import tpu as pltpu

F32 = jnp.float32
BF16 = jnp.bfloat16

D_MODEL = 1024
GRID_W = 64
HEAD_DIM = 64
N_Q_HEADS = 8
N_KV_HEADS = 2
GQA_GROUP = N_Q_HEADS // N_KV_HEADS
ATT_BLOCK = 128
ROPE_THETA = 10000.0
Q_W = N_Q_HEADS * HEAD_DIM
KV_W = N_KV_HEADS * HEAD_DIM
CHUNK = 128
SGU_GROUPS = 4
SGU_WIDTH = 512
FNET_GROUPS = 4
FNET_WIDTH = 512
FNET_GROUP_W = FNET_WIDTH // FNET_GROUPS
N_BRANCH = 3
D_FF = 2752
N_MOD = 9
MODEL_DEPTH = 4
ALPHA = (2 * MODEL_DEPTH) ** 0.25
LN_EPS = 1e-5
NEG_INF = -1e30

LANES = 128
MXU_W = 256
D_FF_PAD = -(-D_FF // MXU_W) * MXU_W
UP_OFF = D_FF % LANES
UP0 = D_FF - UP_OFF
assert UP0 + D_FF_PAD == 2 * D_FF
KV_DUP_W = 2 * KV_W
IN_Q, IN_K, IN_U, IN_Z, IN_F, IN_END = 0, 512, 768, 1280, 1792, 2304
VMEM_LIMIT = 56 * 1024 * 1024
N_RES = 32
K2_BLK = 16
STAGE_PITCH = 40
DEN_FLOOR = 2.0 ** -100
BOUND_SLACK = 1.01
LOG2E = math.log2(math.e)
Q_SCALE = HEAD_DIM ** -0.5 * LOG2E


def _params(n_axes):
    return pltpu.CompilerParams(dimension_semantics=("parallel",) * n_axes,
                                vmem_limit_bytes=VMEM_LIMIT)


def _const_spec(shape, lead=()):
    nl = len(lead)
    block = (None,) * nl + tuple(shape[nl:])
    index = tuple(lead) + (0,) * (len(shape) - nl)
    return pl.BlockSpec(block, lambda *_: index, pipeline_mode=pl.Buffered(1))


def _mods_spec(layer, ctx_row):
    if ctx_row is None:
        return pl.BlockSpec((None, None, N_MOD, D_MODEL), lambda bi, t: (layer, bi, 0, 0))
    return pl.BlockSpec((None, None, N_MOD, D_MODEL), lambda bi, t: (layer, ctx_row, 0, 0))


def _row_tile(s, pref):
    t = min(s, pref)
    while s % t:
        t -= LANES
    return t


def _dot(a, b):
    return jnp.dot(a, b, preferred_element_type=F32)


def _layer_norm(y, g, b):
    mu = jnp.mean(y, axis=-1, keepdims=True)
    d = y - mu
    var = jnp.mean(d * d, axis=-1, keepdims=True)
    return d * lax.rsqrt(var + LN_EPS) * g + b


def _mods_body(cc_ref, w_ref, b_ref, o_ref):
    s = cc_ref[...]
    s = (s * jax.nn.sigmoid(s)).astype(BF16)
    o_ref[...] = _dot(s, w_ref[...].astype(BF16)) + b_ref[...]


def _mods_call(cc, w_mod, b_mod):
    depth, _, n = w_mod.shape
    rows = cc.shape[0]
    tn = 1024
    return pl.pallas_call(
        _mods_body,
        grid=(depth, n // tn),
        in_specs=[
            pl.BlockSpec((rows, D_MODEL), lambda l, j: (0, 0)),
            pl.BlockSpec((None, D_MODEL, tn), lambda l, j: (l, 0, j)),
            pl.BlockSpec((None, 1, tn), lambda l, j: (l, 0, j)),
        ],
        out_specs=pl.BlockSpec((None, rows, tn), lambda l, j: (l, 0, j)),
        out_shape=jax.ShapeDtypeStruct((depth, rows, n), F32),
        compiler_params=_params(2),
        name="mods",
    )(cc, w_mod, b_mod.reshape(depth, 1, n))


def _pack_up_body(w_ref, o_ref):
    rows = w_ref.shape[0]
    lane = lax.broadcasted_iota(jnp.int32, (rows, D_FF_PAD), 1)
    real = lane < D_FF
    o_ref[:, :D_FF_PAD] = jnp.where(real, w_ref[:, :D_FF_PAD], 0.0).astype(BF16)
    up = pltpu.roll(w_ref[:, UP0:], D_FF_PAD - UP_OFF, axis=1)
    o_ref[:, D_FF_PAD:] = jnp.where(real, up, 0.0).astype(BF16)


def _pack_up_call(w_up):
    depth, two, d, n = w_up.shape
    tr = 256
    return pl.pallas_call(
        _pack_up_body,
        grid=(depth, two, d // tr),
        in_specs=[pl.BlockSpec((None, None, tr, n), lambda l, h, r: (l, h, r, 0))],
        out_specs=pl.BlockSpec((None, None, tr, 2 * D_FF_PAD), lambda l, h, r: (l, h, r, 0)),
        out_shape=jax.ShapeDtypeStruct((depth, two, d, 2 * D_FF_PAD), BF16),
        compiler_params=_params(3),
        name="pack_up",
    )(w_up)


def _pack_down_body(w_ref, o_ref):
    o_ref[:D_FF, :] = w_ref[...].astype(BF16)
    o_ref[D_FF:, :] = jnp.zeros((D_FF_PAD - D_FF, w_ref.shape[1]), BF16)


def _pack_down_call(w_down):
    depth, two, n, d = w_down.shape
    return pl.pallas_call(
        _pack_down_body,
        grid=(depth, two),
        in_specs=[pl.BlockSpec((None, None, n, d), lambda l, h: (l, h, 0, 0))],
        out_specs=pl.BlockSpec((None, None, D_FF_PAD, d), lambda l, h: (l, h, 0, 0)),
        out_shape=jax.ShapeDtypeStruct((depth, two, D_FF_PAD, d), BF16),
        compiler_params=_params(2),
        name="pack_down",
    )(w_down)


def _ffn_body(x_ref, m_ref, wgu_ref, wd_ref, g_ref, b_ref, o_ref, *, j, sub):
    shift = m_ref[3 * j:3 * j + 1, :]
    scale = m_ref[3 * j + 1:3 * j + 2, :]
    gate = m_ref[3 * j + 2:3 * j + 3, :]
    for r in range(x_ref.shape[0] // sub):
        rows = slice(r * sub, (r + 1) * sub)
        x = x_ref[rows, :]
        xm = (x * (1 + scale) + shift).astype(BF16)
        acc = None
        for c in range(D_FF_PAD // MXU_W):
            sl = slice(c * MXU_W, (c + 1) * MXU_W)
            su = slice(D_FF_PAD + c * MXU_W, D_FF_PAD + (c + 1) * MXU_W)
            gg = _dot(xm, wgu_ref[:, sl])
            uu = _dot(xm, wgu_ref[:, su])
            h = (gg * jax.nn.sigmoid(gg) * uu).astype(BF16)
            d = _dot(h, wd_ref[sl, :])
            acc = d if acc is None else acc + d
        y = ALPHA * x + (0.5 * gate) * acc
        o_ref[rows, :] = _layer_norm(y, g_ref[...], b_ref[...])


def _ffn_call(x, mods, wgu, wd, g, b, *, layer, ctx_row, half, j, tm, sub):
    bsz, s, _ = x.shape
    tm = _row_tile(s, tm)
    sub = min(sub, tm)
    return pl.pallas_call(
        functools.partial(_ffn_body, j=j, sub=sub),
        grid=(bsz, s // tm),
        in_specs=[
            pl.BlockSpec((None, tm, D_MODEL), lambda bi, t: (bi, t, 0)),
            _mods_spec(layer, ctx_row),
            _const_spec(wgu.shape, (layer, half)), _const_spec(wd.shape, (layer, half)),
            _const_spec(g.shape, (layer, j)), _const_spec(b.shape, (layer, j)),
        ],
        out_specs=pl.BlockSpec((None, tm, D_MODEL), lambda bi, t: (bi, t, 0)),
        out_shape=jax.ShapeDtypeStruct(x.shape, F32),
        compiler_params=_params(2),
        name="ffn",
    )(x, mods, wgu, wd, g, b)


def _gelu(x):
    return jax.nn.gelu(x)


def _inproj_body(*refs, rope, sub):
    if rope:
        (x_ref, m_ref, w_ref, cos_ref, sin_ref, sw_ref, sb_ref, sg_ref, sbn_ref, dft_ref,
         q_ref, k_ref, v_ref, s_ref, ab_ref, stage_ref) = refs
    else:
        (x_ref, m_ref, w_ref, sw_ref, sb_ref, sg_ref, sbn_ref, dft_ref,
         q_ref, k_ref, v_ref, s_ref, ab_ref) = refs
    lane = lax.broadcasted_iota(jnp.int32, (sub, LANES), 1)
    low_half = lane < HEAD_DIM
    first = (lane % 32) < 16

    for r in range(x_ref.shape[0] // sub):
        rows = slice(r * sub, (r + 1) * sub)
        x = x_ref[rows, :]
        xm = (x * (1 + m_ref[4:5, :]) + m_ref[3:4, :]).astype(BF16)

        if rope:
            cos = cos_ref[rows, :]
            sin = sin_ref[rows, :]

            def rot(t, cos=cos, sin=sin):
                partner = jnp.where(first, pltpu.roll(t, LANES - 16, axis=1), pltpu.roll(t, 16, axis=1))
                return t * cos + partner * sin
        else:
            def rot(t):
                return t

        q = _dot(xm, w_ref[:, IN_Q:IN_K])
        for c in range(Q_W // LANES):
            sl = slice(c * LANES, (c + 1) * LANES)
            q_ref[rows, sl] = (rot(q[:, sl]) * Q_SCALE).astype(BF16)
        kv = _dot(xm, w_ref[:, IN_K:IN_U])
        for t, o_ref in ((rot(kv[:, :KV_W]), k_ref), (kv[:, KV_W:], v_ref)):
            swapped = pltpu.roll(t, HEAD_DIM, axis=1)
            o_ref[rows, :LANES] = jnp.where(low_half, t, swapped).astype(BF16)
            o_ref[rows, LANES:] = jnp.where(low_half, swapped, t).astype(BF16)

        u = _gelu(_dot(xm, w_ref[:, IN_U:IN_Z]))
        z = _gelu(_dot(xm, w_ref[:, IN_Z:IN_F]))
        zn = _layer_norm(z, sg_ref[...], sbn_ref[...]).astype(BF16)
        for ci in range(sub // CHUNK):
            rs = slice(ci * CHUNK, (ci + 1) * CHUNK)
            ro = slice(r * sub + ci * CHUNK, r * sub + (ci + 1) * CHUNK)
            for gi in range(SGU_GROUPS):
                cs = slice(gi * LANES, (gi + 1) * LANES)
                mixed = _dot(sw_ref[gi], zn[rs, cs]) + sb_ref[gi]
                s_ref[ro, cs] = (u[rs, cs] * mixed).astype(BF16)

        f = _dot(xm, w_ref[:, IN_F:IN_END]).astype(BF16)
        n_cb = 2 * FNET_WIDTH // LANES
        ab = [None] * n_cb
        for gi in range(FNET_GROUPS):
            cs_ = _dot(f[:, gi * FNET_GROUP_W:(gi + 1) * FNET_GROUP_W], dft_ref[...])
            ab[gi] = cs_[:, :FNET_GROUP_W]
            ab[FNET_GROUPS + gi] = cs_[:, FNET_GROUP_W:]
        if rope:
            n2 = sub // N_RES
            base = r * n2 * STAGE_PITCH
            for g in range(n2):
                for cb in range(n_cb):
                    stage_ref[cb, base + g * STAGE_PITCH:base + g * STAGE_PITCH + N_RES, :] = (
                        ab[cb][g * N_RES:(g + 1) * N_RES, :])
            for j in range(N_RES):
                cols = slice(j * 2 * FNET_WIDTH, (j + 1) * 2 * FNET_WIDTH)
                piece = [stage_ref[cb, pl.ds(base + j, n2, stride=STAGE_PITCH), :] for cb in range(n_cb)]
                ab_ref[r * n2:(r + 1) * n2, cols] = jnp.concatenate(piece, axis=1).astype(BF16)
        else:
            for cb in range(n_cb):
                ab_ref[rows, cb * LANES:(cb + 1) * LANES] = ab[cb].astype(BF16)


def _inproj_call(x, mods, w_in, tables, sw, sb, sg, sbn, dft_c, *, layer, ctx_row, tm, sub):
    bsz, s, _ = x.shape
    tm = _row_tile(s, tm)
    sub = min(sub, tm)
    rope = tables is not None
    row = lambda w: pl.BlockSpec((None, tm, w), lambda bi, t: (bi, t, 0))
    in_specs = [row(D_MODEL), _mods_spec(layer, ctx_row), _const_spec(w_in.shape, (layer,))]
    args = [x, mods, w_in]
    if rope:
        in_specs += [pl.BlockSpec((tm, LANES), lambda bi, t: (t, 0))] * 2
        args += list(tables)
    in_specs += [_const_spec(a.shape, (layer,)) for a in (sw, sb, sg, sbn)] + [_const_spec(dft_c.shape)]
    args += [sw, sb, sg, sbn, dft_c]
    widths = (Q_W, KV_DUP_W, KV_DUP_W, SGU_WIDTH, 2 * FNET_WIDTH)
    out_specs = [row(w) for w in widths]
    out_shape = [jax.ShapeDtypeStruct((bsz, s, w), BF16) for w in widths]
    scratch = []
    if rope:
        ab_w = N_RES * 2 * FNET_WIDTH
        out_specs[-1] = pl.BlockSpec((None, tm // N_RES, ab_w), lambda bi, t: (bi, t, 0))
        out_shape[-1] = jax.ShapeDtypeStruct((bsz, s // N_RES, ab_w), BF16)
        scratch = [pltpu.VMEM((2 * FNET_WIDTH // LANES, tm // N_RES * STAGE_PITCH, LANES), F32)]
    return pl.pallas_call(
        functools.partial(_inproj_body, rope=rope, sub=sub),
        grid=(bsz, s // tm),
        in_specs=in_specs,
        out_specs=out_specs,
        out_shape=out_shape,
        scratch_shapes=scratch,
        compiler_params=_params(2),
        name="inproj",
    )(*args)


def _attn_body(*refs, band, nq, nb, layer):
    if band:
        (sink_ref, q_ref, kp_ref, km_ref, kn_ref, vp_ref, vm_ref, vn_ref, kx_ref, vx_ref,
         o_ref, kw_ref, vw_ref) = refs
    else:
        sink_ref, q_ref, kx_ref, vx_ref, o_ref = refs
    blk = ATT_BLOCK
    n_ctx = kx_ref.shape[0]
    t = pl.program_id(1)
    row = lax.broadcasted_iota(jnp.int32, (blk, blk), 0)
    col = lax.broadcasted_iota(jnp.int32, (blk, blk), 1)
    lo = col < HEAD_DIM
    zero = jnp.zeros((blk, blk), BF16)
    nt_dims = (((1,), (1,)), ((), ()))
    if band:
        kw_ref[0:blk] = kp_ref[...]
        kw_ref[blk:(nq + 1) * blk] = km_ref[...]
        kw_ref[(nq + 1) * blk:(nq + 2) * blk] = kn_ref[...]
        vw_ref[0:blk] = vp_ref[...]
        vw_ref[blk:(nq + 1) * blk] = vm_ref[...]
        vw_ref[(nq + 1) * blk:(nq + 2) * blk] = vn_ref[...]

    def max_key_norm_sq(kh):
        cs = slice(kh * LANES, (kh + 1) * LANES)
        best = None
        for ref in ((kw_ref, kx_ref) if band else (kx_ref,)):
            kf = ref[:, cs].astype(F32)
            sq = jnp.max(jnp.sum(kf * kf, axis=1, keepdims=True), axis=0, keepdims=True)
            best = sq if best is None else jnp.maximum(best, sq)
        return 0.5 * best

    def run(exact):
        den_min = None
        k2 = None if exact else [max_key_norm_sq(kh) for kh in range(N_KV_HEADS)]
        for qi in range(nq):
            rows = slice(qi * blk, (qi + 1) * blk)
            if band:
                gi = t * nq + qi
                prev_ok = jnp.logical_and(col >= row, gi > 0)
                next_ok = jnp.logical_and(col <= row, gi < nb - 1)
            for kh in range(N_KV_HEADS):
                cs = slice(kh * LANES, (kh + 1) * LANES)
                qs = []
                for pr in range(GQA_GROUP // 2):
                    pair = q_ref[rows, (2 * kh + pr) * LANES:(2 * kh + pr + 1) * LANES]
                    qs.append(jnp.where(lo, pair, zero))
                    qs.append(jnp.where(lo, zero, pair))
                qstack = jnp.concatenate(qs, axis=0)
                if not exact:
                    qf = qstack.astype(F32)
                    q2 = jnp.sum(qf * qf, axis=1, keepdims=True)
                sc_ctx = lax.dot_general(qstack, kx_ref[:, cs], nt_dims, preferred_element_type=F32)
                if band:
                    win = slice(qi * blk, (qi + 3) * blk)
                    sc_band = lax.dot_general(qstack, kw_ref[win, cs], nt_dims, preferred_element_type=F32)
                ps, dens = [], []
                for g in range(GQA_GROUP):
                    hr = slice(g * blk, (g + 1) * blk)
                    pieces = [sc_ctx[hr, c * blk:(c + 1) * blk] for c in range(n_ctx // blk)]
                    if band:
                        pieces = [jnp.where(prev_ok, sc_band[hr, 0:blk], NEG_INF),
                                  sc_band[hr, blk:2 * blk],
                                  jnp.where(next_ok, sc_band[hr, 2 * blk:3 * blk], NEG_INF)] + pieces
                    sink = sink_ref[layer, kh * GQA_GROUP + g] * LOG2E
                    if exact:
                        mx = pieces[0]
                        for pc in pieces[1:]:
                            mx = jnp.maximum(mx, pc)
                        m = jnp.maximum(jnp.max(mx, axis=1, keepdims=True), sink)
                    else:
                        q2max = jnp.max(q2[hr], axis=0, keepdims=True)
                        m = jnp.maximum(jnp.sqrt(q2max * k2[kh]) * BOUND_SLACK, sink)
                    es = [jnp.exp2(pc - m) for pc in pieces]
                    tot = es[0]
                    for e in es[1:]:
                        tot = tot + e
                    den = jnp.sum(tot, axis=1, keepdims=True) + jnp.exp2(sink - m)
                    dens.append(den)
                    if not exact:
                        den_min = den if den_min is None else jnp.minimum(den_min, den)
                    ps.append(jnp.concatenate([e.astype(BF16) for e in es], axis=1))
                pall = jnp.concatenate(ps, axis=0)
                if band:
                    o = _dot(pall[:, :3 * blk], vw_ref[win, cs]) + _dot(pall[:, 3 * blk:], vx_ref[:, cs])
                else:
                    o = _dot(pall, vx_ref[:, cs])
                outs = [o[g * blk:(g + 1) * blk] / dens[g] for g in range(GQA_GROUP)]
                for pr in range(GQA_GROUP // 2):
                    pair = jnp.where(lo, outs[2 * pr], outs[2 * pr + 1])
                    o_ref[rows, (2 * kh + pr) * LANES:(2 * kh + pr + 1) * LANES] = pair.astype(BF16)
        return den_min

    den_min = run(exact=False)
    ok = jnp.min(den_min) >= DEN_FLOOR

    @pl.when(jnp.logical_not(ok))
    def _():
        run(exact=True)


def _attn_call(sink, q, k, v, kx, vx, *, layer, band, tq):
    bsz, s, _ = q.shape
    n_ctx = kx.shape[1]
    nb = s // ATT_BLOCK
    tq = _row_tile(s, tq)
    nq = tq // ATT_BLOCK
    qspec = pl.BlockSpec((None, tq, Q_W), lambda bi, i: (bi, i, 0))
    xspec = pl.BlockSpec((None, n_ctx, KV_DUP_W), lambda bi, i: (bi, 0, 0))
    sspec = pl.BlockSpec(memory_space=pltpu.SMEM)
    scratch = []
    if band:
        edge = lambda f: pl.BlockSpec((None, ATT_BLOCK, KV_DUP_W), lambda bi, i: (bi, f(i), 0))
        prev = edge(lambda i: jnp.maximum(i * nq - 1, 0))
        nxt = edge(lambda i: jnp.minimum((i + 1) * nq, nb - 1))
        main = pl.BlockSpec((None, tq, KV_DUP_W), lambda bi, i: (bi, i, 0))
        in_specs = [sspec, qspec] + [prev, main, nxt] * 2 + [xspec, xspec]
        args = (sink, q, k, k, k, v, v, v, kx, vx)
        scratch = [pltpu.VMEM(((nq + 2) * ATT_BLOCK, KV_DUP_W), BF16)] * 2
    else:
        in_specs = [sspec, qspec, xspec, xspec]
        args = (sink, q, kx, vx)
    return pl.pallas_call(
        functools.partial(_attn_body, band=band, nq=nq, nb=nb, layer=layer),
        grid=(bsz, s // tq),
        in_specs=in_specs,
        out_specs=qspec,
        out_shape=jax.ShapeDtypeStruct((bsz, s, Q_W), BF16),
        scratch_shapes=scratch,
        compiler_params=_params(2),
        name="attn_band" if band else "attn_ctx",
    )(*args)


def _fourier_body(cn_ref, nsn_ref, ab_ref, o_ref, *, scale):
    y = _dot(cn_ref[...], ab_ref[:, :FNET_WIDTH]) + _dot(nsn_ref[...], ab_ref[:, FNET_WIDTH:])
    o_ref[...] = (y * scale).astype(BF16)


def _fourier_call(cn, nsn, ab, *, tk):
    bsz, s, _ = ab.shape
    tk = _row_tile(s, tk)
    scale = 1.0 / math.sqrt(s * FNET_GROUP_W)
    return pl.pallas_call(
        functools.partial(_fourier_body, scale=scale),
        grid=(bsz, s // tk),
        in_specs=[
            pl.BlockSpec((tk, s), lambda bi, t: (t, 0)),
            pl.BlockSpec((tk, s), lambda bi, t: (t, 0)),
            pl.BlockSpec((None, s, 2 * FNET_WIDTH), lambda bi, t: (bi, 0, 0)),
        ],
        out_specs=pl.BlockSpec((None, tk, FNET_WIDTH), lambda bi, t: (bi, t, 0)),
        out_shape=jax.ShapeDtypeStruct((bsz, s, FNET_WIDTH), BF16),
        compiler_params=_params(2),
        name="fourier",
    )(cn, nsn, ab)


def _fourier_fact_body(ab_ref, m1_ref, g_ref, o_ref, t_ref, *, n_low, scale):
    w = FNET_WIDTH
    for j in range(N_RES):
        blk = ab_ref[:, j * 2 * w:(j + 1) * 2 * w]
        d = jnp.concatenate([blk[:, :w], blk[:, w:]], axis=0)
        r = _dot(m1_ref[j], d)
        rows = slice(j * n_low, (j + 1) * n_low)
        t_ref[rows, :w] = r[:n_low].astype(BF16)
        t_ref[rows, w:] = r[n_low:].astype(BF16)
    for kb in range(n_low // K2_BLK):
        x = jnp.concatenate(
            [t_ref[j * n_low + kb * K2_BLK:j * n_low + (kb + 1) * K2_BLK, :] for j in range(N_RES)], axis=0)
        y = _dot(g_ref[0], x[:, :w]) + _dot(g_ref[1], x[:, w:])
        for k1 in range(N_RES):
            o_ref[k1 * n_low + kb * K2_BLK:k1 * n_low + (kb + 1) * K2_BLK, :] = (
                y[k1 * K2_BLK:(k1 + 1) * K2_BLK] * scale).astype(BF16)


def _fourier_fact_call(m1, g, ab_view):
    bsz, n_low, _ = ab_view.shape
    s = n_low * N_RES
    scale = 1.0 / math.sqrt(s * FNET_GROUP_W)
    return pl.pallas_call(
        functools.partial(_fourier_fact_body, n_low=n_low, scale=scale),
        grid=(bsz,),
        in_specs=[
            pl.BlockSpec((None, n_low, N_RES * 2 * FNET_WIDTH), lambda bi: (bi, 0, 0)),
            _const_spec(m1.shape), _const_spec(g.shape),
        ],
        out_specs=pl.BlockSpec((None, s, FNET_WIDTH), lambda bi: (bi, 0, 0)),
        out_shape=jax.ShapeDtypeStruct((bsz, s, FNET_WIDTH), BF16),
        scratch_shapes=[pltpu.VMEM((s, 2 * FNET_WIDTH), BF16)],
        compiler_params=_params(1),
        name="fourier_fact",
    )(ab_view, m1, g)


def _merge_body(x_ref, m_ref, a_ref, s_ref, c_ref, wg_ref, wb_ref, wo_ref, g_ref, b_ref, o_ref, mg_ref, *, sub):
    for st in range(x_ref.shape[0] // sub):
        rows = slice(st * sub, (st + 1) * sub)
        x = x_ref[rows, :]
        h = (x * (1 + m_ref[4:5, :]) + m_ref[3:4, :]).astype(BF16)
        branches = (a_ref[rows, :], s_ref[rows, :], c_ref[rows, :])
        for n in range(D_MODEL // MXU_W):
            sl = slice(n * MXU_W, (n + 1) * MXU_W)
            acc = None
            for r in range(N_BRANCH):
                t = jax.nn.sigmoid(_dot(h, wg_ref[r, :, sl])) * _dot(branches[r], wb_ref[r, :, sl])
                acc = t if acc is None else acc + t
            mg_ref[rows, sl] = acc.astype(BF16)
        y = _dot(mg_ref[rows, :], wo_ref[...])
        o_ref[rows, :] = _layer_norm(ALPHA * x + m_ref[5:6, :] * y, g_ref[...], b_ref[...])


def _merge_call(x, mods, a, sgu, c, wg, wb, wo, g, b, *, layer, ctx_row, tm, sub):
    bsz, s, _ = x.shape
    tm = _row_tile(s, tm)
    sub = min(sub, tm)
    row = lambda w: pl.BlockSpec((None, tm, w), lambda bi, t: (bi, t, 0))
    return pl.pallas_call(
        functools.partial(_merge_body, sub=sub),
        grid=(bsz, s // tm),
        in_specs=[
            row(D_MODEL), _mods_spec(layer, ctx_row),
            row(Q_W), row(SGU_WIDTH), row(FNET_WIDTH),
            _const_spec(wg.shape, (layer,)), _const_spec(wb.shape, (layer,)), _const_spec(wo.shape, (layer,)),
            _const_spec(g.shape, (layer, 1)), _const_spec(b.shape, (layer, 1)),
        ],
        out_specs=row(D_MODEL),
        out_shape=jax.ShapeDtypeStruct(x.shape, F32),
        scratch_shapes=[pltpu.VMEM((tm, D_MODEL), BF16)],
        compiler_params=_params(2),
        name="merge",
    )(x, mods, a, sgu, c, wg, wb, wo, g, b)


def _rope_tables(s):
    pos = jnp.arange(s)
    row = (pos // GRID_W).astype(F32)
    col = (pos % GRID_W).astype(F32)
    axis_dim = HEAD_DIM // 2
    inv_freq = ROPE_THETA ** (-jnp.arange(0, axis_dim, 2, dtype=F32) / axis_dim)
    ang_r = row[:, None] * inv_freq[None, :]
    ang_c = col[:, None] * inv_freq[None, :]
    cos = jnp.concatenate([jnp.cos(ang_r)] * 2 + [jnp.cos(ang_c)] * 2, axis=1)
    sin = jnp.concatenate([-jnp.sin(ang_r), jnp.sin(ang_r), -jnp.sin(ang_c), jnp.sin(ang_c)], axis=1)
    return jnp.tile(cos, (1, 2)), jnp.tile(sin, (1, 2))


def _dft_angles(n):
    k = jnp.arange(n, dtype=jnp.int32)
    r = (k[:, None] * k[None, :]) % n
    return r.astype(F32) * (2.0 * math.pi / n)


def _position_dft(n):
    ang = _dft_angles(n)
    return jnp.cos(ang).astype(BF16), (-jnp.sin(ang)).astype(BF16)


def _factored_dft(s):
    n_low = s // N_RES
    idx = jnp.arange(n_low, dtype=jnp.int32)
    n = jnp.arange(N_RES, dtype=jnp.int32)[:, None, None] + N_RES * idx[None, None, :]
    ang = ((idx[None, :, None] * n) % s).astype(F32) * (2.0 * math.pi / s)
    c, sn = jnp.cos(ang), jnp.sin(ang)
    m1 = jnp.concatenate([jnp.concatenate([c, -sn], axis=2), jnp.concatenate([sn, c], axis=2)], axis=1)
    ang_r = _dft_angles(N_RES)
    eye = jnp.eye(K2_BLK, dtype=F32)
    g = jnp.stack([jnp.kron(jnp.cos(ang_r), eye), -jnp.kron(jnp.sin(ang_r), eye)])
    return m1.astype(BF16), g.astype(BF16)


def _channel_dft():
    ang = _dft_angles(FNET_GROUP_W)
    return jnp.concatenate([jnp.cos(ang), jnp.sin(ang)], axis=1).astype(BF16)


def kernel(x, c, ctx, c_ctx, w_mod, b_mod, w_ffn_up, w_ffn_down, ln_g, ln_b, w_in, attn_sink,
           sgu_w, sgu_b, sgu_ln_g, sgu_ln_b, w_gate, w_branch, w_out):
    depth = w_mod.shape[0]
    bsz, s, _ = x.shape
    n_ctx = ctx.shape[1]

    w_gu = _pack_up_call(w_ffn_up)
    w_d = _pack_down_call(w_ffn_down)
    w_in_b = w_in.astype(BF16)
    w_gate_b = w_gate.astype(BF16)
    w_branch_b = w_branch.astype(BF16)
    w_out_b = w_out.astype(BF16)
    sgu_w_b = sgu_w.astype(BF16)
    sgu_b_col = sgu_b[..., None]
    ln_g4 = ln_g[:, :, None, :]
    ln_b4 = ln_b[:, :, None, :]
    sgu_g3 = sgu_ln_g[:, None, :]
    sgu_b3 = sgu_ln_b[:, None, :]

    tables = _rope_tables(s)
    dft_c = _channel_dft()
    m1_lat, g_lat = _factored_dft(s)
    cn_ctx, nsn_ctx = _position_dft(n_ctx)

    rows = -(-(bsz + 1) // 8) * 8
    cc = jnp.zeros((rows, D_MODEL), F32).at[:bsz].set(c).at[bsz].set(c_ctx)
    mods = _mods_call(cc, w_mod, b_mod).reshape(depth, rows, N_MOD, D_MODEL)

    def ffn(xx, i, ctx_row, half, j, tm):
        return _ffn_call(xx, mods, w_gu, w_d, ln_g4, ln_b4, layer=i, ctx_row=ctx_row, half=half, j=j,
                         tm=tm, sub=512)

    def inproj(xx, i, ctx_row, tb, tm):
        return _inproj_call(xx, mods, w_in_b, tb, sgu_w_b, sgu_b_col, sgu_g3, sgu_b3, dft_c,
                            layer=i, ctx_row=ctx_row, tm=tm, sub=512)

    def merge(xx, i, ctx_row, a, sg, cf, tm):
        return _merge_call(xx, mods, a, sg, cf, w_gate_b, w_branch_b, w_out_b, ln_g4, ln_b4,
                           layer=i, ctx_row=ctx_row, tm=tm, sub=512)

    flat = lambda a: a.reshape(1, bsz * n_ctx, a.shape[-1])
    unflat = lambda a: a.reshape(bsz, n_ctx, a.shape[-1])
    x_lat, x_ctx = x, flat(ctx)
    for i in range(depth):
        last = i == depth - 1
        x_lat = ffn(x_lat, i, None, 0, 0, 1024)
        x_ctx = ffn(x_ctx, i, bsz, 0, 0, 1024)

        q_c, k_c, v_c, s_c, ab_c = inproj(x_ctx, i, bsz, None, 1024)
        q_c, k_c, v_c, ab_c = unflat(q_c), unflat(k_c), unflat(v_c), unflat(ab_c)
        q_l, k_l, v_l, s_l, ab_l = inproj(x_lat, i, None, tables, 1024)

        a_l = _attn_call(attn_sink, q_l, k_l, v_l, k_c, v_c, layer=i, band=True, tq=1024)
        f_l = _fourier_fact_call(m1_lat, g_lat, ab_l)
        x_lat = merge(x_lat, i, None, a_l, s_l, f_l, 1024)
        x_lat = ffn(x_lat, i, None, 1, 2, 1024)
        if not last:
            a_c = _attn_call(attn_sink, q_c, None, None, k_c, v_c, layer=i, band=False, tq=256)
            f_c = _fourier_call(cn_ctx, nsn_ctx, ab_c, tk=256)
            x_ctx = merge(x_ctx, i, bsz, flat(a_c), s_c, flat(f_c), 1024)
            x_ctx = ffn(x_ctx, i, bsz, 1, 2, 1024)
    return x_lat
```

```python
import functools
import math

import jax
import jax.numpy as jnp
from jax import lax
from jax.experimental import pallas as pl
from jax.experimental.pallas import tpu as pltpu

F32 = jnp.float32
BF16 = jnp.bfloat16

D_MODEL = 1024
GRID_W = 64
HEAD_DIM = 64
N_Q_HEADS = 8
N_KV_HEADS = 2
GQA_GROUP = N_Q_HEADS // N_KV_HEADS
ATT_BLOCK = 128
ROPE_THETA = 10000.0
Q_W = N_Q_HEADS * HEAD_DIM
KV_W = N_KV_HEADS * HEAD_DIM
CHUNK = 128
SGU_GROUPS = 4
SGU_WIDTH = 512
FNET_GROUPS = 4
FNET_WIDTH = 512
FNET_GROUP_W = FNET_WIDTH // FNET_GROUPS
N_BRANCH = 3
D_FF = 2752
N_MOD = 9
MODEL_DEPTH = 4
ALPHA = (2 * MODEL_DEPTH) ** 0.25
LN_EPS = 1e-5
NEG_INF = -1e30

LANES = 128
MXU_W = 256
D_FF_PAD = -(-D_FF // MXU_W) * MXU_W
UP_OFF = D_FF % LANES
UP0 = D_FF - UP_OFF
assert UP0 + D_FF_PAD == 2 * D_FF
KV_DUP_W = 2 * KV_W
IN_Q, IN_K, IN_U, IN_Z, IN_F, IN_END = 0, 512, 768, 1280, 1792, 2304
VMEM_LIMIT = 56 * 1024 * 1024
ROW_TILE = 1024
SUB_TILE = 512
ATT_Q_TILE = 1024
PACK_ROWS = 256
MODS_COLS = 1024
N_RES = 32
BF16_ROWS = 16
K2_BLK = 8
STAGE_PITCH = 40
DEN_FLOOR = 2.0 ** -100
BOUND_SLACK = 1.01
LOG2E = math.log2(math.e)
Q_SCALE = HEAD_DIM ** -0.5 * LOG2E


def _params(n_axes):
    return pltpu.CompilerParams(dimension_semantics=("parallel",) * n_axes,
                                vmem_limit_bytes=VMEM_LIMIT)


def _const_spec(shape, lead=()):
    nl = len(lead)
    block = (None,) * nl + tuple(shape[nl:])
    index = tuple(lead) + (0,) * (len(shape) - nl)
    return pl.BlockSpec(block, lambda *_: index, pipeline_mode=pl.Buffered(1))


def _mods_spec(layer, ctx_row):
    if ctx_row is None:
        return pl.BlockSpec((None, None, N_MOD, D_MODEL), lambda bi, t: (layer, bi, 0, 0))
    return pl.BlockSpec((None, None, N_MOD, D_MODEL), lambda bi, t: (layer, ctx_row, 0, 0))


def _row_tile(s, pref):
    t = min(s, pref)
    while s % t:
        t -= LANES
    return t


def _dot(a, b):
    return jnp.dot(a, b, preferred_element_type=F32)


def _layer_norm(y, g, b):
    mu = jnp.mean(y, axis=-1, keepdims=True)
    d = y - mu
    var = jnp.mean(d * d, axis=-1, keepdims=True)
    return d * lax.rsqrt(var + LN_EPS) * g + b


def _mods_body(cc_ref, w_ref, b_ref, o_ref):
    s = cc_ref[...]
    s = (s * jax.nn.sigmoid(s)).astype(BF16)
    o_ref[...] = _dot(s, w_ref[...].astype(BF16)) + b_ref[...]


def _mods_call(cc, w_mod, b_mod):
    depth, _, n = w_mod.shape
    rows = cc.shape[0]
    tn = MODS_COLS
    return pl.pallas_call(
        _mods_body,
        grid=(depth, n // tn),
        in_specs=[
            pl.BlockSpec((rows, D_MODEL), lambda l, j: (0, 0)),
            pl.BlockSpec((None, D_MODEL, tn), lambda l, j: (l, 0, j)),
            pl.BlockSpec((None, 1, tn), lambda l, j: (l, 0, j)),
        ],
        out_specs=pl.BlockSpec((None, rows, tn), lambda l, j: (l, 0, j)),
        out_shape=jax.ShapeDtypeStruct((depth, rows, n), F32),
        compiler_params=_params(2),
        name="mods",
    )(cc, w_mod, b_mod.reshape(depth, 1, n))


def _pack_up_body(w_ref, o_ref):
    rows = w_ref.shape[0]
    lane = lax.broadcasted_iota(jnp.int32, (rows, D_FF_PAD), 1)
    real = lane < D_FF
    o_ref[:, :D_FF_PAD] = jnp.where(real, w_ref[:, :D_FF_PAD], 0.0).astype(BF16)
    up = pltpu.roll(w_ref[:, UP0:], D_FF_PAD - UP_OFF, axis=1)
    o_ref[:, D_FF_PAD:] = jnp.where(real, up, 0.0).astype(BF16)


def _pack_up_call(w_up):
    depth, two, d, n = w_up.shape
    tr = PACK_ROWS
    return pl.pallas_call(
        _pack_up_body,
        grid=(depth, two, d // tr),
        in_specs=[pl.BlockSpec((None, None, tr, n), lambda l, h, r: (l, h, r, 0))],
        out_specs=pl.BlockSpec((None, None, tr, 2 * D_FF_PAD), lambda l, h, r: (l, h, r, 0)),
        out_shape=jax.ShapeDtypeStruct((depth, two, d, 2 * D_FF_PAD), BF16),
        compiler_params=_params(3),
        name="pack_up",
    )(w_up)


def _pack_down_body(w_ref, o_ref):
    o_ref[:D_FF, :] = w_ref[...].astype(BF16)
    o_ref[D_FF:, :] = jnp.zeros((D_FF_PAD - D_FF, w_ref.shape[1]), BF16)


def _pack_down_call(w_down):
    depth, two, n, d = w_down.shape
    return pl.pallas_call(
        _pack_down_body,
        grid=(depth, two),
        in_specs=[pl.BlockSpec((None, None, n, d), lambda l, h: (l, h, 0, 0))],
        out_specs=pl.BlockSpec((None, None, D_FF_PAD, d), lambda l, h: (l, h, 0, 0)),
        out_shape=jax.ShapeDtypeStruct((depth, two, D_FF_PAD, d), BF16),
        compiler_params=_params(2),
        name="pack_down",
    )(w_down)


def _ffn_body(x_ref, m_ref, wgu_ref, wd_ref, g_ref, b_ref, o_ref, *, j, sub):
    shift = m_ref[3 * j:3 * j + 1, :]
    scale = m_ref[3 * j + 1:3 * j + 2, :]
    gate = m_ref[3 * j + 2:3 * j + 3, :]
    for r in range(x_ref.shape[0] // sub):
        rows = slice(r * sub, (r + 1) * sub)
        x = x_ref[rows, :]
        xm = (x * (1 + scale) + shift).astype(BF16)
        acc = None
        for c in range(D_FF_PAD // MXU_W):
            sl = slice(c * MXU_W, (c + 1) * MXU_W)
            su = slice(D_FF_PAD + c * MXU_W, D_FF_PAD + (c + 1) * MXU_W)
            gg = _dot(xm, wgu_ref[:, sl])
            uu = _dot(xm, wgu_ref[:, su])
            h = (gg * jax.nn.sigmoid(gg) * uu).astype(BF16)
            d = _dot(h, wd_ref[sl, :])
            acc = d if acc is None else acc + d
        y = ALPHA * x + (0.5 * gate) * acc
        o_ref[rows, :] = _layer_norm(y, g_ref[...], b_ref[...])


def _ffn_call(x, mods, wgu, wd, g, b, *, layer, ctx_row, half, j, tm, sub):
    bsz, s, _ = x.shape
    tm = _row_tile(s, tm)
    sub = min(sub, tm)
    return pl.pallas_call(
        functools.partial(_ffn_body, j=j, sub=sub),
        grid=(bsz, s // tm),
        in_specs=[
            pl.BlockSpec((None, tm, D_MODEL), lambda bi, t: (bi, t, 0)),
            _mods_spec(layer, ctx_row),
            _const_spec(wgu.shape, (layer, half)), _const_spec(wd.shape, (layer, half)),
            _const_spec(g.shape, (layer, j)), _const_spec(b.shape, (layer, j)),
        ],
        out_specs=pl.BlockSpec((None, tm, D_MODEL), lambda bi, t: (bi, t, 0)),
        out_shape=jax.ShapeDtypeStruct(x.shape, F32),
        compiler_params=_params(2),
        name="ffn",
    )(x, mods, wgu, wd, g, b)


def _gelu(x):
    return jax.nn.gelu(x)


def _inproj_body(*refs, rope, sub):
    if rope:
        (x_ref, m_ref, w_ref, cos_ref, sin_ref, sw_ref, sb_ref, sg_ref, sbn_ref, dft_ref,
         q_ref, k_ref, v_ref, s_ref, ab_ref, stage_ref) = refs
    else:
        (x_ref, m_ref, w_ref, sw_ref, sb_ref, sg_ref, sbn_ref, dft_ref,
         q_ref, k_ref, v_ref, s_ref, ab_ref) = refs
    lane = lax.broadcasted_iota(jnp.int32, (sub, LANES), 1)
    low_half = lane < HEAD_DIM
    first = (lane % 32) < 16

    for r in range(x_ref.shape[0] // sub):
        rows = slice(r * sub, (r + 1) * sub)
        x = x_ref[rows, :]
        xm = (x * (1 + m_ref[4:5, :]) + m_ref[3:4, :]).astype(BF16)

        if rope:
            cos = cos_ref[rows, :]
            sin = sin_ref[rows, :]

            def rot(t, cos=cos, sin=sin):
                partner = jnp.where(first, pltpu.roll(t, LANES - 16, axis=1), pltpu.roll(t, 16, axis=1))
                return t * cos + partner * sin
        else:
            def rot(t):
                return t

        q = _dot(xm, w_ref[:, IN_Q:IN_K])
        for c in range(Q_W // LANES):
            sl = slice(c * LANES, (c + 1) * LANES)
            q_ref[rows, sl] = (rot(q[:, sl]) * Q_SCALE).astype(BF16)
        kv = _dot(xm, w_ref[:, IN_K:IN_U])
        for t, o_ref in ((rot(kv[:, :KV_W]), k_ref), (kv[:, KV_W:], v_ref)):
            swapped = pltpu.roll(t, HEAD_DIM, axis=1)
            o_ref[rows, :LANES] = jnp.where(low_half, t, swapped).astype(BF16)
            o_ref[rows, LANES:] = jnp.where(low_half, swapped, t).astype(BF16)

        u = _gelu(_dot(xm, w_ref[:, IN_U:IN_Z]))
        z = _gelu(_dot(xm, w_ref[:, IN_Z:IN_F]))
        zn = _layer_norm(z, sg_ref[...], sbn_ref[...]).astype(BF16)
        for ci in range(sub // CHUNK):
            rs = slice(ci * CHUNK, (ci + 1) * CHUNK)
            ro = slice(r * sub + ci * CHUNK, r * sub + (ci + 1) * CHUNK)
            for gi in range(SGU_GROUPS):
                cs = slice(gi * LANES, (gi + 1) * LANES)
                mixed = _dot(sw_ref[gi], zn[rs, cs]) + sb_ref[gi]
                s_ref[ro, cs] = (u[rs, cs] * mixed).astype(BF16)

        f = _dot(xm, w_ref[:, IN_F:IN_END]).astype(BF16)
        n_cb = 2 * FNET_WIDTH // LANES
        ab = [None] * n_cb
        for gi in range(FNET_GROUPS):
            cs_ = _dot(f[:, gi * FNET_GROUP_W:(gi + 1) * FNET_GROUP_W], dft_ref[...])
            ab[gi] = cs_[:, :FNET_GROUP_W]
            ab[FNET_GROUPS + gi] = cs_[:, FNET_GROUP_W:]
        if rope:
            n2 = sub // N_RES
            base = r * n2 * STAGE_PITCH
            for g in range(n2):
                for cb in range(n_cb):
                    stage_ref[cb, base + g * STAGE_PITCH:base + g * STAGE_PITCH + N_RES, :] = (
                        ab[cb][g * N_RES:(g + 1) * N_RES, :])
            for j in range(N_RES):
                cols = slice(j * 2 * FNET_WIDTH, (j + 1) * 2 * FNET_WIDTH)
                piece = [stage_ref[cb, pl.ds(base + j, n2, stride=STAGE_PITCH), :] for cb in range(n_cb)]
                ab_ref[r * n2:(r + 1) * n2, cols] = jnp.concatenate(piece, axis=1).astype(BF16)
        else:
            for cb in range(n_cb):
                ab_ref[rows, cb * LANES:(cb + 1) * LANES] = ab[cb].astype(BF16)


def _inproj_call(x, mods, w_in, tables, sw, sb, sg, sbn, dft_c, *, layer, ctx_row, tm, sub):
    bsz, s, _ = x.shape
    tm = _row_tile(s, tm)
    sub = min(sub, tm)
    rope = tables is not None
    row = lambda w: pl.BlockSpec((None, tm, w), lambda bi, t: (bi, t, 0))
    in_specs = [row(D_MODEL), _mods_spec(layer, ctx_row), _const_spec(w_in.shape, (layer,))]
    args = [x, mods, w_in]
    if rope:
        in_specs += [pl.BlockSpec((tm, LANES), lambda bi, t: (t, 0))] * 2
        args += list(tables)
    in_specs += [_const_spec(a.shape, (layer,)) for a in (sw, sb, sg, sbn)] + [_const_spec(dft_c.shape)]
    args += [sw, sb, sg, sbn, dft_c]
    widths = (Q_W, KV_DUP_W, KV_DUP_W, SGU_WIDTH, 2 * FNET_WIDTH)
    out_specs = [row(w) for w in widths]
    out_shape = [jax.ShapeDtypeStruct((bsz, s, w), BF16) for w in widths]
    scratch = []
    if rope:
        ab_w = N_RES * 2 * FNET_WIDTH
        out_specs[-1] = pl.BlockSpec((None, tm // N_RES, ab_w), lambda bi, t: (bi, t, 0))
        out_shape[-1] = jax.ShapeDtypeStruct((bsz, s // N_RES, ab_w), BF16)
        scratch = [pltpu.VMEM((2 * FNET_WIDTH // LANES, tm // N_RES * STAGE_PITCH, LANES), F32)]
    return pl.pallas_call(
        functools.partial(_inproj_body, rope=rope, sub=sub),
        grid=(bsz, s // tm),
        in_specs=in_specs,
        out_specs=out_specs,
        out_shape=out_shape,
        scratch_shapes=scratch,
        compiler_params=_params(2),
        name="inproj",
    )(*args)


def _attn_body(*refs, band, nq, nb, layer):
    if band:
        (sink_ref, q_ref, kp_ref, km_ref, kn_ref, vp_ref, vm_ref, vn_ref, kx_ref, vx_ref,
         o_ref, kw_ref, vw_ref) = refs
    else:
        sink_ref, q_ref, kx_ref, vx_ref, o_ref = refs
    blk = ATT_BLOCK
    n_ctx = kx_ref.shape[0]
    t = pl.program_id(1)
    row = lax.broadcasted_iota(jnp.int32, (blk, blk), 0)
    col = lax.broadcasted_iota(jnp.int32, (blk, blk), 1)
    lo = col < HEAD_DIM
    zero = jnp.zeros((blk, blk), BF16)
    nt_dims = (((1,), (1,)), ((), ()))
    if band:
        kw_ref[0:blk] = kp_ref[...]
        kw_ref[blk:(nq + 1) * blk] = km_ref[...]
        kw_ref[(nq + 1) * blk:(nq + 2) * blk] = kn_ref[...]
        vw_ref[0:blk] = vp_ref[...]
        vw_ref[blk:(nq + 1) * blk] = vm_ref[...]
        vw_ref[(nq + 1) * blk:(nq + 2) * blk] = vn_ref[...]

    def max_key_norm_sq(kh):
        cs = slice(kh * LANES, (kh + 1) * LANES)
        best = None
        for ref in ((kw_ref, kx_ref) if band else (kx_ref,)):
            kf = ref[:, cs].astype(F32)
            sq = jnp.max(jnp.sum(kf * kf, axis=1, keepdims=True), axis=0, keepdims=True)
            best = sq if best is None else jnp.maximum(best, sq)
        return 0.5 * best

    def run(exact):
        den_min = None
        k2 = None if exact else [max_key_norm_sq(kh) for kh in range(N_KV_HEADS)]
        for qi in range(nq):
            rows = slice(qi * blk, (qi + 1) * blk)
            if band:
                gi = t * nq + qi
                prev_ok = jnp.logical_and(col >= row, gi > 0)
                next_ok = jnp.logical_and(col <= row, gi < nb - 1)
            for kh in range(N_KV_HEADS):
                cs = slice(kh * LANES, (kh + 1) * LANES)
                qs = []
                for pr in range(GQA_GROUP // 2):
                    pair = q_ref[rows, (2 * kh + pr) * LANES:(2 * kh + pr + 1) * LANES]
                    qs.append(jnp.where(lo, pair, zero))
                    qs.append(jnp.where(lo, zero, pair))
                qstack = jnp.concatenate(qs, axis=0)
                if not exact:
                    qf = qstack.astype(F32)
                    q2 = jnp.sum(qf * qf, axis=1, keepdims=True)
                sc_ctx = lax.dot_general(qstack, kx_ref[:, cs], nt_dims, preferred_element_type=F32)
                if band:
                    win = slice(qi * blk, (qi + 3) * blk)
                    sc_band = lax.dot_general(qstack, kw_ref[win, cs], nt_dims, preferred_element_type=F32)
                ps, dens = [], []
                for g in range(GQA_GROUP):
                    hr = slice(g * blk, (g + 1) * blk)
                    pieces = [sc_ctx[hr, c * blk:(c + 1) * blk] for c in range(n_ctx // blk)]
                    if band:
                        pieces = [jnp.where(prev_ok, sc_band[hr, 0:blk], NEG_INF),
                                  sc_band[hr, blk:2 * blk],
                                  jnp.where(next_ok, sc_band[hr, 2 * blk:3 * blk], NEG_INF)] + pieces
                    sink = sink_ref[layer, kh * GQA_GROUP + g] * LOG2E
                    if exact:
                        mx = pieces[0]
                        for pc in pieces[1:]:
                            mx = jnp.maximum(mx, pc)
                        m = jnp.maximum(jnp.max(mx, axis=1, keepdims=True), sink)
                    else:
                        q2max = jnp.max(q2[hr], axis=0, keepdims=True)
                        m = jnp.maximum(jnp.sqrt(q2max * k2[kh]) * BOUND_SLACK, sink)
                    es = [jnp.exp2(pc - m) for pc in pieces]
                    tot = es[0]
                    for e in es[1:]:
                        tot = tot + e
                    den = jnp.sum(tot, axis=1, keepdims=True) + jnp.exp2(sink - m)
                    dens.append(den)
                    if not exact:
                        den_min = den if den_min is None else jnp.minimum(den_min, den)
                    ps.append(jnp.concatenate([e.astype(BF16) for e in es], axis=1))
                pall = jnp.concatenate(ps, axis=0)
                if band:
                    o = _dot(pall[:, :3 * blk], vw_ref[win, cs]) + _dot(pall[:, 3 * blk:], vx_ref[:, cs])
                else:
                    o = _dot(pall, vx_ref[:, cs])
                outs = [o[g * blk:(g + 1) * blk] / dens[g] for g in range(GQA_GROUP)]
                for pr in range(GQA_GROUP // 2):
                    pair = jnp.where(lo, outs[2 * pr], outs[2 * pr + 1])
                    o_ref[rows, (2 * kh + pr) * LANES:(2 * kh + pr + 1) * LANES] = pair.astype(BF16)
        return den_min

    den_min = run(exact=False)
    ok = jnp.min(den_min) >= DEN_FLOOR

    @pl.when(jnp.logical_not(ok))
    def _():
        run(exact=True)


def _attn_call(sink, q, k, v, kx, vx, *, layer, band, tq):
    bsz, s, _ = q.shape
    n_ctx = kx.shape[1]
    nb = s // ATT_BLOCK
    tq = _row_tile(s, tq)
    nq = tq // ATT_BLOCK
    qspec = pl.BlockSpec((None, tq, Q_W), lambda bi, i: (bi, i, 0))
    xspec = pl.BlockSpec((None, n_ctx, KV_DUP_W), lambda bi, i: (bi, 0, 0))
    sspec = pl.BlockSpec(memory_space=pltpu.SMEM)
    scratch = []
    if band:
        edge = lambda f: pl.BlockSpec((None, ATT_BLOCK, KV_DUP_W), lambda bi, i: (bi, f(i), 0))
        prev = edge(lambda i: jnp.maximum(i * nq - 1, 0))
        nxt = edge(lambda i: jnp.minimum((i + 1) * nq, nb - 1))
        main = pl.BlockSpec((None, tq, KV_DUP_W), lambda bi, i: (bi, i, 0))
        in_specs = [sspec, qspec] + [prev, main, nxt] * 2 + [xspec, xspec]
        args = (sink, q, k, k, k, v, v, v, kx, vx)
        scratch = [pltpu.VMEM(((nq + 2) * ATT_BLOCK, KV_DUP_W), BF16)] * 2
    else:
        in_specs = [sspec, qspec, xspec, xspec]
        args = (sink, q, kx, vx)
    return pl.pallas_call(
        functools.partial(_attn_body, band=band, nq=nq, nb=nb, layer=layer),
        grid=(bsz, s // tq),
        in_specs=in_specs,
        out_specs=qspec,
        out_shape=jax.ShapeDtypeStruct((bsz, s, Q_W), BF16),
        scratch_shapes=scratch,
        compiler_params=_params(2),
        name="attn_band" if band else "attn_ctx",
    )(*args)


def _fourier_body(cn_ref, nsn_ref, ab_ref, o_ref, *, scale):
    y = _dot(cn_ref[...], ab_ref[:, :FNET_WIDTH]) + _dot(nsn_ref[...], ab_ref[:, FNET_WIDTH:])
    o_ref[...] = (y * scale).astype(BF16)


def _fourier_call(cn, nsn, ab, *, tk):
    bsz, s, _ = ab.shape
    tk = _row_tile(s, tk)
    scale = 1.0 / math.sqrt(s * FNET_GROUP_W)
    return pl.pallas_call(
        functools.partial(_fourier_body, scale=scale),
        grid=(bsz, s // tk),
        in_specs=[
            pl.BlockSpec((tk, s), lambda bi, t: (t, 0)),
            pl.BlockSpec((tk, s), lambda bi, t: (t, 0)),
            pl.BlockSpec((None, s, 2 * FNET_WIDTH), lambda bi, t: (bi, 0, 0)),
        ],
        out_specs=pl.BlockSpec((None, tk, FNET_WIDTH), lambda bi, t: (bi, t, 0)),
        out_shape=jax.ShapeDtypeStruct((bsz, s, FNET_WIDTH), BF16),
        compiler_params=_params(2),
        name="fourier",
    )(cn, nsn, ab)


def _fourier_fact_body(ab_ref, m1_ref, g_ref, o_ref, t_ref, *, n_low, scale):
    w = FNET_WIDTH
    for j in range(N_RES):
        blk = ab_ref[:, j * 2 * w:(j + 1) * 2 * w]
        d = jnp.concatenate([blk[:, :w], blk[:, w:]], axis=0)
        r = _dot(m1_ref[j], d)
        rows = slice(j * n_low, (j + 1) * n_low)
        t_ref[rows, :w] = r[:n_low]
        t_ref[rows, w:] = r[n_low:]

    def stage2(kb):
        x = jnp.concatenate(
            [t_ref[j * n_low + kb * K2_BLK:j * n_low + (kb + 1) * K2_BLK, :] for j in range(N_RES)],
            axis=0).astype(BF16)
        return (_dot(g_ref[0], x[:, :w]) + _dot(g_ref[1], x[:, w:])) * scale

    pair = BF16_ROWS // K2_BLK
    for kb in range(0, n_low // K2_BLK, pair):
        ys = [stage2(kb + p) for p in range(pair)]
        for k1 in range(N_RES):
            piece = jnp.concatenate([y[k1 * K2_BLK:(k1 + 1) * K2_BLK] for y in ys], axis=0)
            o_ref[k1 * n_low + kb * K2_BLK:k1 * n_low + (kb + pair) * K2_BLK, :] = piece.astype(BF16)


def _fourier_fact_call(m1, g, ab_view):
    bsz, n_low, _ = ab_view.shape
    s = n_low * N_RES
    scale = 1.0 / math.sqrt(s * FNET_GROUP_W)
    return pl.pallas_call(
        functools.partial(_fourier_fact_body, n_low=n_low, scale=scale),
        grid=(bsz,),
        in_specs=[
            pl.BlockSpec((None, n_low, N_RES * 2 * FNET_WIDTH), lambda bi: (bi, 0, 0)),
            _const_spec(m1.shape), _const_spec(g.shape),
        ],
        out_specs=pl.BlockSpec((None, s, FNET_WIDTH), lambda bi: (bi, 0, 0)),
        out_shape=jax.ShapeDtypeStruct((bsz, s, FNET_WIDTH), BF16),
        scratch_shapes=[pltpu.VMEM((s, 2 * FNET_WIDTH), F32)],
        compiler_params=_params(1),
        name="fourier_fact",
    )(ab_view, m1, g)


def _merge_body(x_ref, m_ref, a_ref, s_ref, c_ref, wg_ref, wb_ref, wo_ref, g_ref, b_ref, o_ref, mg_ref, *, sub):
    for st in range(x_ref.shape[0] // sub):
        rows = slice(st * sub, (st + 1) * sub)
        x = x_ref[rows, :]
        h = (x * (1 + m_ref[4:5, :]) + m_ref[3:4, :]).astype(BF16)
        branches = (a_ref[rows, :], s_ref[rows, :], c_ref[rows, :])
        for n in range(D_MODEL // MXU_W):
            sl = slice(n * MXU_W, (n + 1) * MXU_W)
            acc = None
            for r in range(N_BRANCH):
                t = jax.nn.sigmoid(_dot(h, wg_ref[r, :, sl])) * _dot(branches[r], wb_ref[r, :, sl])
                acc = t if acc is None else acc + t
            mg_ref[rows, sl] = acc.astype(BF16)
        y = _dot(mg_ref[rows, :], wo_ref[...])
        o_ref[rows, :] = _layer_norm(ALPHA * x + m_ref[5:6, :] * y, g_ref[...], b_ref[...])


def _merge_call(x, mods, a, sgu, c, wg, wb, wo, g, b, *, layer, ctx_row, tm, sub):
    bsz, s, _ = x.shape
    tm = _row_tile(s, tm)
    sub = min(sub, tm)
    row = lambda w: pl.BlockSpec((None, tm, w), lambda bi, t: (bi, t, 0))
    return pl.pallas_call(
        functools.partial(_merge_body, sub=sub),
        grid=(bsz, s // tm),
        in_specs=[
            row(D_MODEL), _mods_spec(layer, ctx_row),
            row(Q_W), row(SGU_WIDTH), row(FNET_WIDTH),
            _const_spec(wg.shape, (layer,)), _const_spec(wb.shape, (layer,)), _const_spec(wo.shape, (layer,)),
            _const_spec(g.shape, (layer, 1)), _const_spec(b.shape, (layer, 1)),
        ],
        out_specs=row(D_MODEL),
        out_shape=jax.ShapeDtypeStruct(x.shape, F32),
        scratch_shapes=[pltpu.VMEM((tm, D_MODEL), BF16)],
        compiler_params=_params(2),
        name="merge",
    )(x, mods, a, sgu, c, wg, wb, wo, g, b)


def _rope_tables(s):
    pos = jnp.arange(s)
    row = (pos // GRID_W).astype(F32)
    col = (pos % GRID_W).astype(F32)
    axis_dim = HEAD_DIM // 2
    inv_freq = ROPE_THETA ** (-jnp.arange(0, axis_dim, 2, dtype=F32) / axis_dim)
    ang_r = row[:, None] * inv_freq[None, :]
    ang_c = col[:, None] * inv_freq[None, :]
    cos = jnp.concatenate([jnp.cos(ang_r)] * 2 + [jnp.cos(ang_c)] * 2, axis=1)
    sin = jnp.concatenate([-jnp.sin(ang_r), jnp.sin(ang_r), -jnp.sin(ang_c), jnp.sin(ang_c)], axis=1)
    return jnp.tile(cos, (1, 2)), jnp.tile(sin, (1, 2))


def _dft_angles(n):
    k = jnp.arange(n, dtype=jnp.int32)
    r = (k[:, None] * k[None, :]) % n
    return r.astype(F32) * (2.0 * math.pi / n)


def _position_dft(n):
    ang = _dft_angles(n)
    return jnp.cos(ang).astype(BF16), (-jnp.sin(ang)).astype(BF16)


def _factored_dft(s):
    n_low = s // N_RES
    idx = jnp.arange(n_low, dtype=jnp.int32)
    n = jnp.arange(N_RES, dtype=jnp.int32)[:, None, None] + N_RES * idx[None, None, :]
    ang = ((idx[None, :, None] * n) % s).astype(F32) * (2.0 * math.pi / s)
    c, sn = jnp.cos(ang), jnp.sin(ang)
    m1 = jnp.concatenate([jnp.concatenate([c, -sn], axis=2), jnp.concatenate([sn, c], axis=2)], axis=1)
    ang_r = _dft_angles(N_RES)
    eye = jnp.eye(K2_BLK, dtype=F32)
    g = jnp.stack([jnp.kron(jnp.cos(ang_r), eye), -jnp.kron(jnp.sin(ang_r), eye)])
    return m1.astype(BF16), g.astype(BF16)


def _channel_dft():
    ang = _dft_angles(FNET_GROUP_W)
    return jnp.concatenate([jnp.cos(ang), jnp.sin(ang)], axis=1).astype(BF16)


def kernel(x, c, ctx, c_ctx, w_mod, b_mod, w_ffn_up, w_ffn_down, ln_g, ln_b, w_in, attn_sink,
           sgu_w, sgu_b, sgu_ln_g, sgu_ln_b, w_gate, w_branch, w_out):
    depth = w_mod.shape[0]
    bsz, s, _ = x.shape
    n_ctx = ctx.shape[1]

    w_gu = _pack_up_call(w_ffn_up)
    w_d = _pack_down_call(w_ffn_down)
    w_in_b = w_in.astype(BF16)
    w_gate_b = w_gate.astype(BF16)
    w_branch_b = w_branch.astype(BF16)
    w_out_b = w_out.astype(BF16)
    sgu_w_b = sgu_w.astype(BF16)
    sgu_b_col = sgu_b[..., None]
    ln_g4 = ln_g[:, :, None, :]
    ln_b4 = ln_b[:, :, None, :]
    sgu_g3 = sgu_ln_g[:, None, :]
    sgu_b3 = sgu_ln_b[:, None, :]

    tables = _rope_tables(s)
    dft_c = _channel_dft()
    m1_lat, g_lat = _factored_dft(s)
    cn_ctx, nsn_ctx = _position_dft(n_ctx)

    rows = -(-(bsz + 1) // 8) * 8
    cc = jnp.zeros((rows, D_MODEL), F32).at[:bsz].set(c).at[bsz].set(c_ctx)
    mods = _mods_call(cc, w_mod, b_mod).reshape(depth, rows, N_MOD, D_MODEL)

    def ffn(xx, i, ctx_row, half, j):
        return _ffn_call(xx, mods, w_gu, w_d, ln_g4, ln_b4, layer=i, ctx_row=ctx_row, half=half, j=j,
                         tm=ROW_TILE, sub=SUB_TILE)

    def inproj(xx, i, ctx_row, tb):
        return _inproj_call(xx, mods, w_in_b, tb, sgu_w_b, sgu_b_col, sgu_g3, sgu_b3, dft_c,
                            layer=i, ctx_row=ctx_row, tm=ROW_TILE, sub=SUB_TILE)

    def merge(xx, i, ctx_row, a, sg, cf):
        return _merge_call(xx, mods, a, sg, cf, w_gate_b, w_branch_b, w_out_b, ln_g4, ln_b4,
                           layer=i, ctx_row=ctx_row, tm=ROW_TILE, sub=SUB_TILE)

    flat = lambda a: a.reshape(1, bsz * n_ctx, a.shape[-1])
    unflat = lambda a: a.reshape(bsz, n_ctx, a.shape[-1])
    x_lat, x_ctx = x, flat(ctx)
    for i in range(depth):
        last = i == depth - 1
        x_lat = ffn(x_lat, i, None, 0, 0)
        x_ctx = ffn(x_ctx, i, bsz, 0, 0)

        q_c, k_c, v_c, s_c, ab_c = inproj(x_ctx, i, bsz, None)
        q_c, k_c, v_c, ab_c = unflat(q_c), unflat(k_c), unflat(v_c), unflat(ab_c)
        q_l, k_l, v_l, s_l, ab_l = inproj(x_lat, i, None, tables)

        a_l = _attn_call(attn_sink, q_l, k_l, v_l, k_c, v_c, layer=i, band=True, tq=ATT_Q_TILE)
        f_l = _fourier_fact_call(m1_lat, g_lat, ab_l)
        x_lat = merge(x_lat, i, None, a_l, s_l, f_l)
        x_lat = ffn(x_lat, i, None, 1, 2)
        if not last:
            a_c = _attn_call(attn_sink, q_c, None, None, k_c, v_c, layer=i, band=False, tq=ATT_Q_TILE)
            f_c = _fourier_call(cn_ctx, nsn_ctx, ab_c, tk=ROW_TILE)
            x_ctx = merge(x_ctx, i, bsz, flat(a_c), s_c, flat(f_c))
            x_ctx = ffn(x_ctx, i, bsz, 1, 2)
    return x_lat
```

```python
import functools
import math

import jax
import jax.numpy as jnp
from jax import lax
from jax.experimental import pallas as pl
from jax.experimental.pallas import tpu as pltpu

F32 = jnp.float32
BF16 = jnp.bfloat16

D_MODEL = 1024
GRID_W = 64
HEAD_DIM = 64
N_Q_HEADS = 8
N_KV_HEADS = 2
GQA_GROUP = N_Q_HEADS // N_KV_HEADS
ATT_BLOCK = 128
ROPE_THETA = 10000.0
Q_W = N_Q_HEADS * HEAD_DIM
KV_W = N_KV_HEADS * HEAD_DIM
CHUNK = 128
SGU_GROUPS = 4
SGU_WIDTH = 512
FNET_GROUPS = 4
FNET_WIDTH = 512
FNET_GROUP_W = FNET_WIDTH // FNET_GROUPS
N_BRANCH = 3
D_FF = 2752
N_MOD = 9
MODEL_DEPTH = 4
ALPHA = (2 * MODEL_DEPTH) ** 0.25
LN_EPS = 1e-5
NEG_INF = -1e30

LANES = 128
MXU_W = 256
D_FF_PAD = -(-D_FF // MXU_W) * MXU_W
UP_OFF = D_FF % LANES
UP0 = D_FF - UP_OFF
assert UP0 + D_FF_PAD == 2 * D_FF
KV_DUP_W = 2 * KV_W
IN_Q, IN_K, IN_U, IN_Z, IN_F, IN_END = 0, 512, 768, 1280, 1792, 2304
VMEM_LIMIT = 56 * 1024 * 1024
ROW_TILE = 1024
SUB_TILE = 512
ATT_Q_TILE = 1024
PACK_ROWS = 256
MODS_COLS = 1024
N_RES = 32
BF16_ROWS = 16
K2_BLK = 8
STAGE_PITCH = 40
DEN_FLOOR = 2.0 ** -100
BOUND_SLACK = 1.01
LOG2E = math.log2(math.e)
Q_SCALE = HEAD_DIM ** -0.5 * LOG2E


def _params(n_axes):
    return pltpu.CompilerParams(dimension_semantics=("parallel",) * n_axes,
                                vmem_limit_bytes=VMEM_LIMIT)


def _const_spec(shape, lead=()):
    nl = len(lead)
    block = (None,) * nl + tuple(shape[nl:])
    index = tuple(lead) + (0,) * (len(shape) - nl)
    return pl.BlockSpec(block, lambda *_: index, pipeline_mode=pl.Buffered(1))


def _mods_spec(layer, ctx_row):
    if ctx_row is None:
        return pl.BlockSpec((None, None, N_MOD, D_MODEL), lambda bi, t: (layer, bi, 0, 0))
    return pl.BlockSpec((None, None, N_MOD, D_MODEL), lambda bi, t: (layer, ctx_row, 0, 0))


def _row_tile(s, pref):
    t = min(s, pref)
    while s % t:
        t -= LANES
    return t


def _dot(a, b):
    return jnp.dot(a, b, preferred_element_type=F32)


def _layer_norm(y, g, b):
    mu = jnp.mean(y, axis=-1, keepdims=True)
    d = y - mu
    var = jnp.mean(d * d, axis=-1, keepdims=True)
    return d * lax.rsqrt(var + LN_EPS) * g + b


def _mods_body(cc_ref, w_ref, b_ref, o_ref):
    s = cc_ref[...]
    s = (s * jax.nn.sigmoid(s)).astype(BF16)
    o_ref[...] = _dot(s, w_ref[...].astype(BF16)) + b_ref[...]


def _mods_call(cc, w_mod, b_mod):
    depth, _, n = w_mod.shape
    rows = cc.shape[0]
    tn = MODS_COLS
    return pl.pallas_call(
        _mods_body,
        grid=(depth, n // tn),
        in_specs=[
            pl.BlockSpec((rows, D_MODEL), lambda l, j: (0, 0)),
            pl.BlockSpec((None, D_MODEL, tn), lambda l, j: (l, 0, j)),
            pl.BlockSpec((None, 1, tn), lambda l, j: (l, 0, j)),
        ],
        out_specs=pl.BlockSpec((None, rows, tn), lambda l, j: (l, 0, j)),
        out_shape=jax.ShapeDtypeStruct((depth, rows, n), F32),
        compiler_params=_params(2),
        name="mods",
    )(cc, w_mod, b_mod.reshape(depth, 1, n))


def _pack_up_body(w_ref, o_ref):
    rows = w_ref.shape[0]
    lane = lax.broadcasted_iota(jnp.int32, (rows, D_FF_PAD), 1)
    real = lane < D_FF
    o_ref[:, :D_FF_PAD] = jnp.where(real, w_ref[:, :D_FF_PAD], 0.0).astype(BF16)
    up = pltpu.roll(w_ref[:, UP0:], D_FF_PAD - UP_OFF, axis=1)
    o_ref[:, D_FF_PAD:] = jnp.where(real, up, 0.0).astype(BF16)


def _pack_up_call(w_up):
    depth, two, d, n = w_up.shape
    tr = PACK_ROWS
    return pl.pallas_call(
        _pack_up_body,
        grid=(depth, two, d // tr),
        in_specs=[pl.BlockSpec((None, None, tr, n), lambda l, h, r: (l, h, r, 0))],
        out_specs=pl.BlockSpec((None, None, tr, 2 * D_FF_PAD), lambda l, h, r: (l, h, r, 0)),
        out_shape=jax.ShapeDtypeStruct((depth, two, d, 2 * D_FF_PAD), BF16),
        compiler_params=_params(3),
        name="pack_up",
    )(w_up)


def _pack_down_body(w_ref, o_ref):
    o_ref[:D_FF, :] = w_ref[...].astype(BF16)
    o_ref[D_FF:, :] = jnp.zeros((D_FF_PAD - D_FF, w_ref.shape[1]), BF16)


def _pack_down_call(w_down):
    depth, two, n, d = w_down.shape
    return pl.pallas_call(
        _pack_down_body,
        grid=(depth, two),
        in_specs=[pl.BlockSpec((None, None, n, d), lambda l, h: (l, h, 0, 0))],
        out_specs=pl.BlockSpec((None, None, D_FF_PAD, d), lambda l, h: (l, h, 0, 0)),
        out_shape=jax.ShapeDtypeStruct((depth, two, D_FF_PAD, d), BF16),
        compiler_params=_params(2),
        name="pack_down",
    )(w_down)


def _ffn_body(*refs, j, sub, n_lat):
    if n_lat is None:
        x_refs, (m_ref, wgu_ref, wd_ref, g_ref, b_ref), o_refs = refs[:1], refs[1:6], refs[6:]
    else:
        x_refs, (m_ref, wgu_ref, wd_ref, g_ref, b_ref), o_refs = refs[:2], refs[2:7], refs[7:]
    shift = m_ref[3 * j:3 * j + 1, :]
    scale = m_ref[3 * j + 1:3 * j + 2, :]
    gate = m_ref[3 * j + 2:3 * j + 3, :]

    def tile(x_ref, o_ref):
        for r in range(x_ref.shape[0] // sub):
            rows = slice(r * sub, (r + 1) * sub)
            x = x_ref[rows, :]
            xm = (x * (1 + scale) + shift).astype(BF16)
            acc = None
            for c in range(D_FF_PAD // MXU_W):
                sl = slice(c * MXU_W, (c + 1) * MXU_W)
                su = slice(D_FF_PAD + c * MXU_W, D_FF_PAD + (c + 1) * MXU_W)
                gg = _dot(xm, wgu_ref[:, sl])
                uu = _dot(xm, wgu_ref[:, su])
                h = (gg * jax.nn.sigmoid(gg) * uu).astype(BF16)
                d = _dot(h, wd_ref[sl, :])
                acc = d if acc is None else acc + d
            y = ALPHA * x + (0.5 * gate) * acc
            o_ref[rows, :] = _layer_norm(y, g_ref[...], b_ref[...])

    if n_lat is None:
        tile(x_refs[0], o_refs[0])
    else:
        t = pl.program_id(0)
        pl.when(t < n_lat)(lambda: tile(x_refs[0], o_refs[0]))
        pl.when(t >= n_lat)(lambda: tile(x_refs[1], o_refs[1]))


def _ffn_call(x_lat, x_ctx, mods, wgu, wd, g, b, *, layer, ctx_row, half, j, tm, sub):
    bsz, s, _ = x_lat.shape
    tm = _row_tile(s, tm)
    if x_ctx is not None:
        tm = _row_tile(x_ctx.shape[1], tm)
        assert s % tm == 0
    sub = min(sub, tm)
    per_b = s // tm
    n_lat = bsz * per_b
    weights = [_const_spec(wgu.shape, (layer, half)), _const_spec(wd.shape, (layer, half)),
               _const_spec(g.shape, (layer, j)), _const_spec(b.shape, (layer, j))]
    if x_ctx is None:
        lat = pl.BlockSpec((None, tm, D_MODEL), lambda t: (t // per_b, t % per_b, 0))
        mspec = pl.BlockSpec((None, None, N_MOD, D_MODEL), lambda t: (layer, t // per_b, 0, 0))
        return pl.pallas_call(
            functools.partial(_ffn_body, j=j, sub=sub, n_lat=None),
            grid=(n_lat,),
            in_specs=[lat, mspec] + weights,
            out_specs=lat,
            out_shape=jax.ShapeDtypeStruct(x_lat.shape, F32),
            compiler_params=_params(1),
            name="ffn",
        )(x_lat, mods, wgu, wd, g, b)
    n_ctx = x_ctx.shape[1] // tm

    def lat_index(t):
        tl = jnp.minimum(t, n_lat - 1)
        return (tl // per_b, tl % per_b, 0)

    lat = pl.BlockSpec((None, tm, D_MODEL), lat_index)
    cspec = pl.BlockSpec((None, tm, D_MODEL), lambda t: (0, jnp.maximum(t - n_lat, 0), 0))
    mspec = pl.BlockSpec((None, None, N_MOD, D_MODEL),
                         lambda t: (layer, jnp.where(t < n_lat, t // per_b, ctx_row), 0, 0))
    return pl.pallas_call(
        functools.partial(_ffn_body, j=j, sub=sub, n_lat=n_lat),
        grid=(n_lat + n_ctx,),
        in_specs=[lat, cspec, mspec] + weights,
        out_specs=[lat, cspec],
        out_shape=[jax.ShapeDtypeStruct(x_lat.shape, F32), jax.ShapeDtypeStruct(x_ctx.shape, F32)],
        compiler_params=pltpu.CompilerParams(dimension_semantics=("arbitrary",), vmem_limit_bytes=VMEM_LIMIT),
        name="ffn",
    )(x_lat, x_ctx, mods, wgu, wd, g, b)


def _gelu(x):
    return jax.nn.gelu(x)


def _inproj_body(*refs, rope, sub):
    if rope:
        (x_ref, m_ref, w_ref, cos_ref, sin_ref, sw_ref, sb_ref, sg_ref, sbn_ref, dft_ref,
         q_ref, k_ref, v_ref, s_ref, ab_ref, stage_ref) = refs
    else:
        (x_ref, m_ref, w_ref, sw_ref, sb_ref, sg_ref, sbn_ref, dft_ref,
         q_ref, k_ref, v_ref, s_ref, ab_ref) = refs
    lane = lax.broadcasted_iota(jnp.int32, (sub, LANES), 1)
    low_half = lane < HEAD_DIM
    first = (lane % 32) < 16

    for r in range(x_ref.shape[0] // sub):
        rows = slice(r * sub, (r + 1) * sub)
        x = x_ref[rows, :]
        xm = (x * (1 + m_ref[4:5, :]) + m_ref[3:4, :]).astype(BF16)

        if rope:
            cos = cos_ref[rows, :]
            sin = sin_ref[rows, :]

            def rot(t, cos=cos, sin=sin):
                partner = jnp.where(first, pltpu.roll(t, LANES - 16, axis=1), pltpu.roll(t, 16, axis=1))
                return t * cos + partner * sin
        else:
            def rot(t):
                return t

        q = _dot(xm, w_ref[:, IN_Q:IN_K])
        for c in range(Q_W // LANES):
            sl = slice(c * LANES, (c + 1) * LANES)
            q_ref[rows, sl] = (rot(q[:, sl]) * Q_SCALE).astype(BF16)
        kv = _dot(xm, w_ref[:, IN_K:IN_U])
        for t, o_ref in ((rot(kv[:, :KV_W]), k_ref), (kv[:, KV_W:], v_ref)):
            swapped = pltpu.roll(t, HEAD_DIM, axis=1)
            o_ref[rows, :LANES] = jnp.where(low_half, t, swapped).astype(BF16)
            o_ref[rows, LANES:] = jnp.where(low_half, swapped, t).astype(BF16)

        u = _gelu(_dot(xm, w_ref[:, IN_U:IN_Z]))
        z = _gelu(_dot(xm, w_ref[:, IN_Z:IN_F]))
        zn = _layer_norm(z, sg_ref[...], sbn_ref[...]).astype(BF16)
        for ci in range(sub // CHUNK):
            rs = slice(ci * CHUNK, (ci + 1) * CHUNK)
            ro = slice(r * sub + ci * CHUNK, r * sub + (ci + 1) * CHUNK)
            for gi in range(SGU_GROUPS):
                cs = slice(gi * LANES, (gi + 1) * LANES)
                mixed = _dot(sw_ref[gi], zn[rs, cs]) + sb_ref[gi]
                s_ref[ro, cs] = (u[rs, cs] * mixed).astype(BF16)

        f = _dot(xm, w_ref[:, IN_F:IN_END]).astype(BF16)
        n_cb = 2 * FNET_WIDTH // LANES
        ab = [None] * n_cb
        for gi in range(FNET_GROUPS):
            cs_ = _dot(f[:, gi * FNET_GROUP_W:(gi + 1) * FNET_GROUP_W], dft_ref[...])
            ab[gi] = cs_[:, :FNET_GROUP_W]
            ab[FNET_GROUPS + gi] = cs_[:, FNET_GROUP_W:]
        if rope:
            n2 = sub // N_RES
            base = r * n2 * STAGE_PITCH
            for g in range(n2):
                for cb in range(n_cb):
                    stage_ref[cb, base + g * STAGE_PITCH:base + g * STAGE_PITCH + N_RES, :] = (
                        ab[cb][g * N_RES:(g + 1) * N_RES, :])
            for j in range(N_RES):
                cols = slice(j * 2 * FNET_WIDTH, (j + 1) * 2 * FNET_WIDTH)
                piece = [stage_ref[cb, pl.ds(base + j, n2, stride=STAGE_PITCH), :] for cb in range(n_cb)]
                ab_ref[r * n2:(r + 1) * n2, cols] = jnp.concatenate(piece, axis=1).astype(BF16)
        else:
            for cb in range(n_cb):
                ab_ref[rows, cb * LANES:(cb + 1) * LANES] = ab[cb].astype(BF16)


def _inproj_call(x, mods, w_in, tables, sw, sb, sg, sbn, dft_c, *, layer, ctx_row, tm, sub):
    bsz, s, _ = x.shape
    tm = _row_tile(s, tm)
    sub = min(sub, tm)
    rope = tables is not None
    row = lambda w: pl.BlockSpec((None, tm, w), lambda bi, t: (bi, t, 0))
    in_specs = [row(D_MODEL), _mods_spec(layer, ctx_row), _const_spec(w_in.shape, (layer,))]
    args = [x, mods, w_in]
    if rope:
        in_specs += [pl.BlockSpec((tm, LANES), lambda bi, t: (t, 0))] * 2
        args += list(tables)
    in_specs += [_const_spec(a.shape, (layer,)) for a in (sw, sb, sg, sbn)] + [_const_spec(dft_c.shape)]
    args += [sw, sb, sg, sbn, dft_c]
    widths = (Q_W, KV_DUP_W, KV_DUP_W, SGU_WIDTH, 2 * FNET_WIDTH)
    out_specs = [row(w) for w in widths]
    out_shape = [jax.ShapeDtypeStruct((bsz, s, w), BF16) for w in widths]
    scratch = []
    if rope:
        ab_w = N_RES * 2 * FNET_WIDTH
        out_specs[-1] = pl.BlockSpec((None, tm // N_RES, ab_w), lambda bi, t: (bi, t, 0))
        out_shape[-1] = jax.ShapeDtypeStruct((bsz, s // N_RES, ab_w), BF16)
        scratch = [pltpu.VMEM((2 * FNET_WIDTH // LANES, tm // N_RES * STAGE_PITCH, LANES), F32)]
    return pl.pallas_call(
        functools.partial(_inproj_body, rope=rope, sub=sub),
        grid=(bsz, s // tm),
        in_specs=in_specs,
        out_specs=out_specs,
        out_shape=out_shape,
        scratch_shapes=scratch,
        compiler_params=_params(2),
        name="inproj",
    )(*args)


def _attn_body(*refs, band, nq, nb, layer):
    if band:
        (sink_ref, q_ref, kp_ref, km_ref, kn_ref, vp_ref, vm_ref, vn_ref, kx_ref, vx_ref,
         o_ref, kw_ref, vw_ref) = refs
    else:
        sink_ref, q_ref, kx_ref, vx_ref, o_ref = refs
    blk = ATT_BLOCK
    n_ctx = kx_ref.shape[0]
    t = pl.program_id(1)
    row = lax.broadcasted_iota(jnp.int32, (blk, blk), 0)
    col = lax.broadcasted_iota(jnp.int32, (blk, blk), 1)
    lo = col < HEAD_DIM
    zero = jnp.zeros((blk, blk), BF16)
    nt_dims = (((1,), (1,)), ((), ()))
    if band:
        kw_ref[0:blk] = kp_ref[...]
        kw_ref[blk:(nq + 1) * blk] = km_ref[...]
        kw_ref[(nq + 1) * blk:(nq + 2) * blk] = kn_ref[...]
        vw_ref[0:blk] = vp_ref[...]
        vw_ref[blk:(nq + 1) * blk] = vm_ref[...]
        vw_ref[(nq + 1) * blk:(nq + 2) * blk] = vn_ref[...]

    def max_key_norm_sq(kh):
        cs = slice(kh * LANES, (kh + 1) * LANES)
        best = None
        for ref in ((kw_ref, kx_ref) if band else (kx_ref,)):
            kf = ref[:, cs].astype(F32)
            sq = jnp.max(jnp.sum(kf * kf, axis=1, keepdims=True), axis=0, keepdims=True)
            best = sq if best is None else jnp.maximum(best, sq)
        return 0.5 * best

    def run(exact):
        den_min = None
        k2 = None if exact else [max_key_norm_sq(kh) for kh in range(N_KV_HEADS)]
        for qi in range(nq):
            rows = slice(qi * blk, (qi + 1) * blk)
            if band:
                gi = t * nq + qi
                prev_ok = jnp.logical_and(col >= row, gi > 0)
                next_ok = jnp.logical_and(col <= row, gi < nb - 1)
            for kh in range(N_KV_HEADS):
                cs = slice(kh * LANES, (kh + 1) * LANES)
                qs = []
                for pr in range(GQA_GROUP // 2):
                    pair = q_ref[rows, (2 * kh + pr) * LANES:(2 * kh + pr + 1) * LANES]
                    qs.append(jnp.where(lo, pair, zero))
                    qs.append(jnp.where(lo, zero, pair))
                qstack = jnp.concatenate(qs, axis=0)
                if not exact:
                    qf = qstack.astype(F32)
                    q2 = jnp.sum(qf * qf, axis=1, keepdims=True)
                sc_ctx = lax.dot_general(qstack, kx_ref[:, cs], nt_dims, preferred_element_type=F32)
                if band:
                    win = slice(qi * blk, (qi + 3) * blk)
                    sc_band = lax.dot_general(qstack, kw_ref[win, cs], nt_dims, preferred_element_type=F32)
                ps, dens = [], []
                for g in range(GQA_GROUP):
                    hr = slice(g * blk, (g + 1) * blk)
                    pieces = [sc_ctx[hr, c * blk:(c + 1) * blk] for c in range(n_ctx // blk)]
                    if band:
                        pieces = [jnp.where(prev_ok, sc_band[hr, 0:blk], NEG_INF),
                                  sc_band[hr, blk:2 * blk],
                                  jnp.where(next_ok, sc_band[hr, 2 * blk:3 * blk], NEG_INF)] + pieces
                    sink = sink_ref[layer, kh * GQA_GROUP + g] * LOG2E
                    if exact:
                        mx = pieces[0]
                        for pc in pieces[1:]:
                            mx = jnp.maximum(mx, pc)
                        m = jnp.maximum(jnp.max(mx, axis=1, keepdims=True), sink)
                    else:
                        q2max = jnp.max(q2[hr], axis=0, keepdims=True)
                        m = jnp.maximum(jnp.sqrt(q2max * k2[kh]) * BOUND_SLACK, sink)
                    es = [jnp.exp2(pc - m) for pc in pieces]
                    tot = es[0]
                    for e in es[1:]:
                        tot = tot + e
                    den = jnp.sum(tot, axis=1, keepdims=True) + jnp.exp2(sink - m)
                    dens.append(den)
                    if not exact:
                        den_min = den if den_min is None else jnp.minimum(den_min, den)
                    ps.append(jnp.concatenate([e.astype(BF16) for e in es], axis=1))
                pall = jnp.concatenate(ps, axis=0)
                if band:
                    o = _dot(pall[:, :3 * blk], vw_ref[win, cs]) + _dot(pall[:, 3 * blk:], vx_ref[:, cs])
                else:
                    o = _dot(pall, vx_ref[:, cs])
                outs = [o[g * blk:(g + 1) * blk] / dens[g] for g in range(GQA_GROUP)]
                for pr in range(GQA_GROUP // 2):
                    pair = jnp.where(lo, outs[2 * pr], outs[2 * pr + 1])
                    o_ref[rows, (2 * kh + pr) * LANES:(2 * kh + pr + 1) * LANES] = pair.astype(BF16)
        return den_min

    den_min = run(exact=False)
    ok = jnp.min(den_min) >= DEN_FLOOR

    @pl.when(jnp.logical_not(ok))
    def _():
        run(exact=True)


def _attn_call(sink, q, k, v, kx, vx, *, layer, band, tq):
    bsz, s, _ = q.shape
    n_ctx = kx.shape[1]
    nb = s // ATT_BLOCK
    tq = _row_tile(s, tq)
    nq = tq // ATT_BLOCK
    qspec = pl.BlockSpec((None, tq, Q_W), lambda bi, i: (bi, i, 0))
    xspec = pl.BlockSpec((None, n_ctx, KV_DUP_W), lambda bi, i: (bi, 0, 0))
    sspec = pl.BlockSpec(memory_space=pltpu.SMEM)
    scratch = []
    if band:
        edge = lambda f: pl.BlockSpec((None, ATT_BLOCK, KV_DUP_W), lambda bi, i: (bi, f(i), 0))
        prev = edge(lambda i: jnp.maximum(i * nq - 1, 0))
        nxt = edge(lambda i: jnp.minimum((i + 1) * nq, nb - 1))
        main = pl.BlockSpec((None, tq, KV_DUP_W), lambda bi, i: (bi, i, 0))
        in_specs = [sspec, qspec] + [prev, main, nxt] * 2 + [xspec, xspec]
        args = (sink, q, k, k, k, v, v, v, kx, vx)
        scratch = [pltpu.VMEM(((nq + 2) * ATT_BLOCK, KV_DUP_W), BF16)] * 2
    else:
        in_specs = [sspec, qspec, xspec, xspec]
        args = (sink, q, kx, vx)
    return pl.pallas_call(
        functools.partial(_attn_body, band=band, nq=nq, nb=nb, layer=layer),
        grid=(bsz, s // tq),
        in_specs=in_specs,
        out_specs=qspec,
        out_shape=jax.ShapeDtypeStruct((bsz, s, Q_W), BF16),
        scratch_shapes=scratch,
        compiler_params=_params(2),
        name="attn_band" if band else "attn_ctx",
    )(*args)


def _fourier_body(cn_ref, nsn_ref, ab_ref, o_ref, *, scale):
    y = _dot(cn_ref[...], ab_ref[:, :FNET_WIDTH]) + _dot(nsn_ref[...], ab_ref[:, FNET_WIDTH:])
    o_ref[...] = (y * scale).astype(BF16)


def _fourier_call(cn, nsn, ab, *, tk):
    bsz, s, _ = ab.shape
    tk = _row_tile(s, tk)
    scale = 1.0 / math.sqrt(s * FNET_GROUP_W)
    return pl.pallas_call(
        functools.partial(_fourier_body, scale=scale),
        grid=(bsz, s // tk),
        in_specs=[
            pl.BlockSpec((tk, s), lambda bi, t: (t, 0)),
            pl.BlockSpec((tk, s), lambda bi, t: (t, 0)),
            pl.BlockSpec((None, s, 2 * FNET_WIDTH), lambda bi, t: (bi, 0, 0)),
        ],
        out_specs=pl.BlockSpec((None, tk, FNET_WIDTH), lambda bi, t: (bi, t, 0)),
        out_shape=jax.ShapeDtypeStruct((bsz, s, FNET_WIDTH), BF16),
        compiler_params=_params(2),
        name="fourier",
    )(cn, nsn, ab)


def _fourier_fact_body(ab_ref, m1_ref, g_ref, o_ref, t_ref, *, n_low, scale):
    w = FNET_WIDTH
    for j in range(N_RES):
        blk = ab_ref[:, j * 2 * w:(j + 1) * 2 * w]
        d = jnp.concatenate([blk[:, :w], blk[:, w:]], axis=0)
        r = _dot(m1_ref[j], d)
        rows = slice(j * n_low, (j + 1) * n_low)
        t_ref[rows, :w] = r[:n_low]
        t_ref[rows, w:] = r[n_low:]

    def stage2(kb):
        x = jnp.concatenate(
            [t_ref[j * n_low + kb * K2_BLK:j * n_low + (kb + 1) * K2_BLK, :] for j in range(N_RES)],
            axis=0).astype(BF16)
        return (_dot(g_ref[0], x[:, :w]) + _dot(g_ref[1], x[:, w:])) * scale

    pair = BF16_ROWS // K2_BLK
    for kb in range(0, n_low // K2_BLK, pair):
        ys = [stage2(kb + p) for p in range(pair)]
        for k1 in range(N_RES):
            piece = jnp.concatenate([y[k1 * K2_BLK:(k1 + 1) * K2_BLK] for y in ys], axis=0)
            o_ref[k1 * n_low + kb * K2_BLK:k1 * n_low + (kb + pair) * K2_BLK, :] = piece.astype(BF16)


def _fourier_fact_call(m1, g, ab_view):
    bsz, n_low, _ = ab_view.shape
    s = n_low * N_RES
    scale = 1.0 / math.sqrt(s * FNET_GROUP_W)
    return pl.pallas_call(
        functools.partial(_fourier_fact_body, n_low=n_low, scale=scale),
        grid=(bsz,),
        in_specs=[
            pl.BlockSpec((None, n_low, N_RES * 2 * FNET_WIDTH), lambda bi: (bi, 0, 0)),
            _const_spec(m1.shape), _const_spec(g.shape),
        ],
        out_specs=pl.BlockSpec((None, s, FNET_WIDTH), lambda bi: (bi, 0, 0)),
        out_shape=jax.ShapeDtypeStruct((bsz, s, FNET_WIDTH), BF16),
        scratch_shapes=[pltpu.VMEM((s, 2 * FNET_WIDTH), F32)],
        compiler_params=_params(1),
        name="fourier_fact",
    )(ab_view, m1, g)


def _merge_body(x_ref, m_ref, a_ref, s_ref, c_ref, wg_ref, wb_ref, wo_ref, g_ref, b_ref, o_ref, mg_ref, *, sub):
    for st in range(x_ref.shape[0] // sub):
        rows = slice(st * sub, (st + 1) * sub)
        x = x_ref[rows, :]
        h = (x * (1 + m_ref[4:5, :]) + m_ref[3:4, :]).astype(BF16)
        branches = (a_ref[rows, :], s_ref[rows, :], c_ref[rows, :])
        for n in range(D_MODEL // MXU_W):
            sl = slice(n * MXU_W, (n + 1) * MXU_W)
            acc = None
            for r in range(N_BRANCH):
                t = jax.nn.sigmoid(_dot(h, wg_ref[r, :, sl])) * _dot(branches[r], wb_ref[r, :, sl])
                acc = t if acc is None else acc + t
            mg_ref[rows, sl] = acc.astype(BF16)
        y = _dot(mg_ref[rows, :], wo_ref[...])
        o_ref[rows, :] = _layer_norm(ALPHA * x + m_ref[5:6, :] * y, g_ref[...], b_ref[...])


def _merge_call(x, mods, a, sgu, c, wg, wb, wo, g, b, *, layer, ctx_row, tm, sub):
    bsz, s, _ = x.shape
    tm = _row_tile(s, tm)
    sub = min(sub, tm)
    row = lambda w: pl.BlockSpec((None, tm, w), lambda bi, t: (bi, t, 0))
    return pl.pallas_call(
        functools.partial(_merge_body, sub=sub),
        grid=(bsz, s // tm),
        in_specs=[
            row(D_MODEL), _mods_spec(layer, ctx_row),
            row(Q_W), row(SGU_WIDTH), row(FNET_WIDTH),
            _const_spec(wg.shape, (layer,)), _const_spec(wb.shape, (layer,)), _const_spec(wo.shape, (layer,)),
            _const_spec(g.shape, (layer, 1)), _const_spec(b.shape, (layer, 1)),
        ],
        out_specs=row(D_MODEL),
        out_shape=jax.ShapeDtypeStruct(x.shape, F32),
        scratch_shapes=[pltpu.VMEM((tm, D_MODEL), BF16)],
        compiler_params=_params(2),
        name="merge",
    )(x, mods, a, sgu, c, wg, wb, wo, g, b)


def _rope_tables(s):
    pos = jnp.arange(s)
    row = (pos // GRID_W).astype(F32)
    col = (pos % GRID_W).astype(F32)
    axis_dim = HEAD_DIM // 2
    inv_freq = ROPE_THETA ** (-jnp.arange(0, axis_dim, 2, dtype=F32) / axis_dim)
    ang_r = row[:, None] * inv_freq[None, :]
    ang_c = col[:, None] * inv_freq[None, :]
    cos = jnp.concatenate([jnp.cos(ang_r)] * 2 + [jnp.cos(ang_c)] * 2, axis=1)
    sin = jnp.concatenate([-jnp.sin(ang_r), jnp.sin(ang_r), -jnp.sin(ang_c), jnp.sin(ang_c)], axis=1)
    return jnp.tile(cos, (1, 2)), jnp.tile(sin, (1, 2))


def _dft_angles(n):
    k = jnp.arange(n, dtype=jnp.int32)
    r = (k[:, None] * k[None, :]) % n
    return r.astype(F32) * (2.0 * math.pi / n)


def _position_dft(n):
    ang = _dft_angles(n)
    return jnp.cos(ang).astype(BF16), (-jnp.sin(ang)).astype(BF16)


def _factored_dft(s):
    n_low = s // N_RES
    idx = jnp.arange(n_low, dtype=jnp.int32)
    n = jnp.arange(N_RES, dtype=jnp.int32)[:, None, None] + N_RES * idx[None, None, :]
    ang = ((idx[None, :, None] * n) % s).astype(F32) * (2.0 * math.pi / s)
    c, sn = jnp.cos(ang), jnp.sin(ang)
    m1 = jnp.concatenate([jnp.concatenate([c, -sn], axis=2), jnp.concatenate([sn, c], axis=2)], axis=1)
    ang_r = _dft_angles(N_RES)
    eye = jnp.eye(K2_BLK, dtype=F32)
    g = jnp.stack([jnp.kron(jnp.cos(ang_r), eye), -jnp.kron(jnp.sin(ang_r), eye)])
    return m1.astype(BF16), g.astype(BF16)


def _channel_dft():
    ang = _dft_angles(FNET_GROUP_W)
    return jnp.concatenate([jnp.cos(ang), jnp.sin(ang)], axis=1).astype(BF16)


def kernel(x, c, ctx, c_ctx, w_mod, b_mod, w_ffn_up, w_ffn_down, ln_g, ln_b, w_in, attn_sink,
           sgu_w, sgu_b, sgu_ln_g, sgu_ln_b, w_gate, w_branch, w_out):
    depth = w_mod.shape[0]
    bsz, s, _ = x.shape
    n_ctx = ctx.shape[1]

    w_gu = _pack_up_call(w_ffn_up)
    w_d = _pack_down_call(w_ffn_down)
    w_in_b = w_in.astype(BF16)
    w_gate_b = w_gate.astype(BF16)
    w_branch_b = w_branch.astype(BF16)
    w_out_b = w_out.astype(BF16)
    sgu_w_b = sgu_w.astype(BF16)
    sgu_b_col = sgu_b[..., None]
    ln_g4 = ln_g[:, :, None, :]
    ln_b4 = ln_b[:, :, None, :]
    sgu_g3 = sgu_ln_g[:, None, :]
    sgu_b3 = sgu_ln_b[:, None, :]

    tables = _rope_tables(s)
    dft_c = _channel_dft()
    m1_lat, g_lat = _factored_dft(s)
    cn_ctx, nsn_ctx = _position_dft(n_ctx)

    rows = -(-(bsz + 1) // 8) * 8
    cc = jnp.zeros((rows, D_MODEL), F32).at[:bsz].set(c).at[bsz].set(c_ctx)
    mods = _mods_call(cc, w_mod, b_mod).reshape(depth, rows, N_MOD, D_MODEL)

    def ffn(xl, xc, i, half, j):
        return _ffn_call(xl, xc, mods, w_gu, w_d, ln_g4, ln_b4, layer=i, ctx_row=bsz, half=half, j=j,
                         tm=ROW_TILE, sub=SUB_TILE)

    def inproj(xx, i, ctx_row, tb):
        return _inproj_call(xx, mods, w_in_b, tb, sgu_w_b, sgu_b_col, sgu_g3, sgu_b3, dft_c,
                            layer=i, ctx_row=ctx_row, tm=ROW_TILE, sub=SUB_TILE)

    def merge(xx, i, ctx_row, a, sg, cf):
        return _merge_call(xx, mods, a, sg, cf, w_gate_b, w_branch_b, w_out_b, ln_g4, ln_b4,
                           layer=i, ctx_row=ctx_row, tm=ROW_TILE, sub=SUB_TILE)

    flat = lambda a: a.reshape(1, bsz * n_ctx, a.shape[-1])
    unflat = lambda a: a.reshape(bsz, n_ctx, a.shape[-1])
    x_lat, x_ctx = x, flat(ctx)
    for i in range(depth):
        last = i == depth - 1
        x_lat, x_ctx = ffn(x_lat, x_ctx, i, 0, 0)

        q_c, k_c, v_c, s_c, ab_c = inproj(x_ctx, i, bsz, None)
        q_c, k_c, v_c, ab_c = unflat(q_c), unflat(k_c), unflat(v_c), unflat(ab_c)
        q_l, k_l, v_l, s_l, ab_l = inproj(x_lat, i, None, tables)

        a_l = _attn_call(attn_sink, q_l, k_l, v_l, k_c, v_c, layer=i, band=True, tq=ATT_Q_TILE)
        f_l = _fourier_fact_call(m1_lat, g_lat, ab_l)
        x_lat = merge(x_lat, i, None, a_l, s_l, f_l)
        if last:
            x_lat = ffn(x_lat, None, i, 1, 2)
        else:
            a_c = _attn_call(attn_sink, q_c, None, None, k_c, v_c, layer=i, band=False, tq=ATT_Q_TILE)
            f_c = _fourier_call(cn_ctx, nsn_ctx, ab_c, tk=ROW_TILE)
            x_ctx = merge(x_ctx, i, bsz, flat(a_c), s_c, flat(f_c))
            x_lat, x_ctx = ffn(x_lat, x_ctx, i, 1, 2)
    return x_lat
```

```python
import functools
import math

import jax
import jax.numpy as jnp
from jax import lax
from jax.experimental import pallas as pl
from jax.experimental.pallas import tpu as pltpu

F32 = jnp.float32
BF16 = jnp.bfloat16

D_MODEL = 1024
GRID_W = 64
HEAD_DIM = 64
N_Q_HEADS = 8
N_KV_HEADS = 2
GQA_GROUP = N_Q_HEADS // N_KV_HEADS
ATT_BLOCK = 128
ROPE_THETA = 10000.0
Q_W = N_Q_HEADS * HEAD_DIM
KV_W = N_KV_HEADS * HEAD_DIM
CHUNK = 128
SGU_GROUPS = 4
SGU_WIDTH = 512
FNET_GROUPS = 4
FNET_WIDTH = 512
FNET_GROUP_W = FNET_WIDTH // FNET_GROUPS
N_BRANCH = 3
D_FF = 2752
N_MOD = 9
MODEL_DEPTH = 4
ALPHA = (2 * MODEL_DEPTH) ** 0.25
LN_EPS = 1e-5
NEG_INF = -1e30

LANES = 128
MXU_W = 256
D_FF_PAD = -(-D_FF // MXU_W) * MXU_W
UP_OFF = D_FF % LANES
UP0 = D_FF - UP_OFF
assert UP0 + D_FF_PAD == 2 * D_FF
KV_DUP_W = 2 * KV_W
IN_Q, IN_K, IN_U, IN_Z, IN_F, IN_END = 0, 512, 768, 1280, 1792, 2304
VMEM_LIMIT = 56 * 1024 * 1024
ROW_TILE = 1024
SUB_TILE = 512
ATT_Q_TILE = 1024
PACK_ROWS = 256
MODS_COLS = 1024
N_RES = 32
BF16_ROWS = 16
K2_BLK = 8
STAGE_PITCH = 40
DEN_FLOOR = 2.0 ** -100
BOUND_SLACK = 1.01
LOG2E = math.log2(math.e)
Q_SCALE = HEAD_DIM ** -0.5 * LOG2E


def _params(n_axes):
    return pltpu.CompilerParams(dimension_semantics=("parallel",) * n_axes,
                                vmem_limit_bytes=VMEM_LIMIT)


def _const_spec(shape, lead=()):
    nl = len(lead)
    block = (None,) * nl + tuple(shape[nl:])
    index = tuple(lead) + (0,) * (len(shape) - nl)
    return pl.BlockSpec(block, lambda *_: index, pipeline_mode=pl.Buffered(1))


def _mods_spec(layer, ctx_row):
    if ctx_row is None:
        return pl.BlockSpec((None, None, N_MOD, D_MODEL), lambda bi, t: (layer, bi, 0, 0))
    return pl.BlockSpec((None, None, N_MOD, D_MODEL), lambda bi, t: (layer, ctx_row, 0, 0))


def _row_tile(s, pref):
    t = min(s, pref)
    while s % t:
        t -= LANES
    return t


def _dot(a, b):
    return jnp.dot(a, b, preferred_element_type=F32)


def _layer_norm(y, g, b):
    mu = jnp.mean(y, axis=-1, keepdims=True)
    d = y - mu
    var = jnp.mean(d * d, axis=-1, keepdims=True)
    return d * lax.rsqrt(var + LN_EPS) * g + b


def _mods_body(cc_ref, w_ref, b_ref, o_ref):
    s = cc_ref[...]
    s = (s * jax.nn.sigmoid(s)).astype(BF16)
    o_ref[...] = _dot(s, w_ref[...].astype(BF16)) + b_ref[...]


def _mods_call(cc, w_mod, b_mod):
    depth, _, n = w_mod.shape
    rows = cc.shape[0]
    tn = MODS_COLS
    return pl.pallas_call(
        _mods_body,
        grid=(depth, n // tn),
        in_specs=[
            pl.BlockSpec((rows, D_MODEL), lambda l, j: (0, 0)),
            pl.BlockSpec((None, D_MODEL, tn), lambda l, j: (l, 0, j)),
            pl.BlockSpec((None, 1, tn), lambda l, j: (l, 0, j)),
        ],
        out_specs=pl.BlockSpec((None, rows, tn), lambda l, j: (l, 0, j)),
        out_shape=jax.ShapeDtypeStruct((depth, rows, n), F32),
        compiler_params=_params(2),
        name="mods",
    )(cc, w_mod, b_mod.reshape(depth, 1, n))


def _pack_up_body(w_ref, o_ref):
    rows = w_ref.shape[0]
    lane = lax.broadcasted_iota(jnp.int32, (rows, D_FF_PAD), 1)
    real = lane < D_FF
    o_ref[:, :D_FF_PAD] = jnp.where(real, w_ref[:, :D_FF_PAD], 0.0).astype(BF16)
    up = pltpu.roll(w_ref[:, UP0:], D_FF_PAD - UP_OFF, axis=1)
    o_ref[:, D_FF_PAD:] = jnp.where(real, up, 0.0).astype(BF16)


def _pack_up_call(w_up):
    depth, two, d, n = w_up.shape
    tr = PACK_ROWS
    return pl.pallas_call(
        _pack_up_body,
        grid=(depth, two, d // tr),
        in_specs=[pl.BlockSpec((None, None, tr, n), lambda l, h, r: (l, h, r, 0))],
        out_specs=pl.BlockSpec((None, None, tr, 2 * D_FF_PAD), lambda l, h, r: (l, h, r, 0)),
        out_shape=jax.ShapeDtypeStruct((depth, two, d, 2 * D_FF_PAD), BF16),
        compiler_params=_params(3),
        name="pack_up",
    )(w_up)


def _pack_down_body(w_ref, o_ref):
    o_ref[:D_FF, :] = w_ref[...].astype(BF16)
    o_ref[D_FF:, :] = jnp.zeros((D_FF_PAD - D_FF, w_ref.shape[1]), BF16)


def _pack_down_call(w_down):
    depth, two, n, d = w_down.shape
    return pl.pallas_call(
        _pack_down_body,
        grid=(depth, two),
        in_specs=[pl.BlockSpec((None, None, n, d), lambda l, h: (l, h, 0, 0))],
        out_specs=pl.BlockSpec((None, None, D_FF_PAD, d), lambda l, h: (l, h, 0, 0)),
        out_shape=jax.ShapeDtypeStruct((depth, two, D_FF_PAD, d), BF16),
        compiler_params=_params(2),
        name="pack_down",
    )(w_down)


def _ffn_body(x_ref, m_ref, wgu_ref, wd_ref, g_ref, b_ref, o_ref, *, j, sub):
    shift = m_ref[3 * j:3 * j + 1, :]
    scale = m_ref[3 * j + 1:3 * j + 2, :]
    gate = m_ref[3 * j + 2:3 * j + 3, :]
    for r in range(x_ref.shape[0] // sub):
        rows = slice(r * sub, (r + 1) * sub)
        x = x_ref[rows, :]
        xm = (x * (1 + scale) + shift).astype(BF16)
        acc = None
        for c in range(D_FF_PAD // MXU_W):
            sl = slice(c * MXU_W, (c + 1) * MXU_W)
            su = slice(D_FF_PAD + c * MXU_W, D_FF_PAD + (c + 1) * MXU_W)
            gg = _dot(xm, wgu_ref[:, sl])
            uu = _dot(xm, wgu_ref[:, su])
            h = (gg * jax.nn.sigmoid(gg) * uu).astype(BF16)
            d = _dot(h, wd_ref[sl, :])
            acc = d if acc is None else acc + d
        y = ALPHA * x + (0.5 * gate) * acc
        o_ref[rows, :] = _layer_norm(y, g_ref[...], b_ref[...])


def _ffn_call(x, mods, wgu, wd, g, b, *, layer, ctx_row, half, j, tm, sub):
    bsz, s, _ = x.shape
    tm = _row_tile(s, tm)
    sub = min(sub, tm)
    return pl.pallas_call(
        functools.partial(_ffn_body, j=j, sub=sub),
        grid=(bsz, s // tm),
        in_specs=[
            pl.BlockSpec((None, tm, D_MODEL), lambda bi, t: (bi, t, 0)),
            _mods_spec(layer, ctx_row),
            _const_spec(wgu.shape, (layer, half)), _const_spec(wd.shape, (layer, half)),
            _const_spec(g.shape, (layer, j)), _const_spec(b.shape, (layer, j)),
        ],
        out_specs=pl.BlockSpec((None, tm, D_MODEL), lambda bi, t: (bi, t, 0)),
        out_shape=jax.ShapeDtypeStruct(x.shape, F32),
        compiler_params=_params(2),
        name="ffn",
    )(x, mods, wgu, wd, g, b)


def _gelu(x):
    return jax.nn.gelu(x)


def _inproj_body(*refs, rope, sub):
    if rope:
        (x_ref, m_ref, w_ref, cos_ref, sin_ref, sw_ref, sb_ref, sg_ref, sbn_ref, dft_ref,
         q_ref, k_ref, v_ref, s_ref, ab_ref, stage_ref) = refs
    else:
        (x_ref, m_ref, w_ref, sw_ref, sb_ref, sg_ref, sbn_ref, dft_ref,
         q_ref, k_ref, v_ref, s_ref, ab_ref) = refs
    lane = lax.broadcasted_iota(jnp.int32, (sub, LANES), 1)
    low_half = lane < HEAD_DIM
    first = (lane % 32) < 16

    for r in range(x_ref.shape[0] // sub):
        rows = slice(r * sub, (r + 1) * sub)
        x = x_ref[rows, :]
        xm = (x * (1 + m_ref[4:5, :]) + m_ref[3:4, :]).astype(BF16)

        if rope:
            cos = cos_ref[rows, :]
            sin = sin_ref[rows, :]

            def rot(t, cos=cos, sin=sin):
                partner = jnp.where(first, pltpu.roll(t, LANES - 16, axis=1), pltpu.roll(t, 16, axis=1))
                return t * cos + partner * sin
        else:
            def rot(t):
                return t

        q = _dot(xm, w_ref[:, IN_Q:IN_K])
        for c in range(Q_W // LANES):
            sl = slice(c * LANES, (c + 1) * LANES)
            q_ref[rows, sl] = (rot(q[:, sl]) * Q_SCALE).astype(BF16)
        kv = _dot(xm, w_ref[:, IN_K:IN_U])
        for t, o_ref in ((rot(kv[:, :KV_W]), k_ref), (kv[:, KV_W:], v_ref)):
            swapped = pltpu.roll(t, HEAD_DIM, axis=1)
            o_ref[rows, :LANES] = jnp.where(low_half, t, swapped).astype(BF16)
            o_ref[rows, LANES:] = jnp.where(low_half, swapped, t).astype(BF16)

        u = _gelu(_dot(xm, w_ref[:, IN_U:IN_Z]))
        z = _gelu(_dot(xm, w_ref[:, IN_Z:IN_F]))
        zn = _layer_norm(z, sg_ref[...], sbn_ref[...]).astype(BF16)
        for ci in range(sub // CHUNK):
            rs = slice(ci * CHUNK, (ci + 1) * CHUNK)
            ro = slice(r * sub + ci * CHUNK, r * sub + (ci + 1) * CHUNK)
            for gi in range(SGU_GROUPS):
                cs = slice(gi * LANES, (gi + 1) * LANES)
                mixed = _dot(sw_ref[gi], zn[rs, cs]) + sb_ref[gi]
                s_ref[ro, cs] = (u[rs, cs] * mixed).astype(BF16)

        f = _dot(xm, w_ref[:, IN_F:IN_END]).astype(BF16)
        n_cb = 2 * FNET_WIDTH // LANES
        ab = [None] * n_cb
        for gi in range(FNET_GROUPS):
            cs_ = _dot(f[:, gi * FNET_GROUP_W:(gi + 1) * FNET_GROUP_W], dft_ref[...])
            ab[gi] = cs_[:, :FNET_GROUP_W]
            ab[FNET_GROUPS + gi] = cs_[:, FNET_GROUP_W:]
        if rope:
            n2 = sub // N_RES
            base = r * n2 * STAGE_PITCH
            for g in range(n2):
                for cb in range(n_cb):
                    stage_ref[cb, base + g * STAGE_PITCH:base + g * STAGE_PITCH + N_RES, :] = (
                        ab[cb][g * N_RES:(g + 1) * N_RES, :])
            for j in range(N_RES):
                cols = slice(j * 2 * FNET_WIDTH, (j + 1) * 2 * FNET_WIDTH)
                piece = [stage_ref[cb, pl.ds(base + j, n2, stride=STAGE_PITCH), :] for cb in range(n_cb)]
                ab_ref[r * n2:(r + 1) * n2, cols] = jnp.concatenate(piece, axis=1).astype(BF16)
        else:
            for cb in range(n_cb):
                ab_ref[rows, cb * LANES:(cb + 1) * LANES] = ab[cb].astype(BF16)


def _inproj_call(x, mods, w_in, tables, sw, sb, sg, sbn, dft_c, *, layer, ctx_row, tm, sub):
    bsz, s, _ = x.shape
    tm = _row_tile(s, tm)
    sub = min(sub, tm)
    rope = tables is not None
    row = lambda w: pl.BlockSpec((None, tm, w), lambda bi, t: (bi, t, 0))
    in_specs = [row(D_MODEL), _mods_spec(layer, ctx_row), _const_spec(w_in.shape, (layer,))]
    args = [x, mods, w_in]
    if rope:
        in_specs += [pl.BlockSpec((tm, LANES), lambda bi, t: (t, 0))] * 2
        args += list(tables)
    in_specs += [_const_spec(a.shape, (layer,)) for a in (sw, sb, sg, sbn)] + [_const_spec(dft_c.shape)]
    args += [sw, sb, sg, sbn, dft_c]
    widths = (Q_W, KV_DUP_W, KV_DUP_W, SGU_WIDTH, 2 * FNET_WIDTH)
    out_specs = [row(w) for w in widths]
    out_shape = [jax.ShapeDtypeStruct((bsz, s, w), BF16) for w in widths]
    scratch = []
    if rope:
        ab_w = N_RES * 2 * FNET_WIDTH
        out_specs[-1] = pl.BlockSpec((None, tm // N_RES, ab_w), lambda bi, t: (bi, t, 0))
        out_shape[-1] = jax.ShapeDtypeStruct((bsz, s // N_RES, ab_w), BF16)
        scratch = [pltpu.VMEM((2 * FNET_WIDTH // LANES, tm // N_RES * STAGE_PITCH, LANES), F32)]
    return pl.pallas_call(
        functools.partial(_inproj_body, rope=rope, sub=sub),
        grid=(bsz, s // tm),
        in_specs=in_specs,
        out_specs=out_specs,
        out_shape=out_shape,
        scratch_shapes=scratch,
        compiler_params=_params(2),
        name="inproj",
    )(*args)


def _attn_body(*refs, band, nq, nb, layer):
    if band:
        (sink_ref, q_ref, kp_ref, km_ref, kn_ref, vp_ref, vm_ref, vn_ref, kx_ref, vx_ref,
         o_ref, kw_ref, vw_ref) = refs
    else:
        sink_ref, q_ref, kx_ref, vx_ref, o_ref = refs
    blk = ATT_BLOCK
    n_ctx = kx_ref.shape[0]
    t = pl.program_id(1)
    row = lax.broadcasted_iota(jnp.int32, (blk, blk), 0)
    col = lax.broadcasted_iota(jnp.int32, (blk, blk), 1)
    lo = col < HEAD_DIM
    zero = jnp.zeros((blk, blk), BF16)
    nt_dims = (((1,), (1,)), ((), ()))
    if band:
        kw_ref[0:blk] = kp_ref[...]
        kw_ref[blk:(nq + 1) * blk] = km_ref[...]
        kw_ref[(nq + 1) * blk:(nq + 2) * blk] = kn_ref[...]
        vw_ref[0:blk] = vp_ref[...]
        vw_ref[blk:(nq + 1) * blk] = vm_ref[...]
        vw_ref[(nq + 1) * blk:(nq + 2) * blk] = vn_ref[...]

    def max_key_norm_sq(kh):
        cs = slice(kh * LANES, (kh + 1) * LANES)
        best = None
        for ref in ((kw_ref, kx_ref) if band else (kx_ref,)):
            kf = ref[:, cs].astype(F32)
            sq = jnp.max(jnp.sum(kf * kf, axis=1, keepdims=True), axis=0, keepdims=True)
            best = sq if best is None else jnp.maximum(best, sq)
        return 0.5 * best

    def run(exact):
        den_min = None
        k2 = None if exact else [max_key_norm_sq(kh) for kh in range(N_KV_HEADS)]
        for qi in range(nq):
            rows = slice(qi * blk, (qi + 1) * blk)
            if band:
                gi = t * nq + qi
                prev_ok = jnp.logical_and(col >= row, gi > 0)
                next_ok = jnp.logical_and(col <= row, gi < nb - 1)
            for kh in range(N_KV_HEADS):
                cs = slice(kh * LANES, (kh + 1) * LANES)
                qs = []
                for pr in range(GQA_GROUP // 2):
                    pair = q_ref[rows, (2 * kh + pr) * LANES:(2 * kh + pr + 1) * LANES]
                    qs.append(jnp.where(lo, pair, zero))
                    qs.append(jnp.where(lo, zero, pair))
                qstack = jnp.concatenate(qs, axis=0)
                if not exact:
                    qf = qstack.astype(F32)
                    q2 = jnp.sum(qf * qf, axis=1, keepdims=True)
                sc_ctx = lax.dot_general(qstack, kx_ref[:, cs], nt_dims, preferred_element_type=F32)
                if band:
                    win = slice(qi * blk, (qi + 3) * blk)
                    sc_band = lax.dot_general(qstack, kw_ref[win, cs], nt_dims, preferred_element_type=F32)
                ps, dens = [], []
                for g in range(GQA_GROUP):
                    hr = slice(g * blk, (g + 1) * blk)
                    pieces = [sc_ctx[hr, c * blk:(c + 1) * blk] for c in range(n_ctx // blk)]
                    if band:
                        pieces = [jnp.where(prev_ok, sc_band[hr, 0:blk], NEG_INF),
                                  sc_band[hr, blk:2 * blk],
                                  jnp.where(next_ok, sc_band[hr, 2 * blk:3 * blk], NEG_INF)] + pieces
                    sink = sink_ref[layer, kh * GQA_GROUP + g] * LOG2E
                    if exact:
                        mx = pieces[0]
                        for pc in pieces[1:]:
                            mx = jnp.maximum(mx, pc)
                        m = jnp.maximum(jnp.max(mx, axis=1, keepdims=True), sink)
                    else:
                        q2max = jnp.max(q2[hr], axis=0, keepdims=True)
                        m = jnp.maximum(jnp.sqrt(q2max * k2[kh]) * BOUND_SLACK, sink)
                    es = [jnp.exp2(pc - m) for pc in pieces]
                    tot = es[0]
                    for e in es[1:]:
                        tot = tot + e
                    den = jnp.sum(tot, axis=1, keepdims=True) + jnp.exp2(sink - m)
                    dens.append(den)
                    if not exact:
                        den_min = den if den_min is None else jnp.minimum(den_min, den)
                    ps.append(jnp.concatenate([e.astype(BF16) for e in es], axis=1))
                pall = jnp.concatenate(ps, axis=0)
                if band:
                    o = _dot(pall[:, :3 * blk], vw_ref[win, cs]) + _dot(pall[:, 3 * blk:], vx_ref[:, cs])
                else:
                    o = _dot(pall, vx_ref[:, cs])
                outs = [o[g * blk:(g + 1) * blk] / dens[g] for g in range(GQA_GROUP)]
                for pr in range(GQA_GROUP // 2):
                    pair = jnp.where(lo, outs[2 * pr], outs[2 * pr + 1])
                    o_ref[rows, (2 * kh + pr) * LANES:(2 * kh + pr + 1) * LANES] = pair.astype(BF16)
        return den_min

    den_min = run(exact=False)
    ok = jnp.min(den_min) >= DEN_FLOOR

    @pl.when(jnp.logical_not(ok))
    def _():
        run(exact=True)


def _attn_call(sink, q, k, v, kx, vx, *, layer, band, tq):
    bsz, s, _ = q.shape
    n_ctx = kx.shape[1]
    nb = s // ATT_BLOCK
    tq = _row_tile(s, tq)
    nq = tq // ATT_BLOCK
    qspec = pl.BlockSpec((None, tq, Q_W), lambda bi, i: (bi, i, 0))
    xspec = pl.BlockSpec((None, n_ctx, KV_DUP_W), lambda bi, i: (bi, 0, 0))
    sspec = pl.BlockSpec(memory_space=pltpu.SMEM)
    scratch = []
    if band:
        edge = lambda f: pl.BlockSpec((None, ATT_BLOCK, KV_DUP_W), lambda bi, i: (bi, f(i), 0))
        prev = edge(lambda i: jnp.maximum(i * nq - 1, 0))
        nxt = edge(lambda i: jnp.minimum((i + 1) * nq, nb - 1))
        main = pl.BlockSpec((None, tq, KV_DUP_W), lambda bi, i: (bi, i, 0))
        in_specs = [sspec, qspec] + [prev, main, nxt] * 2 + [xspec, xspec]
        args = (sink, q, k, k, k, v, v, v, kx, vx)
        scratch = [pltpu.VMEM(((nq + 2) * ATT_BLOCK, KV_DUP_W), BF16)] * 2
    else:
        in_specs = [sspec, qspec, xspec, xspec]
        args = (sink, q, kx, vx)
    return pl.pallas_call(
        functools.partial(_attn_body, band=band, nq=nq, nb=nb, layer=layer),
        grid=(bsz, s // tq),
        in_specs=in_specs,
        out_specs=qspec,
        out_shape=jax.ShapeDtypeStruct((bsz, s, Q_W), BF16),
        scratch_shapes=scratch,
        compiler_params=_params(2),
        name="attn_band" if band else "attn_ctx",
    )(*args)


def _fourier_body(cn_ref, nsn_ref, ab_ref, o_ref, *, scale):
    y = _dot(cn_ref[...], ab_ref[:, :FNET_WIDTH]) + _dot(nsn_ref[...], ab_ref[:, FNET_WIDTH:])
    o_ref[...] = (y * scale).astype(BF16)


def _fourier_call(cn, nsn, ab, *, tk):
    bsz, s, _ = ab.shape
    tk = _row_tile(s, tk)
    scale = 1.0 / math.sqrt(s * FNET_GROUP_W)
    return pl.pallas_call(
        functools.partial(_fourier_body, scale=scale),
        grid=(bsz, s // tk),
        in_specs=[
            pl.BlockSpec((tk, s), lambda bi, t: (t, 0)),
            pl.BlockSpec((tk, s), lambda bi, t: (t, 0)),
            pl.BlockSpec((None, s, 2 * FNET_WIDTH), lambda bi, t: (bi, 0, 0)),
        ],
        out_specs=pl.BlockSpec((None, tk, FNET_WIDTH), lambda bi, t: (bi, t, 0)),
        out_shape=jax.ShapeDtypeStruct((bsz, s, FNET_WIDTH), BF16),
        compiler_params=_params(2),
        name="fourier",
    )(cn, nsn, ab)


def _fourier_fact_body(ab_ref, m1_ref, g_ref, o_ref, t_ref, *, n_low, scale):
    w = FNET_WIDTH
    for j in range(N_RES):
        blk = ab_ref[:, j * 2 * w:(j + 1) * 2 * w]
        d = jnp.concatenate([blk[:, :w], blk[:, w:]], axis=0)
        r = _dot(m1_ref[j], d)
        rows = slice(j * n_low, (j + 1) * n_low)
        t_ref[rows, :w] = r[:n_low]
        t_ref[rows, w:] = r[n_low:]

    def stage2(kb):
        x = jnp.concatenate(
            [t_ref[j * n_low + kb * K2_BLK:j * n_low + (kb + 1) * K2_BLK, :] for j in range(N_RES)],
            axis=0).astype(BF16)
        return (_dot(g_ref[0], x[:, :w]) + _dot(g_ref[1], x[:, w:])) * scale

    pair = BF16_ROWS // K2_BLK
    for kb in range(0, n_low // K2_BLK, pair):
        ys = [stage2(kb + p) for p in range(pair)]
        for k1 in range(N_RES):
            piece = jnp.concatenate([y[k1 * K2_BLK:(k1 + 1) * K2_BLK] for y in ys], axis=0)
            o_ref[k1 * n_low + kb * K2_BLK:k1 * n_low + (kb + pair) * K2_BLK, :] = piece.astype(BF16)


def _fourier_fact_call(m1, g, ab_view):
    bsz, n_low, _ = ab_view.shape
    s = n_low * N_RES
    scale = 1.0 / math.sqrt(s * FNET_GROUP_W)
    return pl.pallas_call(
        functools.partial(_fourier_fact_body, n_low=n_low, scale=scale),
        grid=(bsz,),
        in_specs=[
            pl.BlockSpec((None, n_low, N_RES * 2 * FNET_WIDTH), lambda bi: (bi, 0, 0)),
            _const_spec(m1.shape), _const_spec(g.shape),
        ],
        out_specs=pl.BlockSpec((None, s, FNET_WIDTH), lambda bi: (bi, 0, 0)),
        out_shape=jax.ShapeDtypeStruct((bsz, s, FNET_WIDTH), BF16),
        scratch_shapes=[pltpu.VMEM((s, 2 * FNET_WIDTH), F32)],
        compiler_params=_params(1),
        name="fourier_fact",
    )(ab_view, m1, g)


def _merge_body(x_ref, m_ref, a_ref, s_ref, c_ref, wg_ref, wb_ref, wo_ref, g_ref, b_ref, o_ref, mg_ref, *, sub):
    for st in range(x_ref.shape[0] // sub):
        rows = slice(st * sub, (st + 1) * sub)
        x = x_ref[rows, :]
        h = (x * (1 + m_ref[4:5, :]) + m_ref[3:4, :]).astype(BF16)
        branches = (a_ref[rows, :], s_ref[rows, :], c_ref[rows, :])
        for n in range(D_MODEL // MXU_W):
            sl = slice(n * MXU_W, (n + 1) * MXU_W)
            acc = None
            for r in range(N_BRANCH):
                t = jax.nn.sigmoid(_dot(h, wg_ref[r, :, sl])) * _dot(branches[r], wb_ref[r, :, sl])
                acc = t if acc is None else acc + t
            mg_ref[rows, sl] = acc.astype(BF16)
        y = _dot(mg_ref[rows, :], wo_ref[...])
        o_ref[rows, :] = _layer_norm(ALPHA * x + m_ref[5:6, :] * y, g_ref[...], b_ref[...])


def _merge_call(x, mods, a, sgu, c, wg, wb, wo, g, b, *, layer, ctx_row, tm, sub):
    bsz, s, _ = x.shape
    tm = _row_tile(s, tm)
    sub = min(sub, tm)
    row = lambda w: pl.BlockSpec((None, tm, w), lambda bi, t: (bi, t, 0))
    return pl.pallas_call(
        functools.partial(_merge_body, sub=sub),
        grid=(bsz, s // tm),
        in_specs=[
            row(D_MODEL), _mods_spec(layer, ctx_row),
            row(Q_W), row(SGU_WIDTH), row(FNET_WIDTH),
            _const_spec(wg.shape, (layer,)), _const_spec(wb.shape, (layer,)), _const_spec(wo.shape, (layer,)),
            _const_spec(g.shape, (layer, 1)), _const_spec(b.shape, (layer, 1)),
        ],
        out_specs=row(D_MODEL),
        out_shape=jax.ShapeDtypeStruct(x.shape, F32),
        scratch_shapes=[pltpu.VMEM((tm, D_MODEL), BF16)],
        compiler_params=_params(2),
        name="merge",
    )(x, mods, a, sgu, c, wg, wb, wo, g, b)


def _rope_tables(s):
    pos = jnp.arange(s)
    row = (pos // GRID_W).astype(F32)
    col = (pos % GRID_W).astype(F32)
    axis_dim = HEAD_DIM // 2
    inv_freq = ROPE_THETA ** (-jnp.arange(0, axis_dim, 2, dtype=F32) / axis_dim)
    ang_r = row[:, None] * inv_freq[None, :]
    ang_c = col[:, None] * inv_freq[None, :]
    cos = jnp.concatenate([jnp.cos(ang_r)] * 2 + [jnp.cos(ang_c)] * 2, axis=1)
    sin = jnp.concatenate([-jnp.sin(ang_r), jnp.sin(ang_r), -jnp.sin(ang_c), jnp.sin(ang_c)], axis=1)
    return jnp.tile(cos, (1, 2)), jnp.tile(sin, (1, 2))


def _dft_angles(n):
    k = jnp.arange(n, dtype=jnp.int32)
    r = (k[:, None] * k[None, :]) % n
    return r.astype(F32) * (2.0 * math.pi / n)


def _position_dft(n):
    ang = _dft_angles(n)
    return jnp.cos(ang).astype(BF16), (-jnp.sin(ang)).astype(BF16)


def _factored_dft(s):
    n_low = s // N_RES
    idx = jnp.arange(n_low, dtype=jnp.int32)
    n = jnp.arange(N_RES, dtype=jnp.int32)[:, None, None] + N_RES * idx[None, None, :]
    ang = ((idx[None, :, None] * n) % s).astype(F32) * (2.0 * math.pi / s)
    c, sn = jnp.cos(ang), jnp.sin(ang)
    m1 = jnp.concatenate([jnp.concatenate([c, -sn], axis=2), jnp.concatenate([sn, c], axis=2)], axis=1)
    ang_r = _dft_angles(N_RES)
    eye = jnp.eye(K2_BLK, dtype=F32)
    g = jnp.stack([jnp.kron(jnp.cos(ang_r), eye), -jnp.kron(jnp.sin(ang_r), eye)])
    return m1.astype(BF16), g.astype(BF16)


def _channel_dft():
    ang = _dft_angles(FNET_GROUP_W)
    return jnp.concatenate([jnp.cos(ang), jnp.sin(ang)], axis=1).astype(BF16)


def kernel(x, c, ctx, c_ctx, w_mod, b_mod, w_ffn_up, w_ffn_down, ln_g, ln_b, w_in, attn_sink,
           sgu_w, sgu_b, sgu_ln_g, sgu_ln_b, w_gate, w_branch, w_out):
    depth = w_mod.shape[0]
    bsz, s, _ = x.shape
    n_ctx = ctx.shape[1]

    w_gu = _pack_up_call(w_ffn_up)
    w_d = _pack_down_call(w_ffn_down)
    w_in_b = w_in.astype(BF16)
    w_gate_b = w_gate.astype(BF16)
    w_branch_b = w_branch.astype(BF16)
    w_out_b = w_out.astype(BF16)
    sgu_w_b = sgu_w.astype(BF16)
    sgu_b_col = sgu_b[..., None]
    ln_g4 = ln_g[:, :, None, :]
    ln_b4 = ln_b[:, :, None, :]
    sgu_g3 = sgu_ln_g[:, None, :]
    sgu_b3 = sgu_ln_b[:, None, :]

    tables = _rope_tables(s)
    dft_c = _channel_dft()
    m1_lat, g_lat = _factored_dft(s)
    cn_ctx, nsn_ctx = _position_dft(n_ctx)

    rows = -(-(bsz + 1) // 8) * 8
    cc = jnp.zeros((rows, D_MODEL), F32).at[:bsz].set(c).at[bsz].set(c_ctx)
    mods = _mods_call(cc, w_mod, b_mod).reshape(depth, rows, N_MOD, D_MODEL)

    def ffn(xx, i, ctx_row, half, j):
        return _ffn_call(xx, mods, w_gu, w_d, ln_g4, ln_b4, layer=i, ctx_row=ctx_row, half=half, j=j,
                         tm=ROW_TILE, sub=SUB_TILE)

    def inproj(xx, i, ctx_row, tb):
        return _inproj_call(xx, mods, w_in_b, tb, sgu_w_b, sgu_b_col, sgu_g3, sgu_b3, dft_c,
                            layer=i, ctx_row=ctx_row, tm=ROW_TILE, sub=ROW_TILE)

    def merge(xx, i, ctx_row, a, sg, cf):
        return _merge_call(xx, mods, a, sg, cf, w_gate_b, w_branch_b, w_out_b, ln_g4, ln_b4,
                           layer=i, ctx_row=ctx_row, tm=ROW_TILE, sub=SUB_TILE)

    flat = lambda a: a.reshape(1, bsz * n_ctx, a.shape[-1])
    unflat = lambda a: a.reshape(bsz, n_ctx, a.shape[-1])
    x_lat, x_ctx = x, flat(ctx)
    for i in range(depth):
        last = i == depth - 1
        x_lat = ffn(x_lat, i, None, 0, 0)
        x_ctx = ffn(x_ctx, i, bsz, 0, 0)

        q_c, k_c, v_c, s_c, ab_c = inproj(x_ctx, i, bsz, None)
        q_c, k_c, v_c, ab_c = unflat(q_c), unflat(k_c), unflat(v_c), unflat(ab_c)
        q_l, k_l, v_l, s_l, ab_l = inproj(x_lat, i, None, tables)

        a_l = _attn_call(attn_sink, q_l, k_l, v_l, k_c, v_c, layer=i, band=True, tq=ATT_Q_TILE)
        f_l = _fourier_fact_call(m1_lat, g_lat, ab_l)
        x_lat = merge(x_lat, i, None, a_l, s_l, f_l)
        x_lat = ffn(x_lat, i, None, 1, 2)
        if not last:
            a_c = _attn_call(attn_sink, q_c, None, None, k_c, v_c, layer=i, band=False, tq=ATT_Q_TILE)
            f_c = _fourier_call(cn_ctx, nsn_ctx, ab_c, tk=ROW_TILE)
            x_ctx = merge(x_ctx, i, bsz, flat(a_c), s_c, flat(f_c))
            x_ctx = ffn(x_ctx, i, bsz, 1, 2)
    return x_lat
```

```python
import functools
import math

import jax
import jax.numpy as jnp
from jax import lax
from jax.experimental import pallas as pl
from jax.experimental.pallas import tpu as pltpu

F32 = jnp.float32
BF16 = jnp.bfloat16

D_MODEL = 1024
GRID_W = 64
HEAD_DIM = 64
N_Q_HEADS = 8
N_KV_HEADS = 2
GQA_GROUP = N_Q_HEADS // N_KV_HEADS
ATT_BLOCK = 128
ROPE_THETA = 10000.0
Q_W = N_Q_HEADS * HEAD_DIM
KV_W = N_KV_HEADS * HEAD_DIM
CHUNK = 128
SGU_GROUPS = 4
SGU_WIDTH = 512
FNET_GROUPS = 4
FNET_WIDTH = 512
FNET_GROUP_W = FNET_WIDTH // FNET_GROUPS
N_BRANCH = 3
D_FF = 2752
N_MOD = 9
MODEL_DEPTH = 4
ALPHA = (2 * MODEL_DEPTH) ** 0.25
LN_EPS = 1e-5
NEG_INF = -1e30

LANES = 128
MXU_W = 256
D_FF_PAD = -(-D_FF // MXU_W) * MXU_W
UP_OFF = D_FF % LANES
UP0 = D_FF - UP_OFF
assert UP0 + D_FF_PAD == 2 * D_FF
KV_DUP_W = 2 * KV_W
IN_Q, IN_K, IN_U, IN_Z, IN_F, IN_END = 0, 512, 768, 1280, 1792, 2304
VMEM_LIMIT = 56 * 1024 * 1024
ROW_TILE = 1024
SUB_TILE = 512
ATT_Q_TILE = 1024
PACK_ROWS = 256
MODS_COLS = 1024
N_RES = 32
BF16_ROWS = 16
K2_BLK = 8
STAGE_PITCH = 40
DEN_FLOOR = 2.0 ** -100
BOUND_SLACK = 1.01
LOG2E = math.log2(math.e)
Q_SCALE = HEAD_DIM ** -0.5 * LOG2E


def _params(n_axes):
    return pltpu.CompilerParams(dimension_semantics=("parallel",) * n_axes,
                                vmem_limit_bytes=VMEM_LIMIT)


def _const_spec(shape, lead=()):
    nl = len(lead)
    block = (None,) * nl + tuple(shape[nl:])
    index = tuple(lead) + (0,) * (len(shape) - nl)
    return pl.BlockSpec(block, lambda *_: index, pipeline_mode=pl.Buffered(1))


def _mods_spec(layer, ctx_row):
    if ctx_row is None:
        return pl.BlockSpec((None, None, N_MOD, D_MODEL), lambda bi, t: (layer, bi, 0, 0))
    return pl.BlockSpec((None, None, N_MOD, D_MODEL), lambda bi, t: (layer, ctx_row, 0, 0))


def _row_tile(s, pref):
    t = min(s, pref)
    while s % t:
        t -= LANES
    return t


def _dot(a, b):
    return jnp.dot(a, b, preferred_element_type=F32)


def _layer_norm(y, g, b):
    mu = jnp.mean(y, axis=-1, keepdims=True)
    d = y - mu
    var = jnp.mean(d * d, axis=-1, keepdims=True)
    return d * lax.rsqrt(var + LN_EPS) * g + b


def _mods_body(cc_ref, w_ref, b_ref, o_ref):
    s = cc_ref[...]
    s = (s * jax.nn.sigmoid(s)).astype(BF16)
    o_ref[...] = _dot(s, w_ref[...].astype(BF16)) + b_ref[...]


def _mods_call(cc, w_mod, b_mod):
    depth, _, n = w_mod.shape
    rows = cc.shape[0]
    tn = MODS_COLS
    return pl.pallas_call(
        _mods_body,
        grid=(depth, n // tn),
        in_specs=[
            pl.BlockSpec((rows, D_MODEL), lambda l, j: (0, 0)),
            pl.BlockSpec((None, D_MODEL, tn), lambda l, j: (l, 0, j)),
            pl.BlockSpec((None, 1, tn), lambda l, j: (l, 0, j)),
        ],
        out_specs=pl.BlockSpec((None, rows, tn), lambda l, j: (l, 0, j)),
        out_shape=jax.ShapeDtypeStruct((depth, rows, n), F32),
        compiler_params=_params(2),
        name="mods",
    )(cc, w_mod, b_mod.reshape(depth, 1, n))


def _pack_up_body(w_ref, o_ref):
    rows = w_ref.shape[0]
    lane = lax.broadcasted_iota(jnp.int32, (rows, D_FF_PAD), 1)
    real = lane < D_FF
    o_ref[:, :D_FF_PAD] = jnp.where(real, w_ref[:, :D_FF_PAD], 0.0).astype(BF16)
    up = pltpu.roll(w_ref[:, UP0:], D_FF_PAD - UP_OFF, axis=1)
    o_ref[:, D_FF_PAD:] = jnp.where(real, up, 0.0).astype(BF16)


def _pack_up_call(w_up):
    depth, two, d, n = w_up.shape
    tr = PACK_ROWS
    return pl.pallas_call(
        _pack_up_body,
        grid=(depth, two, d // tr),
        in_specs=[pl.BlockSpec((None, None, tr, n), lambda l, h, r: (l, h, r, 0))],
        out_specs=pl.BlockSpec((None, None, tr, 2 * D_FF_PAD), lambda l, h, r: (l, h, r, 0)),
        out_shape=jax.ShapeDtypeStruct((depth, two, d, 2 * D_FF_PAD), BF16),
        compiler_params=_params(3),
        name="pack_up",
    )(w_up)


def _pack_down_body(w_ref, o_ref):
    o_ref[:D_FF, :] = w_ref[...].astype(BF16)
    o_ref[D_FF:, :] = jnp.zeros((D_FF_PAD - D_FF, w_ref.shape[1]), BF16)


def _pack_down_call(w_down):
    depth, two, n, d = w_down.shape
    return pl.pallas_call(
        _pack_down_body,
        grid=(depth, two),
        in_specs=[pl.BlockSpec((None, None, n, d), lambda l, h: (l, h, 0, 0))],
        out_specs=pl.BlockSpec((None, None, D_FF_PAD, d), lambda l, h: (l, h, 0, 0)),
        out_shape=jax.ShapeDtypeStruct((depth, two, D_FF_PAD, d), BF16),
        compiler_params=_params(2),
        name="pack_down",
    )(w_down)


def _ffn_body(x_ref, m_ref, wgu_ref, wd_ref, g_ref, b_ref, o_ref, h_ref, *, j, sub):
    shift = m_ref[3 * j:3 * j + 1, :]
    scale = m_ref[3 * j + 1:3 * j + 2, :]
    gate = m_ref[3 * j + 2:3 * j + 3, :]
    for r in range(x_ref.shape[0] // sub):
        rows = slice(r * sub, (r + 1) * sub)
        x = x_ref[rows, :]
        xm = (x * (1 + scale) + shift).astype(BF16)
        for c in range(D_FF_PAD // MXU_W):
            sl = slice(c * MXU_W, (c + 1) * MXU_W)
            su = slice(D_FF_PAD + c * MXU_W, D_FF_PAD + (c + 1) * MXU_W)
            gg = _dot(xm, wgu_ref[:, sl])
            uu = _dot(xm, wgu_ref[:, su])
            h_ref[rows, sl] = (gg * jax.nn.sigmoid(gg) * uu).astype(BF16)
        acc = _dot(h_ref[rows, :], wd_ref[...])
        y = ALPHA * x + (0.5 * gate) * acc
        o_ref[rows, :] = _layer_norm(y, g_ref[...], b_ref[...])


def _ffn_call(x, mods, wgu, wd, g, b, *, layer, ctx_row, half, j, tm, sub):
    bsz, s, _ = x.shape
    tm = _row_tile(s, tm)
    sub = min(sub, tm)
    return pl.pallas_call(
        functools.partial(_ffn_body, j=j, sub=sub),
        grid=(bsz, s // tm),
        in_specs=[
            pl.BlockSpec((None, tm, D_MODEL), lambda bi, t: (bi, t, 0)),
            _mods_spec(layer, ctx_row),
            _const_spec(wgu.shape, (layer, half)), _const_spec(wd.shape, (layer, half)),
            _const_spec(g.shape, (layer, j)), _const_spec(b.shape, (layer, j)),
        ],
        out_specs=pl.BlockSpec((None, tm, D_MODEL), lambda bi, t: (bi, t, 0)),
        out_shape=jax.ShapeDtypeStruct(x.shape, F32),
        scratch_shapes=[pltpu.VMEM((tm, D_FF_PAD), BF16)],
        compiler_params=_params(2),
        name="ffn",
    )(x, mods, wgu, wd, g, b)


def _gelu(x):
    return jax.nn.gelu(x)


def _inproj_body(*refs, rope, sub):
    if rope:
        (x_ref, m_ref, w_ref, cos_ref, sin_ref, sw_ref, sb_ref, sg_ref, sbn_ref, dft_ref,
         q_ref, k_ref, v_ref, s_ref, ab_ref, stage_ref) = refs
    else:
        (x_ref, m_ref, w_ref, sw_ref, sb_ref, sg_ref, sbn_ref, dft_ref,
         q_ref, k_ref, v_ref, s_ref, ab_ref) = refs
    lane = lax.broadcasted_iota(jnp.int32, (sub, LANES), 1)
    low_half = lane < HEAD_DIM
    first = (lane % 32) < 16

    for r in range(x_ref.shape[0] // sub):
        rows = slice(r * sub, (r + 1) * sub)
        x = x_ref[rows, :]
        xm = (x * (1 + m_ref[4:5, :]) + m_ref[3:4, :]).astype(BF16)

        if rope:
            cos = cos_ref[rows, :]
            sin = sin_ref[rows, :]

            def rot(t, cos=cos, sin=sin):
                partner = jnp.where(first, pltpu.roll(t, LANES - 16, axis=1), pltpu.roll(t, 16, axis=1))
                return t * cos + partner * sin
        else:
            def rot(t):
                return t

        q = _dot(xm, w_ref[:, IN_Q:IN_K])
        for c in range(Q_W // LANES):
            sl = slice(c * LANES, (c + 1) * LANES)
            q_ref[rows, sl] = (rot(q[:, sl]) * Q_SCALE).astype(BF16)
        kv = _dot(xm, w_ref[:, IN_K:IN_U])
        for t, o_ref in ((rot(kv[:, :KV_W]), k_ref), (kv[:, KV_W:], v_ref)):
            swapped = pltpu.roll(t, HEAD_DIM, axis=1)
            o_ref[rows, :LANES] = jnp.where(low_half, t, swapped).astype(BF16)
            o_ref[rows, LANES:] = jnp.where(low_half, swapped, t).astype(BF16)

        u = _gelu(_dot(xm, w_ref[:, IN_U:IN_Z]))
        z = _gelu(_dot(xm, w_ref[:, IN_Z:IN_F]))
        zn = _layer_norm(z, sg_ref[...], sbn_ref[...]).astype(BF16)
        for ci in range(sub // CHUNK):
            rs = slice(ci * CHUNK, (ci + 1) * CHUNK)
            ro = slice(r * sub + ci * CHUNK, r * sub + (ci + 1) * CHUNK)
            for gi in range(SGU_GROUPS):
                cs = slice(gi * LANES, (gi + 1) * LANES)
                mixed = _dot(sw_ref[gi], zn[rs, cs]) + sb_ref[gi]
                s_ref[ro, cs] = (u[rs, cs] * mixed).astype(BF16)

        f = _dot(xm, w_ref[:, IN_F:IN_END]).astype(BF16)
        n_cb = 2 * FNET_WIDTH // LANES
        ab = [None] * n_cb
        for gi in range(FNET_GROUPS):
            cs_ = _dot(f[:, gi * FNET_GROUP_W:(gi + 1) * FNET_GROUP_W], dft_ref[...])
            ab[gi] = cs_[:, :FNET_GROUP_W]
            ab[FNET_GROUPS + gi] = cs_[:, FNET_GROUP_W:]
        if rope:
            n2 = sub // N_RES
            base = r * n2 * STAGE_PITCH
            for g in range(n2):
                for cb in range(n_cb):
                    stage_ref[cb, base + g * STAGE_PITCH:base + g * STAGE_PITCH + N_RES, :] = (
                        ab[cb][g * N_RES:(g + 1) * N_RES, :])
            for j in range(N_RES):
                cols = slice(j * 2 * FNET_WIDTH, (j + 1) * 2 * FNET_WIDTH)
                piece = [stage_ref[cb, pl.ds(base + j, n2, stride=STAGE_PITCH), :] for cb in range(n_cb)]
                ab_ref[r * n2:(r + 1) * n2, cols] = jnp.concatenate(piece, axis=1).astype(BF16)
        else:
            for cb in range(n_cb):
                ab_ref[rows, cb * LANES:(cb + 1) * LANES] = ab[cb].astype(BF16)


def _inproj_call(x, mods, w_in, tables, sw, sb, sg, sbn, dft_c, *, layer, ctx_row, tm, sub):
    bsz, s, _ = x.shape
    tm = _row_tile(s, tm)
    sub = min(sub, tm)
    rope = tables is not None
    row = lambda w: pl.BlockSpec((None, tm, w), lambda bi, t: (bi, t, 0))
    in_specs = [row(D_MODEL), _mods_spec(layer, ctx_row), _const_spec(w_in.shape, (layer,))]
    args = [x, mods, w_in]
    if rope:
        in_specs += [pl.BlockSpec((tm, LANES), lambda bi, t: (t, 0))] * 2
        args += list(tables)
    in_specs += [_const_spec(a.shape, (layer,)) for a in (sw, sb, sg, sbn)] + [_const_spec(dft_c.shape)]
    args += [sw, sb, sg, sbn, dft_c]
    widths = (Q_W, KV_DUP_W, KV_DUP_W, SGU_WIDTH, 2 * FNET_WIDTH)
    out_specs = [row(w) for w in widths]
    out_shape = [jax.ShapeDtypeStruct((bsz, s, w), BF16) for w in widths]
    scratch = []
    if rope:
        ab_w = N_RES * 2 * FNET_WIDTH
        out_specs[-1] = pl.BlockSpec((None, tm // N_RES, ab_w), lambda bi, t: (bi, t, 0))
        out_shape[-1] = jax.ShapeDtypeStruct((bsz, s // N_RES, ab_w), BF16)
        scratch = [pltpu.VMEM((2 * FNET_WIDTH // LANES, tm // N_RES * STAGE_PITCH, LANES), F32)]
    return pl.pallas_call(
        functools.partial(_inproj_body, rope=rope, sub=sub),
        grid=(bsz, s // tm),
        in_specs=in_specs,
        out_specs=out_specs,
        out_shape=out_shape,
        scratch_shapes=scratch,
        compiler_params=_params(2),
        name="inproj",
    )(*args)


def _attn_body(*refs, band, nq, nb, layer):
    if band:
        (sink_ref, q_ref, kp_ref, km_ref, kn_ref, vp_ref, vm_ref, vn_ref, kx_ref, vx_ref,
         o_ref, kw_ref, vw_ref) = refs
    else:
        sink_ref, q_ref, kx_ref, vx_ref, o_ref = refs
    blk = ATT_BLOCK
    n_ctx = kx_ref.shape[0]
    t = pl.program_id(1)
    row = lax.broadcasted_iota(jnp.int32, (blk, blk), 0)
    col = lax.broadcasted_iota(jnp.int32, (blk, blk), 1)
    lo = col < HEAD_DIM
    zero = jnp.zeros((blk, blk), BF16)
    nt_dims = (((1,), (1,)), ((), ()))
    if band:
        kw_ref[0:blk] = kp_ref[...]
        kw_ref[blk:(nq + 1) * blk] = km_ref[...]
        kw_ref[(nq + 1) * blk:(nq + 2) * blk] = kn_ref[...]
        vw_ref[0:blk] = vp_ref[...]
        vw_ref[blk:(nq + 1) * blk] = vm_ref[...]
        vw_ref[(nq + 1) * blk:(nq + 2) * blk] = vn_ref[...]

    def max_key_norm_sq(kh):
        cs = slice(kh * LANES, (kh + 1) * LANES)
        best = None
        for ref in ((kw_ref, kx_ref) if band else (kx_ref,)):
            kf = ref[:, cs].astype(F32)
            sq = jnp.max(jnp.sum(kf * kf, axis=1, keepdims=True), axis=0, keepdims=True)
            best = sq if best is None else jnp.maximum(best, sq)
        return 0.5 * best

    def run(exact):
        den_min = None
        k2 = None if exact else [max_key_norm_sq(kh) for kh in range(N_KV_HEADS)]
        for qi in range(nq):
            rows = slice(qi * blk, (qi + 1) * blk)
            if band:
                gi = t * nq + qi
                prev_ok = jnp.logical_and(col >= row, gi > 0)
                next_ok = jnp.logical_and(col <= row, gi < nb - 1)
            for kh in range(N_KV_HEADS):
                cs = slice(kh * LANES, (kh + 1) * LANES)
                qs = []
                for pr in range(GQA_GROUP // 2):
                    pair = q_ref[rows, (2 * kh + pr) * LANES:(2 * kh + pr + 1) * LANES]
                    qs.append(jnp.where(lo, pair, zero))
                    qs.append(jnp.where(lo, zero, pair))
                qstack = jnp.concatenate(qs, axis=0)
                if not exact:
                    qf = qstack.astype(F32)
                    q2 = jnp.sum(qf * qf, axis=1, keepdims=True)
                sc_ctx = lax.dot_general(qstack, kx_ref[:, cs], nt_dims, preferred_element_type=F32)
                if band:
                    win = slice(qi * blk, (qi + 3) * blk)
                    sc_band = lax.dot_general(qstack, kw_ref[win, cs], nt_dims, preferred_element_type=F32)
                ps, dens = [], []
                for g in range(GQA_GROUP):
                    hr = slice(g * blk, (g + 1) * blk)
                    pieces = [sc_ctx[hr, c * blk:(c + 1) * blk] for c in range(n_ctx // blk)]
                    if band:
                        pieces = [jnp.where(prev_ok, sc_band[hr, 0:blk], NEG_INF),
                                  sc_band[hr, blk:2 * blk],
                                  jnp.where(next_ok, sc_band[hr, 2 * blk:3 * blk], NEG_INF)] + pieces
                    sink = sink_ref[layer, kh * GQA_GROUP + g] * LOG2E
                    if exact:
                        mx = pieces[0]
                        for pc in pieces[1:]:
                            mx = jnp.maximum(mx, pc)
                        m = jnp.maximum(jnp.max(mx, axis=1, keepdims=True), sink)
                    else:
                        q2max = jnp.max(q2[hr], axis=0, keepdims=True)
                        m = jnp.maximum(jnp.sqrt(q2max * k2[kh]) * BOUND_SLACK, sink)
                    es = [jnp.exp2(pc - m) for pc in pieces]
                    tot = es[0]
                    for e in es[1:]:
                        tot = tot + e
                    den = jnp.sum(tot, axis=1, keepdims=True) + jnp.exp2(sink - m)
                    dens.append(den)
                    if not exact:
                        den_min = den if den_min is None else jnp.minimum(den_min, den)
                    ps.append(jnp.concatenate([e.astype(BF16) for e in es], axis=1))
                pall = jnp.concatenate(ps, axis=0)
                if band:
                    o = _dot(pall[:, :3 * blk], vw_ref[win, cs]) + _dot(pall[:, 3 * blk:], vx_ref[:, cs])
                else:
                    o = _dot(pall, vx_ref[:, cs])
                outs = [o[g * blk:(g + 1) * blk] / dens[g] for g in range(GQA_GROUP)]
                for pr in range(GQA_GROUP // 2):
                    pair = jnp.where(lo, outs[2 * pr], outs[2 * pr + 1])
                    o_ref[rows, (2 * kh + pr) * LANES:(2 * kh + pr + 1) * LANES] = pair.astype(BF16)
        return den_min

    den_min = run(exact=False)
    ok = jnp.min(den_min) >= DEN_FLOOR

    @pl.when(jnp.logical_not(ok))
    def _():
        run(exact=True)


def _attn_call(sink, q, k, v, kx, vx, *, layer, band, tq):
    bsz, s, _ = q.shape
    n_ctx = kx.shape[1]
    nb = s // ATT_BLOCK
    tq = _row_tile(s, tq)
    nq = tq // ATT_BLOCK
    qspec = pl.BlockSpec((None, tq, Q_W), lambda bi, i: (bi, i, 0))
    xspec = pl.BlockSpec((None, n_ctx, KV_DUP_W), lambda bi, i: (bi, 0, 0))
    sspec = pl.BlockSpec(memory_space=pltpu.SMEM)
    scratch = []
    if band:
        edge = lambda f: pl.BlockSpec((None, ATT_BLOCK, KV_DUP_W), lambda bi, i: (bi, f(i), 0))
        prev = edge(lambda i: jnp.maximum(i * nq - 1, 0))
        nxt = edge(lambda i: jnp.minimum((i + 1) * nq, nb - 1))
        main = pl.BlockSpec((None, tq, KV_DUP_W), lambda bi, i: (bi, i, 0))
        in_specs = [sspec, qspec] + [prev, main, nxt] * 2 + [xspec, xspec]
        args = (sink, q, k, k, k, v, v, v, kx, vx)
        scratch = [pltpu.VMEM(((nq + 2) * ATT_BLOCK, KV_DUP_W), BF16)] * 2
    else:
        in_specs = [sspec, qspec, xspec, xspec]
        args = (sink, q, kx, vx)
    return pl.pallas_call(
        functools.partial(_attn_body, band=band, nq=nq, nb=nb, layer=layer),
        grid=(bsz, s // tq),
        in_specs=in_specs,
        out_specs=qspec,
        out_shape=jax.ShapeDtypeStruct((bsz, s, Q_W), BF16),
        scratch_shapes=scratch,
        compiler_params=_params(2),
        name="attn_band" if band else "attn_ctx",
    )(*args)


def _fourier_body(cn_ref, nsn_ref, ab_ref, o_ref, *, scale):
    y = _dot(cn_ref[...], ab_ref[:, :FNET_WIDTH]) + _dot(nsn_ref[...], ab_ref[:, FNET_WIDTH:])
    o_ref[...] = (y * scale).astype(BF16)


def _fourier_call(cn, nsn, ab, *, tk):
    bsz, s, _ = ab.shape
    tk = _row_tile(s, tk)
    scale = 1.0 / math.sqrt(s * FNET_GROUP_W)
    return pl.pallas_call(
        functools.partial(_fourier_body, scale=scale),
        grid=(bsz, s // tk),
        in_specs=[
            pl.BlockSpec((tk, s), lambda bi, t: (t, 0)),
            pl.BlockSpec((tk, s), lambda bi, t: (t, 0)),
            pl.BlockSpec((None, s, 2 * FNET_WIDTH), lambda bi, t: (bi, 0, 0)),
        ],
        out_specs=pl.BlockSpec((None, tk, FNET_WIDTH), lambda bi, t: (bi, t, 0)),
        out_shape=jax.ShapeDtypeStruct((bsz, s, FNET_WIDTH), BF16),
        compiler_params=_params(2),
        name="fourier",
    )(cn, nsn, ab)


def _fourier_fact_body(ab_ref, m1_ref, g_ref, o_ref, t_ref, *, n_low, scale):
    w = FNET_WIDTH
    for j in range(N_RES):
        blk = ab_ref[:, j * 2 * w:(j + 1) * 2 * w]
        d = jnp.concatenate([blk[:, :w], blk[:, w:]], axis=0)
        r = _dot(m1_ref[j], d)
        rows = slice(j * n_low, (j + 1) * n_low)
        t_ref[rows, :w] = r[:n_low]
        t_ref[rows, w:] = r[n_low:]

    def stage2(kb):
        x = jnp.concatenate(
            [t_ref[j * n_low + kb * K2_BLK:j * n_low + (kb + 1) * K2_BLK, :] for j in range(N_RES)],
            axis=0).astype(BF16)
        return (_dot(g_ref[0], x[:, :w]) + _dot(g_ref[1], x[:, w:])) * scale

    pair = BF16_ROWS // K2_BLK
    for kb in range(0, n_low // K2_BLK, pair):
        ys = [stage2(kb + p) for p in range(pair)]
        for k1 in range(N_RES):
            piece = jnp.concatenate([y[k1 * K2_BLK:(k1 + 1) * K2_BLK] for y in ys], axis=0)
            o_ref[k1 * n_low + kb * K2_BLK:k1 * n_low + (kb + pair) * K2_BLK, :] = piece.astype(BF16)


def _fourier_fact_call(m1, g, ab_view):
    bsz, n_low, _ = ab_view.shape
    s = n_low * N_RES
    scale = 1.0 / math.sqrt(s * FNET_GROUP_W)
    return pl.pallas_call(
        functools.partial(_fourier_fact_body, n_low=n_low, scale=scale),
        grid=(bsz,),
        in_specs=[
            pl.BlockSpec((None, n_low, N_RES * 2 * FNET_WIDTH), lambda bi: (bi, 0, 0)),
            _const_spec(m1.shape), _const_spec(g.shape),
        ],
        out_specs=pl.BlockSpec((None, s, FNET_WIDTH), lambda bi: (bi, 0, 0)),
        out_shape=jax.ShapeDtypeStruct((bsz, s, FNET_WIDTH), BF16),
        scratch_shapes=[pltpu.VMEM((s, 2 * FNET_WIDTH), F32)],
        compiler_params=_params(1),
        name="fourier_fact",
    )(ab_view, m1, g)


def _merge_body(x_ref, m_ref, a_ref, s_ref, c_ref, wg_ref, wb_ref, wo_ref, g_ref, b_ref, o_ref, mg_ref, *, sub):
    for st in range(x_ref.shape[0] // sub):
        rows = slice(st * sub, (st + 1) * sub)
        x = x_ref[rows, :]
        h = (x * (1 + m_ref[4:5, :]) + m_ref[3:4, :]).astype(BF16)
        branches = (a_ref[rows, :], s_ref[rows, :], c_ref[rows, :])
        for n in range(D_MODEL // MXU_W):
            sl = slice(n * MXU_W, (n + 1) * MXU_W)
            acc = None
            for r in range(N_BRANCH):
                t = jax.nn.sigmoid(_dot(h, wg_ref[r, :, sl])) * _dot(branches[r], wb_ref[r, :, sl])
                acc = t if acc is None else acc + t
            mg_ref[rows, sl] = acc.astype(BF16)
        y = _dot(mg_ref[rows, :], wo_ref[...])
        o_ref[rows, :] = _layer_norm(ALPHA * x + m_ref[5:6, :] * y, g_ref[...], b_ref[...])


def _merge_call(x, mods, a, sgu, c, wg, wb, wo, g, b, *, layer, ctx_row, tm, sub):
    bsz, s, _ = x.shape
    tm = _row_tile(s, tm)
    sub = min(sub, tm)
    row = lambda w: pl.BlockSpec((None, tm, w), lambda bi, t: (bi, t, 0))
    return pl.pallas_call(
        functools.partial(_merge_body, sub=sub),
        grid=(bsz, s // tm),
        in_specs=[
            row(D_MODEL), _mods_spec(layer, ctx_row),
            row(Q_W), row(SGU_WIDTH), row(FNET_WIDTH),
            _const_spec(wg.shape, (layer,)), _const_spec(wb.shape, (layer,)), _const_spec(wo.shape, (layer,)),
            _const_spec(g.shape, (layer, 1)), _const_spec(b.shape, (layer, 1)),
        ],
        out_specs=row(D_MODEL),
        out_shape=jax.ShapeDtypeStruct(x.shape, F32),
        scratch_shapes=[pltpu.VMEM((tm, D_MODEL), BF16)],
        compiler_params=_params(2),
        name="merge",
    )(x, mods, a, sgu, c, wg, wb, wo, g, b)


def _rope_tables(s):
    pos = jnp.arange(s)
    row = (pos // GRID_W).astype(F32)
    col = (pos % GRID_W).astype(F32)
    axis_dim = HEAD_DIM // 2
    inv_freq = ROPE_THETA ** (-jnp.arange(0, axis_dim, 2, dtype=F32) / axis_dim)
    ang_r = row[:, None] * inv_freq[None, :]
    ang_c = col[:, None] * inv_freq[None, :]
    cos = jnp.concatenate([jnp.cos(ang_r)] * 2 + [jnp.cos(ang_c)] * 2, axis=1)
    sin = jnp.concatenate([-jnp.sin(ang_r), jnp.sin(ang_r), -jnp.sin(ang_c), jnp.sin(ang_c)], axis=1)
    return jnp.tile(cos, (1, 2)), jnp.tile(sin, (1, 2))


def _dft_angles(n):
    k = jnp.arange(n, dtype=jnp.int32)
    r = (k[:, None] * k[None, :]) % n
    return r.astype(F32) * (2.0 * math.pi / n)


def _position_dft(n):
    ang = _dft_angles(n)
    return jnp.cos(ang).astype(BF16), (-jnp.sin(ang)).astype(BF16)


def _factored_dft(s):
    n_low = s // N_RES
    idx = jnp.arange(n_low, dtype=jnp.int32)
    n = jnp.arange(N_RES, dtype=jnp.int32)[:, None, None] + N_RES * idx[None, None, :]
    ang = ((idx[None, :, None] * n) % s).astype(F32) * (2.0 * math.pi / s)
    c, sn = jnp.cos(ang), jnp.sin(ang)
    m1 = jnp.concatenate([jnp.concatenate([c, -sn], axis=2), jnp.concatenate([sn, c], axis=2)], axis=1)
    ang_r = _dft_angles(N_RES)
    eye = jnp.eye(K2_BLK, dtype=F32)
    g = jnp.stack([jnp.kron(jnp.cos(ang_r), eye), -jnp.kron(jnp.sin(ang_r), eye)])
    return m1.astype(BF16), g.astype(BF16)


def _channel_dft():
    ang = _dft_angles(FNET_GROUP_W)
    return jnp.concatenate([jnp.cos(ang), jnp.sin(ang)], axis=1).astype(BF16)


def kernel(x, c, ctx, c_ctx, w_mod, b_mod, w_ffn_up, w_ffn_down, ln_g, ln_b, w_in, attn_sink,
           sgu_w, sgu_b, sgu_ln_g, sgu_ln_b, w_gate, w_branch, w_out):
    depth = w_mod.shape[0]
    bsz, s, _ = x.shape
    n_ctx = ctx.shape[1]

    w_gu = _pack_up_call(w_ffn_up)
    w_d = _pack_down_call(w_ffn_down)
    w_in_b = w_in.astype(BF16)
    w_gate_b = w_gate.astype(BF16)
    w_branch_b = w_branch.astype(BF16)
    w_out_b = w_out.astype(BF16)
    sgu_w_b = sgu_w.astype(BF16)
    sgu_b_col = sgu_b[..., None]
    ln_g4 = ln_g[:, :, None, :]
    ln_b4 = ln_b[:, :, None, :]
    sgu_g3 = sgu_ln_g[:, None, :]
    sgu_b3 = sgu_ln_b[:, None, :]

    tables = _rope_tables(s)
    dft_c = _channel_dft()
    m1_lat, g_lat = _factored_dft(s)
    cn_ctx, nsn_ctx = _position_dft(n_ctx)

    rows = -(-(bsz + 1) // 8) * 8
    cc = jnp.zeros((rows, D_MODEL), F32).at[:bsz].set(c).at[bsz].set(c_ctx)
    mods = _mods_call(cc, w_mod, b_mod).reshape(depth, rows, N_MOD, D_MODEL)

    def ffn(xx, i, ctx_row, half, j):
        return _ffn_call(xx, mods, w_gu, w_d, ln_g4, ln_b4, layer=i, ctx_row=ctx_row, half=half, j=j,
                         tm=ROW_TILE, sub=SUB_TILE)

    def inproj(xx, i, ctx_row, tb):
        return _inproj_call(xx, mods, w_in_b, tb, sgu_w_b, sgu_b_col, sgu_g3, sgu_b3, dft_c,
                            layer=i, ctx_row=ctx_row, tm=ROW_TILE, sub=ROW_TILE)

    def merge(xx, i, ctx_row, a, sg, cf):
        return _merge_call(xx, mods, a, sg, cf, w_gate_b, w_branch_b, w_out_b, ln_g4, ln_b4,
                           layer=i, ctx_row=ctx_row, tm=ROW_TILE, sub=SUB_TILE)

    flat = lambda a: a.reshape(1, bsz * n_ctx, a.shape[-1])
    unflat = lambda a: a.reshape(bsz, n_ctx, a.shape[-1])
    x_lat, x_ctx = x, flat(ctx)
    for i in range(depth):
        last = i == depth - 1
        x_lat = ffn(x_lat, i, None, 0, 0)
        x_ctx = ffn(x_ctx, i, bsz, 0, 0)

        q_c, k_c, v_c, s_c, ab_c = inproj(x_ctx, i, bsz, None)
        q_c, k_c, v_c, ab_c = unflat(q_c), unflat(k_c), unflat(v_c), unflat(ab_c)
        q_l, k_l, v_l, s_l, ab_l = inproj(x_lat, i, None, tables)

        a_l = _attn_call(attn_sink, q_l, k_l, v_l, k_c, v_c, layer=i, band=True, tq=ATT_Q_TILE)
        f_l = _fourier_fact_call(m1_lat, g_lat, ab_l)
        x_lat = merge(x_lat, i, None, a_l, s_l, f_l)
        x_lat = ffn(x_lat, i, None, 1, 2)
        if not last:
            a_c = _attn_call(attn_sink, q_c, None, None, k_c, v_c, layer=i, band=False, tq=ATT_Q_TILE)
            f_c = _fourier_call(cn_ctx, nsn_ctx, ab_c, tk=ROW_TILE)
            x_ctx = merge(x_ctx, i, bsz, flat(a_c), s_c, flat(f_c))
            x_ctx = ffn(x_ctx, i, bsz, 1, 2)
    return x_lat
```
